```python
import jax, jax.numpy as jnp
from jax import lax
import numpy as np

D_MODEL = 1024
BATCH = 16
SEQ = 256
DEPTH = 2
DEC_BATCH = 4
DEC_SEQ = 4096
PAST_LEN = 256

GRID_W = 64
N_MIXERS = 2
N_A = (DEPTH + 1) // 2
N_B = DEPTH // 2
HG_HEADS = 8
HG_DK = D_MODEL // HG_HEADS
HG_DV = D_MODEL // HG_HEADS
SCAN_CHUNK = 64
MIX_CHUNK = 128
ROWS_PER_CHUNK = MIX_CHUNK // GRID_W
CM_GROUPS = 8
CM_GROUP_W = D_MODEL // CM_GROUPS
D_FF = 4 * D_MODEL
N_MOD = 6
EPS = 1e-6

kernel_name = "hybrid_hgrn2_chunkmlp_diffusion_step"


def rmsnorm(x, g):
    xf = x.astype(jnp.float32)
    y = xf * lax.rsqrt(jnp.mean(xf * xf, axis=-1, keepdims=True) + EPS)
    return (y * g.astype(jnp.float32)).astype(x.dtype)


def layernorm(x, g, b):
    xf = x.astype(jnp.float32)
    mu = jnp.mean(xf, axis=-1, keepdims=True)
    var = jnp.mean(jnp.square(xf - mu), axis=-1, keepdims=True)
    y = (xf - mu) * lax.rsqrt(var + EPS) * g.astype(jnp.float32) + b.astype(jnp.float32)
    return y.astype(x.dtype)


def adaln(cvec, w, b):
    m = jax.nn.silu(cvec) @ w + b
    return [t[:, None, :] for t in jnp.split(m, N_MOD, axis=-1)]


def gla_scan(q, k, v, g, s0):
    Bn, L, H, _ = q.shape
    DV = v.shape[-1]
    n = L // SCAN_CHUNK

    def to_chunks(a):
        return a.reshape(Bn, n, SCAN_CHUNK, a.shape[2], a.shape[3]).transpose(1, 0, 3, 2, 4)

    mask = jnp.tril(jnp.ones((SCAN_CHUNK, SCAN_CHUNK), dtype=bool))[:, :, None]

    def step(S, inp):
        qc, kc, vc, gc = inp
        b = jnp.cumsum(gc, axis=2)
        o = jnp.einsum('bhtk,bhkv->bhtv', qc * jnp.exp(b), S)
        diff = jnp.where(mask, b[:, :, :, None, :] - b[:, :, None, :, :], -jnp.inf)
        att = jnp.einsum('bhtk,bhsk,bhtsk->bhts', qc, kc, jnp.exp(diff))
        o = o + jnp.einsum('bhts,bhsv->bhtv', att, vc)
        b_last = b[:, :, -1:, :]
        S = jnp.exp(b_last[:, :, 0, :])[..., None] * S + jnp.einsum(
            'bhsk,bhsv->bhkv', kc * jnp.exp(b_last - b), vc)
        return S, o

    S, o = lax.scan(step, s0.astype(jnp.float32), (to_chunks(q), to_chunks(k), to_chunks(v), to_chunks(g)))
    o = o.transpose(1, 0, 3, 2, 4).reshape(Bn, L, H, DV)
    return o, S


def hgrn2_mixer(h, s0_fwd, s0_bwd, w_in, lb, onorm_g, w_out):
    Bn, L, _ = h.shape
    z = h @ w_in
    zq, zff, zfb, zi, zg = jnp.split(z, 5, axis=-1)
    q = jax.nn.silu(zq.astype(jnp.float32)).reshape(Bn, L, HG_HEADS, HG_DK)
    v = zi.astype(jnp.float32).reshape(Bn, L, HG_HEADS, HG_DV)

    def gates(zf, lbd):
        f = lbd + (1.0 - lbd) * jax.nn.sigmoid(zf.astype(jnp.float32))
        return (1.0 - f).reshape(Bn, L, HG_HEADS, HG_DK), jnp.log(f).reshape(Bn, L, HG_HEADS, HG_DK)

    k_f, g_f = gates(zff, lb[0])
    k_b, g_b = gates(zfb, lb[1])
    o_f, s_f = gla_scan(q, k_f, v, g_f, s0_fwd)
    rev = lambda a: jnp.flip(a, axis=1)
    o_b, s_b = gla_scan(rev(q), rev(k_b), rev(v), rev(g_b), s0_bwd)
    o = o_f + rev(o_b)
    o = o * lax.rsqrt(jnp.mean(o * o, axis=-1, keepdims=True) + EPS) * onorm_g.astype(jnp.float32)
    o = o.reshape(Bn, L, D_MODEL) * jax.nn.silu(zg.astype(jnp.float32))
    return o.astype(h.dtype) @ w_out, s_f, s_b


def chunk_mlp_mixer(h, n_chunks, w_in, ln_g, ln_b, w_s, b_s, w_out):
    Bn, L, _ = h.shape
    z = jax.nn.gelu(h @ w_in)
    u, v = jnp.split(z, 2, axis=-1)
    v = layernorm(v, ln_g, ln_b).reshape(Bn, n_chunks, MIX_CHUNK, CM_GROUPS, CM_GROUP_W)
    s = jnp.einsum('gpq,bnqgc->bnpgc', w_s, v) + b_s.T[:, :, None]
    return (u * s.reshape(Bn, L, D_MODEL)) @ w_out


def sqrelu_mlp(h, w1, w2):
    return jnp.square(jax.nn.relu(h @ w1)) @ w2


def setup_inputs(seed: int = 0) -> dict:
    key = jax.random.key(seed)
    ks = jax.random.split(key, 24)
    nrm = lambda k, shape, s=1.0: jax.random.normal(k, shape, jnp.float32) * s
    D = D_MODEL
    return {
        "x_prompt": nrm(ks[0], (BATCH, SEQ, D)),
        "x_sample": nrm(ks[1], (DEC_BATCH, DEC_SEQ, D)),
        "state_hgrn": nrm(ks[2], (DEC_BATCH, N_A, 2, HG_HEADS, HG_DK, HG_DV), 0.5),
        "c": nrm(ks[3], (DEC_BATCH, D)),
        "c_ctx": nrm(ks[4], (D,)),
        "ada_w": nrm(ks[5], (DEPTH, D, N_MOD * D), 0.5 * D ** -0.5),
        "ada_b": nrm(ks[6], (DEPTH, N_MOD * D), 0.02),
        "norm_mix_g": 1.0 + nrm(ks[7], (DEPTH, D), 0.02),
        "norm_mlp_g": 1.0 + nrm(ks[8], (DEPTH, D), 0.02),
        "mlp_w1": nrm(ks[9], (DEPTH, D, D_FF), D ** -0.5),
        "mlp_w2": nrm(ks[10], (DEPTH, D_FF, D), D_FF ** -0.5),
        "hgrn_w_in": nrm(ks[11], (N_A, D, 5 * D), D ** -0.5),
        "hgrn_lb_logits": nrm(ks[12], (N_A + 1, 2, HG_HEADS * HG_DK)),
        "hgrn_onorm_g": 1.0 + nrm(ks[13], (N_A, HG_HEADS, HG_DV), 0.02),
        "hgrn_w_out": nrm(ks[14], (N_A, D, D), D ** -0.5),
        "cm_w_in": nrm(ks[15], (N_B, D, 2 * D), D ** -0.5),
        "cm_ln_g": 1.0 + nrm(ks[16], (N_B, D), 0.02),
        "cm_ln_b": nrm(ks[17], (N_B, D), 0.02),
        "cm_w_s": nrm(ks[18], (N_B, CM_GROUPS, MIX_CHUNK, MIX_CHUNK), MIX_CHUNK ** -0.5),
        "cm_b_s": 1.0 + nrm(ks[19], (N_B, CM_GROUPS, MIX_CHUNK), 0.02),
        "cm_w_out": nrm(ks[20], (N_B, D, D), D ** -0.5),
        "final_norm_g": 1.0 + nrm(ks[21], (D,), 0.02),
    }


def reference(x_prompt, x_sample, state_hgrn, c, c_ctx, ada_w, ada_b, norm_mix_g, norm_mlp_g,
              mlp_w1, mlp_w2, hgrn_w_in, hgrn_lb_logits, hgrn_onorm_g, hgrn_w_out,
              cm_w_in, cm_ln_g, cm_ln_b, cm_w_s, cm_b_s, cm_w_out, final_norm_g):
    ctx = x_prompt
    lat = x_sample
    n_ctx_batch, ctx_len, _ = ctx.shape
    rows = lat.shape[1] // GRID_W
    ctx_cond = c_ctx[None, :]
    lb_all = jnp.cumsum(jax.nn.softmax(hgrn_lb_logits.astype(jnp.float32), axis=0), axis=0)
    zero_state = jnp.zeros((n_ctx_batch, HG_HEADS, HG_DK, HG_DV), jnp.float32)
    new_states = []
    for i in range(DEPTH):
        j = i // N_MIXERS
        sh1c, sc1c, gt1c, sh2c, sc2c, gt2c = adaln(ctx_cond, ada_w[i], ada_b[i])
        sh1l, sc1l, gt1l, sh2l, sc2l, gt2l = adaln(c, ada_w[i], ada_b[i])
        h_ctx = rmsnorm(ctx, norm_mix_g[i]) * (1.0 + sc1c) + sh1c
        h_lat = rmsnorm(lat, norm_mix_g[i]) * (1.0 + sc1l) + sh1l
        if i % N_MIXERS == 0:
            o_ctx, s_f, s_b = hgrn2_mixer(h_ctx, zero_state, zero_state, hgrn_w_in[j], lb_all[j],
                                          hgrn_onorm_g[j], hgrn_w_out[j])
            o_lat, _, _ = hgrn2_mixer(h_lat, state_hgrn[:, j, 0], state_hgrn[:, j, 1], hgrn_w_in[j],
                                      lb_all[j], hgrn_onorm_g[j], hgrn_w_out[j])
            new_states.append(jnp.stack([s_f, s_b], axis=1).astype(x_prompt.dtype))
        else:
            o_ctx = chunk_mlp_mixer(h_ctx, ctx_len // MIX_CHUNK, cm_w_in[j], cm_ln_g[j], cm_ln_b[j],
                                    cm_w_s[j], cm_b_s[j], cm_w_out[j])
            o_lat = chunk_mlp_mixer(h_lat, rows // ROWS_PER_CHUNK, cm_w_in[j], cm_ln_g[j], cm_ln_b[j],
                                    cm_w_s[j], cm_b_s[j], cm_w_out[j])
        ctx = ctx + gt1c * o_ctx
        lat = lat + gt1l * o_lat
        h_ctx = rmsnorm(ctx, norm_mlp_g[i]) * (1.0 + sc2c) + sh2c
        h_lat = rmsnorm(lat, norm_mlp_g[i]) * (1.0 + sc2l) + sh2l
        ctx = ctx + gt2c * sqrelu_mlp(h_ctx, mlp_w1[i], mlp_w2[i])
        lat = lat + gt2l * sqrelu_mlp(h_lat, mlp_w1[i], mlp_w2[i])
    y_prompt = rmsnorm(ctx, final_norm_g)
    y_sample = rmsnorm(lat, final_norm_g)
    new_state_hgrn = jnp.stack(new_states, axis=1)
    return (y_prompt, y_sample, new_state_hgrn)
```

```python
import functools

import jax
import jax.numpy as jnp
from jax import lax
from jax.experimental import pallas as pl
from jax.experimental.pallas import tpu as pltpu

F32 = jnp.float32
BF16 = jnp.bfloat16

EPS = 1e-6
N_MOD = 6
N_MIXERS = 2
HEADS = 8
HEAD_W = 128
PAIR_W = 2 * HEAD_W
MIX_CHUNK = 128
CM_GROUPS = 8
SCAN_BLOCK = 64
SCAN_CHUNK = 2 * SCAN_BLOCK
SUBLANES = 8
COND_ROWS = SUBLANES
SAFE_LOG_DECAY = -80.0
VMEM_LIMIT = 56 * 1024 * 1024


def _sigmoid(x):
    return 1.0 / (1.0 + jnp.exp(-x))


def _silu(x):
    return x * _sigmoid(x)


def _gelu_tanh(x):
    return 0.5 * x * (1.0 + jnp.tanh(0.7978845608028654 * (x + 0.044715 * (x * x * x))))


def _rms(x, g):
    return x * lax.rsqrt(jnp.mean(x * x, axis=-1, keepdims=True) + EPS) * g


def _dot(a, b):
    return jnp.dot(a, b, preferred_element_type=F32)


def _dot_nt(a, b):
    return lax.dot_general(a, b, (((1,), (1,)), ((), ())), preferred_element_type=F32)


def _dot_tn(a, b):
    return lax.dot_general(a, b, (((0,), (0,)), ((), ())), preferred_element_type=F32)


def _ada_kernel(cond_ref, w_ref, b_ref, out_ref):
    a = _silu(cond_ref[...]).astype(BF16)
    out_ref[...] = _dot(a, w_ref[...].astype(BF16)) + b_ref[...]


def _ada_call(cond, ada_w, ada_b):
    depth, d, _ = ada_w.shape
    return pl.pallas_call(
        _ada_kernel,
        grid=(depth, N_MOD),
        in_specs=[
            pl.BlockSpec((COND_ROWS, d), lambda i, n: (0, 0)),
            pl.BlockSpec((None, d, d), lambda i, n: (i, 0, n)),
            pl.BlockSpec((None, 1, d), lambda i, n: (i, 0, n)),
        ],
        out_specs=pl.BlockSpec((None, COND_ROWS, d), lambda i, n: (i, 0, n)),
        out_shape=jax.ShapeDtypeStruct((depth, COND_ROWS, N_MOD * d), F32),
        compiler_params=pltpu.CompilerParams(dimension_semantics=("arbitrary", "arbitrary")),
        name="ada",
    )(cond, ada_w, ada_b.reshape(depth, 1, N_MOD * d))


def _scan_masks(rev):
    c = SCAN_CHUNK
    t = lax.broadcasted_iota(jnp.int32, (c, 2 * c), 0)
    s = lax.broadcasted_iota(jnp.int32, (c, 2 * c), 1) & (c - 1)
    t_hi = t >= SCAN_BLOCK
    s_hi = s >= SCAN_BLOCK
    causal = (s >= t) if rev else (s <= t)
    diag = (t_hi == s_hi) & causal
    cross = (jnp.logical_not(t_hi) & s_hi) if rev else (t_hi & jnp.logical_not(s_hi))
    return diag, cross


def _scan_chunk(q, k, g, v, st_ref, attd_ref, od_ref, slow_refs, rev):
    c, d = q.shape
    n_pairs = d // PAIR_W
    row = lax.broadcasted_iota(jnp.int32, (c, 1), 0)
    hi = row >= SCAN_BLOCK
    near = hi if rev else jnp.logical_not(hi)
    far = jnp.logical_not(near)

    t = lax.broadcasted_iota(jnp.int32, (c, c), 0)
    s = lax.broadcasted_iota(jnp.int32, (c, c), 1)
    same = (t >= SCAN_BLOCK) == (s >= SCAN_BLOCK)
    tri = jnp.where(same & ((s >= t) if rev else (s <= t)), 1.0, 0.0).astype(BF16)
    g_hi = g.astype(BF16)
    g_lo = (g - g_hi.astype(F32)).astype(BF16)
    b_in = _dot(tri, g_hi) + _dot(tri, g_lo)

    if rev:
        tot_lo, tot_hi = b_in[0:1], b_in[SCAN_BLOCK:SCAN_BLOCK + 1]
        tot_near, tot_far = tot_hi, tot_lo
    else:
        tot_lo, tot_hi = b_in[SCAN_BLOCK - 1:SCAN_BLOCK], b_in[c - 1:c]
        tot_near, tot_far = tot_lo, tot_hi
    tot_own = jnp.where(hi, tot_hi, tot_lo)

    p = jnp.exp(b_in)
    e = jnp.exp(tot_own - b_in)
    qh = q * p
    ke = k * e
    qi = (qh * jnp.where(far, jnp.exp(tot_near), 1.0)).astype(BF16)
    ks = (ke * jnp.where(near, jnp.exp(tot_far), 1.0)).astype(BF16)
    dec = jnp.exp(tot_near + tot_far)
    qh16 = qh.astype(BF16)
    ke16 = ke.astype(BF16)
    v16 = v.astype(BF16)

    diag_mask, cross_mask = _scan_masks(rev)
    left = lax.broadcasted_iota(jnp.int32, (1, PAIR_W), 1) < HEAD_W
    bd_mask = (lax.broadcasted_iota(jnp.int32, (PAIR_W, PAIR_W), 0) < HEAD_W) == (
        lax.broadcasted_iota(jnp.int32, (PAIR_W, PAIR_W), 1) < HEAD_W)

    def by_head(a):
        zero = jnp.zeros_like(a)
        return jnp.concatenate([jnp.where(left, a, zero), jnp.where(left, zero, a)], axis=0)

    fast = jnp.min(jnp.minimum(tot_lo, tot_hi)) >= SAFE_LOG_DECAY

    @pl.when(fast)
    def _():
        kr16 = (k * jnp.exp(-b_in)).astype(BF16)
        for pr in range(n_pairs):
            sl = slice(pr * PAIR_W, (pr + 1) * PAIR_W)
            att = _dot_nt(qh16[:, sl], by_head(kr16[:, sl]))
            attd_ref[pr] = jnp.where(diag_mask, att, 0.0)

    @pl.when(jnp.logical_not(fast))
    def _():
        q_s, k_s, v_s, b_s = slow_refs
        attd_ref[...] = jnp.zeros_like(attd_ref)
        q_s[...] = q
        k_s[...] = k
        v_s[...] = v
        b_s[...] = b_in

        def body(grp, carry):
            t0 = pl.multiple_of(grp * SUBLANES, SUBLANES)
            q8 = q_s[pl.ds(t0, SUBLANES), :]
            b8 = b_s[pl.ds(t0, SUBLANES), :]
            out_rows = []
            for r in range(SUBLANES):
                ti = t0 + r
                prod = q8[r:r + 1] * k_s[...] * jnp.exp(jnp.minimum(b8[r:r + 1] - b_s[...], 0.0))
                valid = ((row >= SCAN_BLOCK) == (ti >= SCAN_BLOCK)) & ((row >= ti) if rev else (row <= ti))
                heads = []
                for h in range(d // HEAD_W):
                    hs = slice(h * HEAD_W, (h + 1) * HEAD_W)
                    w = jnp.where(valid, jnp.sum(prod[:, hs], axis=-1, keepdims=True), 0.0)
                    heads.append(jnp.sum(w * v_s[:, hs], axis=0, keepdims=True))
                out_rows.append(jnp.concatenate(heads, axis=1))
            od_ref[pl.ds(t0, SUBLANES), :] = jnp.concatenate(out_rows, axis=0)
            return carry

        lax.fori_loop(0, c // SUBLANES, body, 0)

    outs = []
    for pr in range(n_pairs):
        sl = slice(pr * PAIR_W, (pr + 1) * PAIR_W)
        v2 = by_head(v16[:, sl])
        attx = _dot_nt(qh16[:, sl], by_head(ke16[:, sl]))
        att = (jnp.where(cross_mask, attx, 0.0) + attd_ref[pr]).astype(BF16)
        st = st_ref[pr]
        outs.append(_dot(att, v2) + _dot_nt(qi[:, sl], st.astype(BF16)))
        upd = _dot_tn(v16[:, sl], ks[:, sl])
        st_ref[pr] = st * dec[:, sl] + jnp.where(bd_mask, upd, 0.0)
    return jnp.concatenate(outs, axis=1), fast


def _state_init(st_ref, s0_ref):
    st_ref[...] = jnp.zeros_like(st_ref)
    if s0_ref is not None:
        for h in range(HEADS):
            o = (h % 2) * HEAD_W
            st_ref[h // 2, o:o + HEAD_W, o:o + HEAD_W] = s0_ref[h].T


def _state_write(st_ref, out_ref):
    for h in range(HEADS):
        o = (h % 2) * HEAD_W
        out_ref[h] = st_ref[h // 2, o:o + HEAD_W, o:o + HEAD_W].T


def _lower_bound(lbl_ref, layer, direction):
    n_layers = lbl_ref.shape[0] // 2
    rows = [lbl_ref[2 * i + direction:2 * i + direction + 1, :] for i in range(n_layers)]
    m = functools.reduce(jnp.maximum, rows)
    ex = [jnp.exp(r - m) for r in rows]
    return sum(ex[:layer + 1]) / sum(ex)


def _hg_fwd_kernel(*refs, has_s0, layer):
    if has_s0:
        x_ref, mod_ref, g_ref, w_ref, lbl_ref, s0_ref = refs[:6]
        rest = refs[6:]
    else:
        x_ref, mod_ref, g_ref, w_ref, lbl_ref = refs[:5]
        s0_ref = None
        rest = refs[5:]
    of_ref, q_ref, v_ref, fb_ref, gate_ref, sf_ref = rest[:6]
    st_ref, attd_ref, od_ref = rest[6:9]
    slow_refs = rest[9:13]
    j = pl.program_id(1)
    d = x_ref.shape[-1]

    @pl.when(j == 0)
    def _():
        _state_init(st_ref, s0_ref)

    sh, sc = mod_ref[0:1, :], mod_ref[1:2, :]
    h = (_rms(x_ref[...], g_ref[...]) * (1.0 + sc) + sh).astype(BF16)
    zq = _dot(h, w_ref[:, 0 * d:1 * d])
    zff = _dot(h, w_ref[:, 1 * d:2 * d])
    zfb = _dot(h, w_ref[:, 2 * d:3 * d])
    zi = _dot(h, w_ref[:, 3 * d:4 * d])
    zg = _dot(h, w_ref[:, 4 * d:5 * d])
    lb_f = _lower_bound(lbl_ref, layer, 0)
    lb_b = _lower_bound(lbl_ref, layer, 1)
    q = _silu(zq)
    f_f = lb_f + (1.0 - lb_f) * _sigmoid(zff)
    fb_ref[...] = lb_b + (1.0 - lb_b) * _sigmoid(zfb)
    q_ref[...] = q.astype(BF16)
    v_ref[...] = zi.astype(BF16)
    gate_ref[...] = _silu(zg).astype(BF16)

    o, fast = _scan_chunk(q, 1.0 - f_f, jnp.log(f_f), zi, st_ref, attd_ref, od_ref, slow_refs, rev=False)
    of_ref[...] = o

    @pl.when(jnp.logical_not(fast))
    def _():
        of_ref[...] += od_ref[...]

    @pl.when(j == pl.num_programs(1) - 1)
    def _():
        _state_write(st_ref, sf_ref)


def _const_spec(shape):
    nd = len(shape)
    return pl.BlockSpec(shape, lambda *_: (0,) * nd, pipeline_mode=pl.Buffered(1))


def _scan_scratch(d):
    c = SCAN_CHUNK
    return [
        pltpu.VMEM((d // PAIR_W, PAIR_W, PAIR_W), F32),
        pltpu.VMEM((d // PAIR_W, c, 2 * c), F32),
        pltpu.VMEM((c, d), F32),
        pltpu.VMEM((c, d), F32), pltpu.VMEM((c, d), F32),
        pltpu.VMEM((c, d), F32), pltpu.VMEM((c, d), F32),
    ]


def _hg_fwd_call(x, mod, mod_row0, mod_stride, g, w_in, lbl, state, layer_a):
    bn, l, d = x.shape
    c = SCAN_CHUNK
    n = l // c
    tok = lambda b, j: (b, j, 0)
    in_specs = [
        pl.BlockSpec((None, c, d), tok),
        pl.BlockSpec((None, N_MOD, d), lambda b, j: (mod_row0 + mod_stride * b, 0, 0)),
        _const_spec((1, d)),
        _const_spec(w_in.shape),
        _const_spec(lbl.shape),
    ]
    args = [x, mod, g, w_in, lbl]
    if state is not None:
        in_specs.append(pl.BlockSpec((None, None, None, HEADS, HEAD_W, HEAD_W),
                                     lambda b, j: (b, layer_a, 0, 0, 0, 0)))
        args.append(state)
    st_spec = pl.BlockSpec((None, HEADS, HEAD_W, HEAD_W), lambda b, j: (b, 0, 0, 0))
    return pl.pallas_call(
        functools.partial(_hg_fwd_kernel, has_s0=state is not None, layer=layer_a),
        grid=(bn, n),
        in_specs=in_specs,
        out_specs=[pl.BlockSpec((None, c, d), tok)] * 5 + [st_spec],
        out_shape=[
            jax.ShapeDtypeStruct((bn, l, d), F32),
            jax.ShapeDtypeStruct((bn, l, d), BF16),
            jax.ShapeDtypeStruct((bn, l, d), BF16),
            jax.ShapeDtypeStruct((bn, l, d), F32),
            jax.ShapeDtypeStruct((bn, l, d), BF16),
            jax.ShapeDtypeStruct((bn, HEADS, HEAD_W, HEAD_W), F32),
        ],
        scratch_shapes=_scan_scratch(d),
        compiler_params=pltpu.CompilerParams(
            dimension_semantics=("arbitrary", "arbitrary"), vmem_limit_bytes=VMEM_LIMIT),
        name="hg_fwd",
    )(*args)


def _mlp_tail(x1, mod_ref, g2, w1_ref, w2_ref):
    sh2, sc2, gt2 = mod_ref[3:4, :], mod_ref[4:5, :], mod_ref[5:6, :]
    h2 = (_rms(x1, g2) * (1.0 + sc2) + sh2).astype(BF16)
    a = jnp.maximum(_dot(h2, w1_ref[...]), 0.0)
    return x1 + gt2 * _dot((a * a).astype(BF16), w2_ref[...])


def _hg_bwd_kernel(*refs, has_s0, final):
    x_ref, mod_ref, of_ref, q_ref, v_ref, fb_ref, gate_ref, on_ref, g2_ref, fg_ref = refs[:10]
    wo_ref, w1_ref, w2_ref = refs[10:13]
    if has_s0:
        s0_ref = refs[13]
        rest = refs[14:]
    else:
        s0_ref = None
        rest = refs[13:]
    y_ref, sb_ref = rest[:2]
    st_ref, attd_ref, od_ref = rest[2:5]
    slow_refs = rest[5:9]
    j = pl.program_id(1)
    d = x_ref.shape[-1]

    @pl.when(j == 0)
    def _():
        _state_init(st_ref, s0_ref)

    f_b = fb_ref[...]
    o_b, fast = _scan_chunk(q_ref[...].astype(F32), 1.0 - f_b, jnp.log(f_b), v_ref[...].astype(F32),
                               st_ref, attd_ref, od_ref, slow_refs, rev=True)
    y_ref[...] = of_ref[...] + o_b

    @pl.when(jnp.logical_not(fast))
    def _():
        y_ref[...] += od_ref[...]

    o = y_ref[...]
    parts = []
    for h in range(d // HEAD_W):
        oh = o[:, h * HEAD_W:(h + 1) * HEAD_W]
        parts.append(oh * lax.rsqrt(jnp.mean(oh * oh, axis=-1, keepdims=True) + EPS))
    o = jnp.concatenate(parts, axis=1) * on_ref[...] * gate_ref[...].astype(F32)
    gt1 = mod_ref[2:3, :]
    x1 = x_ref[...] + gt1 * _dot(o.astype(BF16), wo_ref[...])
    x2 = _mlp_tail(x1, mod_ref, g2_ref[...], w1_ref, w2_ref)
    y_ref[...] = _rms(x2, fg_ref[...]) if final else x2

    @pl.when(j == pl.num_programs(1) - 1)
    def _():
        _state_write(st_ref, sb_ref)


def _hg_bwd_call(x, mod, mod_row0, mod_stride, stash, onorm, g2, fg, w_out, w1, w2, state, layer_a, final):
    bn, l, d = x.shape
    c = SCAN_CHUNK
    n = l // c
    tok = lambda b, j: (b, n - 1 - j, 0)
    o_f, q, v, f_b, gate = stash
    in_specs = [
        pl.BlockSpec((None, c, d), tok),
        pl.BlockSpec((None, N_MOD, d), lambda b, j: (mod_row0 + mod_stride * b, 0, 0)),
    ] + [pl.BlockSpec((None, c, d), tok)] * 5 + [
        _const_spec((1, d)), _const_spec((1, d)), _const_spec((1, d)),
        _const_spec(w_out.shape), _const_spec(w1.shape), _const_spec(w2.shape),
    ]
    args = [x, mod, o_f, q, v, f_b, gate, onorm, g2, fg, w_out, w1, w2]
    if state is not None:
        in_specs.append(pl.BlockSpec((None, None, None, HEADS, HEAD_W, HEAD_W),
                                     lambda b, j: (b, layer_a, 1, 0, 0, 0)))
        args.append(state)
    st_spec = pl.BlockSpec((None, HEADS, HEAD_W, HEAD_W), lambda b, j: (b, 0, 0, 0))
    return pl.pallas_call(
        functools.partial(_hg_bwd_kernel, has_s0=state is not None, final=final),
        grid=(bn, n),
        in_specs=in_specs,
        out_specs=[pl.BlockSpec((None, c, d), tok), st_spec],
        out_shape=[
            jax.ShapeDtypeStruct((bn, l, d), F32),
            jax.ShapeDtypeStruct((bn, HEADS, HEAD_W, HEAD_W), F32),
        ],
        scratch_shapes=_scan_scratch(d),
        compiler_params=pltpu.CompilerParams(
            dimension_semantics=("arbitrary", "arbitrary"), vmem_limit_bytes=VMEM_LIMIT),
        name="hg_bwd",
    )(*args)


def _cm_kernel(x_ref, mod_ref, g1_ref, g2_ref, fg_ref, win_ref, lng_ref, lnb_ref, ws_ref, bst_ref,
               wo_ref, w1_ref, w2_ref, y_ref, *, final):
    tb, d = x_ref.shape
    n_chunks = tb // MIX_CHUNK
    gw = d // CM_GROUPS
    x = x_ref[...]
    sh, sc, gt1 = mod_ref[0:1, :], mod_ref[1:2, :], mod_ref[2:3, :]
    h = (_rms(x, g1_ref[...]) * (1.0 + sc) + sh).astype(BF16)
    u = _gelu_tanh(_dot(h, win_ref[:, :d]))
    vv = _gelu_tanh(_dot(h, win_ref[:, d:]))
    mu = jnp.mean(vv, axis=-1, keepdims=True)
    vc = vv - mu
    var = jnp.mean(vc * vc, axis=-1, keepdims=True)
    vn = (vc * lax.rsqrt(var + EPS) * lng_ref[...] + lnb_ref[...]).astype(BF16)
    per_group = []
    for g in range(CM_GROUPS):
        rhs = jnp.concatenate(
            [vn[ch * MIX_CHUNK:(ch + 1) * MIX_CHUNK, g * gw:(g + 1) * gw] for ch in range(n_chunks)], axis=1)
        per_group.append(_dot(ws_ref[g], rhs) + bst_ref[:, g:g + 1])
    s = jnp.concatenate(
        [jnp.concatenate([pg[:, ch * gw:(ch + 1) * gw] for pg in per_group], axis=1) for ch in range(n_chunks)],
        axis=0)
    x1 = x + gt1 * _dot((u * s).astype(BF16), wo_ref[...])
    x2 = _mlp_tail(x1, mod_ref, g2_ref[...], w1_ref, w2_ref)
    y_ref[...] = _rms(x2, fg_ref[...]) if final else x2


def _cm_call(x, mod, mod_row0, mod_stride, g1, g2, fg, w_in, ln_g, ln_b, w_s, b_st, w_out, w1, w2, final, tb):
    bn, l, d = x.shape
    tok = lambda b, j: (b, j, 0)
    consts = [g1, g2, fg, w_in, ln_g, ln_b, w_s, b_st, w_out, w1, w2]
    return pl.pallas_call(
        functools.partial(_cm_kernel, final=final),
        grid=(bn, l // tb),
        in_specs=[
            pl.BlockSpec((None, tb, d), tok),
            pl.BlockSpec((None, N_MOD, d), lambda b, j: (mod_row0 + mod_stride * b, 0, 0)),
        ] + [_const_spec(a.shape) for a in consts],
        out_specs=pl.BlockSpec((None, tb, d), tok),
        out_shape=jax.ShapeDtypeStruct((bn, l, d), F32),
        compiler_params=pltpu.CompilerParams(
            dimension_semantics=("arbitrary", "arbitrary"), vmem_limit_bytes=VMEM_LIMIT),
        name="cm",
    )(x, mod, *consts)


def kernel(x_prompt, x_sample, state_hgrn, c, c_ctx, ada_w, ada_b, norm_mix_g, norm_mlp_g, mlp_w1, mlp_w2,
           hgrn_w_in, hgrn_lb_logits, hgrn_onorm_g, hgrn_w_out, cm_w_in, cm_ln_g, cm_ln_b, cm_w_s, cm_b_s,
           cm_w_out, final_norm_g):
    depth, d, _ = ada_w.shape
    n_lat = c.shape[0]
    assert d % PAIR_W == 0 and d // HEAD_W == HEADS and 1 + n_lat <= COND_ROWS
    assert x_prompt.shape[1] % SCAN_CHUNK == 0 and x_sample.shape[1] % SCAN_CHUNK == 0

    cond = jnp.concatenate([c_ctx[None, :], c, jnp.zeros((COND_ROWS - 1 - n_lat, d), F32)], axis=0)
    mods = _ada_call(cond, ada_w, ada_b).reshape(depth, COND_ROWS, N_MOD, d)

    row = lambda a: a.reshape(1, d)
    lbl = hgrn_lb_logits.reshape(-1, d)
    fg = row(final_norm_g)
    ctx, lat = x_prompt, x_sample
    new_states = []
    for i in range(depth):
        j = i // N_MIXERS
        final = i == depth - 1
        w1 = mlp_w1[i].astype(BF16)
        w2 = mlp_w2[i].astype(BF16)
        g1, g2 = row(norm_mix_g[i]), row(norm_mlp_g[i])
        if i % N_MIXERS == 0:
            w_in = hgrn_w_in[j].astype(BF16)
            w_out = hgrn_w_out[j].astype(BF16)
            onorm = row(hgrn_onorm_g[j])
            *stash, s_f = _hg_fwd_call(ctx, mods[i], 0, 0, g1, w_in, lbl, None, j)
            ctx, s_b = _hg_bwd_call(ctx, mods[i], 0, 0, stash, onorm, g2, fg, w_out, w1, w2, None, j, final)
            new_states.append(jnp.stack([s_f, s_b], axis=1))
            *stash, _ = _hg_fwd_call(lat, mods[i], 1, 1, g1, w_in, lbl, state_hgrn, j)
            lat, _ = _hg_bwd_call(lat, mods[i], 1, 1, stash, onorm, g2, fg, w_out, w1, w2, state_hgrn, j, final)
        else:
            cm_args = (g1, g2, fg, cm_w_in[j].astype(BF16), row(cm_ln_g[j]), row(cm_ln_b[j]),
                       cm_w_s[j].astype(BF16), cm_b_s[j].T, cm_w_out[j].astype(BF16), w1, w2, final)
            ctx = _cm_call(ctx, mods[i], 0, 0, *cm_args, tb=2 * MIX_CHUNK)
            lat = _cm_call(lat, mods[i], 1, 1, *cm_args, tb=2 * MIX_CHUNK)
    new_state = jnp.stack(new_states, axis=1).astype(x_prompt.dtype)
    return ctx, lat, new_state
```

```python
import functools

import jax
import jax.numpy as jnp
from jax import lax
from jax.experimental import pallas as pl
from jax.experimental.pallas import tpu as pltpu

F32 = jnp.float32
BF16 = jnp.bfloat16

EPS = 1e-6
N_MOD = 6
N_MIXERS = 2
HEADS = 8
HEAD_W = 128
PAIR_W = 2 * HEAD_W
MIX_CHUNK = 128
CM_GROUPS = 8
SCAN_BLOCK = 64
SCAN_CHUNK = 2 * SCAN_BLOCK
SUBLANES = 8
COND_ROWS = SUBLANES
SAFE_LOG_DECAY = -80.0
VMEM_LIMIT = 56 * 1024 * 1024


def _sigmoid(x):
    return 1.0 / (1.0 + jnp.exp(-x))


def _silu(x):
    return x * _sigmoid(x)


def _gelu_tanh(x):
    return 0.5 * x * (1.0 + jnp.tanh(0.7978845608028654 * (x + 0.044715 * (x * x * x))))


def _rms(x, g):
    return x * lax.rsqrt(jnp.mean(x * x, axis=-1, keepdims=True) + EPS) * g


def _dot(a, b):
    return jnp.dot(a, b, preferred_element_type=F32)


def _dot_nt(a, b):
    return lax.dot_general(a, b, (((1,), (1,)), ((), ())), preferred_element_type=F32)


def _dot_tn(a, b):
    return lax.dot_general(a, b, (((0,), (0,)), ((), ())), preferred_element_type=F32)


def _ada_kernel(cond_ref, w_ref, b_ref, out_ref):
    a = _silu(cond_ref[...]).astype(BF16)
    out_ref[...] = _dot(a, w_ref[...].astype(BF16)) + b_ref[...]


def _ada_call(cond, ada_w, ada_b):
    depth, d, _ = ada_w.shape
    return pl.pallas_call(
        _ada_kernel,
        grid=(depth, N_MOD),
        in_specs=[
            pl.BlockSpec((COND_ROWS, d), lambda i, n: (0, 0)),
            pl.BlockSpec((None, d, d), lambda i, n: (i, 0, n)),
            pl.BlockSpec((None, 1, d), lambda i, n: (i, 0, n)),
        ],
        out_specs=pl.BlockSpec((None, COND_ROWS, d), lambda i, n: (i, 0, n)),
        out_shape=jax.ShapeDtypeStruct((depth, COND_ROWS, N_MOD * d), F32),
        compiler_params=pltpu.CompilerParams(dimension_semantics=("arbitrary", "arbitrary")),
        name="ada",
    )(cond, ada_w, ada_b.reshape(depth, 1, N_MOD * d))


def _scan_masks(rev):
    c = SCAN_CHUNK
    t = lax.broadcasted_iota(jnp.int32, (c, 2 * c), 0)
    s = lax.broadcasted_iota(jnp.int32, (c, 2 * c), 1) & (c - 1)
    t_hi = t >= SCAN_BLOCK
    s_hi = s >= SCAN_BLOCK
    causal = (s >= t) if rev else (s <= t)
    diag = (t_hi == s_hi) & causal
    cross = (jnp.logical_not(t_hi) & s_hi) if rev else (t_hi & jnp.logical_not(s_hi))
    return diag, cross


def _scan_chunk(q, k, g, v, st_ref, attd_ref, od_ref, slow_refs, rev):
    c, d = q.shape
    n_pairs = d // PAIR_W
    row = lax.broadcasted_iota(jnp.int32, (c, 1), 0)
    hi = row >= SCAN_BLOCK
    near = hi if rev else jnp.logical_not(hi)
    far = jnp.logical_not(near)

    t = lax.broadcasted_iota(jnp.int32, (c, c), 0)
    s = lax.broadcasted_iota(jnp.int32, (c, c), 1)
    same = (t >= SCAN_BLOCK) == (s >= SCAN_BLOCK)
    tri = jnp.where(same & ((s >= t) if rev else (s <= t)), 1.0, 0.0).astype(BF16)
    g_hi = g.astype(BF16)
    g_lo = (g - g_hi.astype(F32)).astype(BF16)
    b_in = _dot(tri, g_hi) + _dot(tri, g_lo)

    half = SCAN_BLOCK // 2
    if rev:
        tot_lo, tot_hi = b_in[0:1], b_in[SCAN_BLOCK:SCAN_BLOCK + 1]
        mid_lo, mid_hi = b_in[half:half + 1], b_in[SCAN_BLOCK + half:SCAN_BLOCK + half + 1]
        tot_near, tot_far = tot_hi, tot_lo
    else:
        tot_lo, tot_hi = b_in[SCAN_BLOCK - 1:SCAN_BLOCK], b_in[c - 1:c]
        mid_lo, mid_hi = b_in[half - 1:half], b_in[SCAN_BLOCK + half - 1:SCAN_BLOCK + half]
        tot_near, tot_far = tot_lo, tot_hi
    tot_own = jnp.where(hi, tot_hi, tot_lo)
    mid_own = jnp.where(hi, mid_hi, mid_lo)

    p = jnp.exp(b_in)
    e = jnp.exp(tot_own - b_in)
    qh = q * p
    ke = k * e
    qi = (qh * jnp.where(far, jnp.exp(tot_near), 1.0)).astype(BF16)
    ks = (ke * jnp.where(near, jnp.exp(tot_far), 1.0)).astype(BF16)
    dec = jnp.exp(tot_near + tot_far)
    qh16 = qh.astype(BF16)
    ke16 = ke.astype(BF16)
    v16 = v.astype(BF16)

    diag_mask, cross_mask = _scan_masks(rev)
    left = lax.broadcasted_iota(jnp.int32, (1, PAIR_W), 1) < HEAD_W
    bd_mask = (lax.broadcasted_iota(jnp.int32, (PAIR_W, PAIR_W), 0) < HEAD_W) == (
        lax.broadcasted_iota(jnp.int32, (PAIR_W, PAIR_W), 1) < HEAD_W)

    def by_head(a):
        zero = jnp.zeros_like(a)
        return jnp.concatenate([jnp.where(left, a, zero), jnp.where(left, zero, a)], axis=0)

    half_sums = jnp.minimum(jnp.minimum(mid_lo, tot_lo - mid_lo), jnp.minimum(mid_hi, tot_hi - mid_hi))
    fast = jnp.min(half_sums) >= SAFE_LOG_DECAY

    @pl.when(fast)
    def _():
        qm16 = (q * jnp.exp(b_in - mid_own)).astype(BF16)
        km16 = (k * jnp.exp(mid_own - b_in)).astype(BF16)
        for pr in range(n_pairs):
            sl = slice(pr * PAIR_W, (pr + 1) * PAIR_W)
            att = _dot_nt(qm16[:, sl], by_head(km16[:, sl]))
            attd_ref[pr] = jnp.where(diag_mask, att, 0.0)

    @pl.when(jnp.logical_not(fast))
    def _():
        q_s, k_s, v_s, b_s = slow_refs
        attd_ref[...] = jnp.zeros_like(attd_ref)
        q_s[...] = q
        k_s[...] = k
        v_s[...] = v
        b_s[...] = b_in

        def body(grp, carry):
            t0 = pl.multiple_of(grp * SUBLANES, SUBLANES)
            q8 = q_s[pl.ds(t0, SUBLANES), :]
            b8 = b_s[pl.ds(t0, SUBLANES), :]
            out_rows = []
            for r in range(SUBLANES):
                ti = t0 + r
                prod = q8[r:r + 1] * k_s[...] * jnp.exp(jnp.minimum(b8[r:r + 1] - b_s[...], 0.0))
                valid = ((row >= SCAN_BLOCK) == (ti >= SCAN_BLOCK)) & ((row >= ti) if rev else (row <= ti))
                heads = []
                for h in range(d // HEAD_W):
                    hs = slice(h * HEAD_W, (h + 1) * HEAD_W)
                    w = jnp.where(valid, jnp.sum(prod[:, hs], axis=-1, keepdims=True), 0.0)
                    heads.append(jnp.sum(w * v_s[:, hs], axis=0, keepdims=True))
                out_rows.append(jnp.concatenate(heads, axis=1))
            od_ref[pl.ds(t0, SUBLANES), :] = jnp.concatenate(out_rows, axis=0)
            return carry

        lax.fori_loop(0, c // SUBLANES, body, 0)

    outs = []
    for pr in range(n_pairs):
        sl = slice(pr * PAIR_W, (pr + 1) * PAIR_W)
        v2 = by_head(v16[:, sl])
        attx = _dot_nt(qh16[:, sl], by_head(ke16[:, sl]))
        att = (jnp.where(cross_mask, attx, 0.0) + attd_ref[pr]).astype(BF16)
        st = st_ref[pr]
        outs.append(_dot(att, v2) + _dot_nt(qi[:, sl], st.astype(BF16)))
        upd = _dot_tn(v16[:, sl], ks[:, sl])
        st_ref[pr] = st * dec[:, sl] + jnp.where(bd_mask, upd, 0.0)
    return jnp.concatenate(outs, axis=1), fast


def _state_init(st_ref, s0_ref):
    st_ref[...] = jnp.zeros_like(st_ref)
    if s0_ref is not None:
        for h in range(HEADS):
            o = (h % 2) * HEAD_W
            st_ref[h // 2, o:o + HEAD_W, o:o + HEAD_W] = s0_ref[h].T


def _state_write(st_ref, out_ref):
    for h in range(HEADS):
        o = (h % 2) * HEAD_W
        out_ref[h] = st_ref[h // 2, o:o + HEAD_W, o:o + HEAD_W].T


def _lower_bound(lbl_ref, layer, direction):
    n_layers = lbl_ref.shape[0] // 2
    rows = [lbl_ref[2 * i + direction:2 * i + direction + 1, :] for i in range(n_layers)]
    m = functools.reduce(jnp.maximum, rows)
    ex = [jnp.exp(r - m) for r in rows]
    return sum(ex[:layer + 1]) / sum(ex)


def _hg_fwd_kernel(*refs, has_s0, layer):
    if has_s0:
        x_ref, mod_ref, g_ref, w_ref, lbl_ref, s0_ref = refs[:6]
        rest = refs[6:]
    else:
        x_ref, mod_ref, g_ref, w_ref, lbl_ref = refs[:5]
        s0_ref = None
        rest = refs[5:]
    of_ref, q_ref, v_ref, fb_ref, gate_ref, sf_ref = rest[:6]
    st_ref, attd_ref, od_ref = rest[6:9]
    slow_refs = rest[9:13]
    j = pl.program_id(1)
    d = x_ref.shape[-1]

    @pl.when(j == 0)
    def _():
        _state_init(st_ref, s0_ref)

    sh, sc = mod_ref[0:1, :], mod_ref[1:2, :]
    h = (_rms(x_ref[...], g_ref[...]) * (1.0 + sc) + sh).astype(BF16)
    zq = _dot(h, w_ref[:, 0 * d:1 * d])
    zff = _dot(h, w_ref[:, 1 * d:2 * d])
    zfb = _dot(h, w_ref[:, 2 * d:3 * d])
    zi = _dot(h, w_ref[:, 3 * d:4 * d])
    zg = _dot(h, w_ref[:, 4 * d:5 * d])
    lb_f = _lower_bound(lbl_ref, layer, 0)
    lb_b = _lower_bound(lbl_ref, layer, 1)
    q = _silu(zq)
    f_f = lb_f + (1.0 - lb_f) * _sigmoid(zff)
    fb_ref[...] = lb_b + (1.0 - lb_b) * _sigmoid(zfb)
    q_ref[...] = q.astype(BF16)
    v_ref[...] = zi.astype(BF16)
    gate_ref[...] = _silu(zg).astype(BF16)

    o, fast = _scan_chunk(q, 1.0 - f_f, jnp.log(f_f), zi, st_ref, attd_ref, od_ref, slow_refs, rev=False)
    of_ref[...] = o

    @pl.when(jnp.logical_not(fast))
    def _():
        of_ref[...] += od_ref[...]

    @pl.when(j == pl.num_programs(1) - 1)
    def _():
        _state_write(st_ref, sf_ref)


def _const_spec(shape):
    nd = len(shape)
    return pl.BlockSpec(shape, lambda *_: (0,) * nd, pipeline_mode=pl.Buffered(1))


def _scan_scratch(d):
    c = SCAN_CHUNK
    return [
        pltpu.VMEM((d // PAIR_W, PAIR_W, PAIR_W), F32),
        pltpu.VMEM((d // PAIR_W, c, 2 * c), F32),
        pltpu.VMEM((c, d), F32),
        pltpu.VMEM((c, d), F32), pltpu.VMEM((c, d), F32),
        pltpu.VMEM((c, d), F32), pltpu.VMEM((c, d), F32),
    ]


def _hg_fwd_call(x, mod, mod_row0, mod_stride, g, w_in, lbl, state, layer_a):
    bn, l, d = x.shape
    c = SCAN_CHUNK
    n = l // c
    tok = lambda b, j: (b, j, 0)
    in_specs = [
        pl.BlockSpec((None, c, d), tok),
        pl.BlockSpec((None, N_MOD, d), lambda b, j: (mod_row0 + mod_stride * b, 0, 0)),
        _const_spec((1, d)),
        _const_spec(w_in.shape),
        _const_spec(lbl.shape),
    ]
    args = [x, mod, g, w_in, lbl]
    if state is not None:
        in_specs.append(pl.BlockSpec((None, None, None, HEADS, HEAD_W, HEAD_W),
                                     lambda b, j: (b, layer_a, 0, 0, 0, 0)))
        args.append(state)
    st_spec = pl.BlockSpec((None, HEADS, HEAD_W, HEAD_W), lambda b, j: (b, 0, 0, 0))
    return pl.pallas_call(
        functools.partial(_hg_fwd_kernel, has_s0=state is not None, layer=layer_a),
        grid=(bn, n),
        in_specs=in_specs,
        out_specs=[pl.BlockSpec((None, c, d), tok)] * 5 + [st_spec],
        out_shape=[
            jax.ShapeDtypeStruct((bn, l, d), F32),
            jax.ShapeDtypeStruct((bn, l, d), BF16),
            jax.ShapeDtypeStruct((bn, l, d), BF16),
            jax.ShapeDtypeStruct((bn, l, d), F32),
            jax.ShapeDtypeStruct((bn, l, d), BF16),
            jax.ShapeDtypeStruct((bn, HEADS, HEAD_W, HEAD_W), F32),
        ],
        scratch_shapes=_scan_scratch(d),
        compiler_params=pltpu.CompilerParams(
            dimension_semantics=("arbitrary", "arbitrary"), vmem_limit_bytes=VMEM_LIMIT),
        name="hg_fwd",
    )(*args)


def _mlp_tail(x1, mod_ref, g2, w1_ref, w2_ref):
    sh2, sc2, gt2 = mod_ref[3:4, :], mod_ref[4:5, :], mod_ref[5:6, :]
    h2 = (_rms(x1, g2) * (1.0 + sc2) + sh2).astype(BF16)
    a = jnp.maximum(_dot(h2, w1_ref[...]), 0.0)
    return x1 + gt2 * _dot((a * a).astype(BF16), w2_ref[...])


def _hg_bwd_kernel(*refs, has_s0, final):
    x_ref, mod_ref, of_ref, q_ref, v_ref, fb_ref, gate_ref, on_ref, g2_ref, fg_ref = refs[:10]
    wo_ref, w1_ref, w2_ref = refs[10:13]
    if has_s0:
        s0_ref = refs[13]
        rest = refs[14:]
    else:
        s0_ref = None
        rest = refs[13:]
    y_ref, sb_ref = rest[:2]
    st_ref, attd_ref, od_ref = rest[2:5]
    slow_refs = rest[5:9]
    j = pl.program_id(1)
    d = x_ref.shape[-1]

    @pl.when(j == 0)
    def _():
        _state_init(st_ref, s0_ref)

    f_b = fb_ref[...]
    o_b, fast = _scan_chunk(q_ref[...].astype(F32), 1.0 - f_b, jnp.log(f_b), v_ref[...].astype(F32),
                               st_ref, attd_ref, od_ref, slow_refs, rev=True)
    y_ref[...] = of_ref[...] + o_b

    @pl.when(jnp.logical_not(fast))
    def _():
        y_ref[...] += od_ref[...]

    o = y_ref[...]
    parts = []
    for h in range(d // HEAD_W):
        oh = o[:, h * HEAD_W:(h + 1) * HEAD_W]
        parts.append(oh * lax.rsqrt(jnp.mean(oh * oh, axis=-1, keepdims=True) + EPS))
    o = jnp.concatenate(parts, axis=1) * on_ref[...] * gate_ref[...].astype(F32)
    gt1 = mod_ref[2:3, :]
    x1 = x_ref[...] + gt1 * _dot(o.astype(BF16), wo_ref[...])
    x2 = _mlp_tail(x1, mod_ref, g2_ref[...], w1_ref, w2_ref)
    y_ref[...] = _rms(x2, fg_ref[...]) if final else x2

    @pl.when(j == pl.num_programs(1) - 1)
    def _():
        _state_write(st_ref, sb_ref)


def _hg_bwd_call(x, mod, mod_row0, mod_stride, stash, onorm, g2, fg, w_out, w1, w2, state, layer_a, final):
    bn, l, d = x.shape
    c = SCAN_CHUNK
    n = l // c
    tok = lambda b, j: (b, n - 1 - j, 0)
    o_f, q, v, f_b, gate = stash
    in_specs = [
        pl.BlockSpec((None, c, d), tok),
        pl.BlockSpec((None, N_MOD, d), lambda b, j: (mod_row0 + mod_stride * b, 0, 0)),
    ] + [pl.BlockSpec((None, c, d), tok)] * 5 + [
        _const_spec((1, d)), _const_spec((1, d)), _const_spec((1, d)),
        _const_spec(w_out.shape), _const_spec(w1.shape), _const_spec(w2.shape),
    ]
    args = [x, mod, o_f, q, v, f_b, gate, onorm, g2, fg, w_out, w1, w2]
    if state is not None:
        in_specs.append(pl.BlockSpec((None, None, None, HEADS, HEAD_W, HEAD_W),
                                     lambda b, j: (b, layer_a, 1, 0, 0, 0)))
        args.append(state)
    st_spec = pl.BlockSpec((None, HEADS, HEAD_W, HEAD_W), lambda b, j: (b, 0, 0, 0))
    return pl.pallas_call(
        functools.partial(_hg_bwd_kernel, has_s0=state is not None, final=final),
        grid=(bn, n),
        in_specs=in_specs,
        out_specs=[pl.BlockSpec((None, c, d), tok), st_spec],
        out_shape=[
            jax.ShapeDtypeStruct((bn, l, d), F32),
            jax.ShapeDtypeStruct((bn, HEADS, HEAD_W, HEAD_W), F32),
        ],
        scratch_shapes=_scan_scratch(d),
        compiler_params=pltpu.CompilerParams(
            dimension_semantics=("arbitrary", "arbitrary"), vmem_limit_bytes=VMEM_LIMIT),
        name="hg_bwd",
    )(*args)


def _cm_kernel(x_ref, mod_ref, g1_ref, g2_ref, fg_ref, win_ref, lng_ref, lnb_ref, ws_ref, bst_ref,
               wo_ref, w1_ref, w2_ref, y_ref, *, final):
    tb, d = x_ref.shape
    n_chunks = tb // MIX_CHUNK
    gw = d // CM_GROUPS
    x = x_ref[...]
    sh, sc, gt1 = mod_ref[0:1, :], mod_ref[1:2, :], mod_ref[2:3, :]
    h = (_rms(x, g1_ref[...]) * (1.0 + sc) + sh).astype(BF16)
    u = _gelu_tanh(_dot(h, win_ref[:, :d]))
    vv = _gelu_tanh(_dot(h, win_ref[:, d:]))
    mu = jnp.mean(vv, axis=-1, keepdims=True)
    vc = vv - mu
    var = jnp.mean(vc * vc, axis=-1, keepdims=True)
    vn = (vc * lax.rsqrt(var + EPS) * lng_ref[...] + lnb_ref[...]).astype(BF16)
    per_group = []
    for g in range(CM_GROUPS):
        rhs = jnp.concatenate(
            [vn[ch * MIX_CHUNK:(ch + 1) * MIX_CHUNK, g * gw:(g + 1) * gw] for ch in range(n_chunks)], axis=1)
        per_group.append(_dot(ws_ref[g], rhs) + bst_ref[:, g:g + 1])
    s = jnp.concatenate(
        [jnp.concatenate([pg[:, ch * gw:(ch + 1) * gw] for pg in per_group], axis=1) for ch in range(n_chunks)],
        axis=0)
    x1 = x + gt1 * _dot((u * s).astype(BF16), wo_ref[...])
    x2 = _mlp_tail(x1, mod_ref, g2_ref[...], w1_ref, w2_ref)
    y_ref[...] = _rms(x2, fg_ref[...]) if final else x2


def _cm_call(x, mod, mod_row0, mod_stride, g1, g2, fg, w_in, ln_g, ln_b, w_s, b_st, w_out, w1, w2, final, tb):
    bn, l, d = x.shape
    tok = lambda b, j: (b, j, 0)
    consts = [g1, g2, fg, w_in, ln_g, ln_b, w_s, b_st, w_out, w1, w2]
    return pl.pallas_call(
        functools.partial(_cm_kernel, final=final),
        grid=(bn, l // tb),
        in_specs=[
            pl.BlockSpec((None, tb, d), tok),
            pl.BlockSpec((None, N_MOD, d), lambda b, j: (mod_row0 + mod_stride * b, 0, 0)),
        ] + [_const_spec(a.shape) for a in consts],
        out_specs=pl.BlockSpec((None, tb, d), tok),
        out_shape=jax.ShapeDtypeStruct((bn, l, d), F32),
        compiler_params=pltpu.CompilerParams(
            dimension_semantics=("arbitrary", "arbitrary"), vmem_limit_bytes=VMEM_LIMIT),
        name="cm",
    )(x, mod, *consts)


def kernel(x_prompt, x_sample, state_hgrn, c, c_ctx, ada_w, ada_b, norm_mix_g, norm_mlp_g, mlp_w1, mlp_w2,
           hgrn_w_in, hgrn_lb_logits, hgrn_onorm_g, hgrn_w_out, cm_w_in, cm_ln_g, cm_ln_b, cm_w_s, cm_b_s,
           cm_w_out, final_norm_g):
    depth, d, _ = ada_w.shape
    n_lat = c.shape[0]
    assert d % PAIR_W == 0 and d // HEAD_W == HEADS and 1 + n_lat <= COND_ROWS
    assert x_prompt.shape[1] % SCAN_CHUNK == 0 and x_sample.shape[1] % SCAN_CHUNK == 0

    cond = jnp.concatenate([c_ctx[None, :], c, jnp.zeros((COND_ROWS - 1 - n_lat, d), F32)], axis=0)
    mods = _ada_call(cond, ada_w, ada_b).reshape(depth, COND_ROWS, N_MOD, d)

    row = lambda a: a.reshape(1, d)
    lbl = hgrn_lb_logits.reshape(-1, d)
    fg = row(final_norm_g)
    ctx, lat = x_prompt, x_sample
    new_states = []
    for i in range(depth):
        j = i // N_MIXERS
        final = i == depth - 1
        w1 = mlp_w1[i].astype(BF16)
        w2 = mlp_w2[i].astype(BF16)
        g1, g2 = row(norm_mix_g[i]), row(norm_mlp_g[i])
        if i % N_MIXERS == 0:
            w_in = hgrn_w_in[j].astype(BF16)
            w_out = hgrn_w_out[j].astype(BF16)
            onorm = row(hgrn_onorm_g[j])
            *stash, s_f = _hg_fwd_call(ctx, mods[i], 0, 0, g1, w_in, lbl, None, j)
            ctx, s_b = _hg_bwd_call(ctx, mods[i], 0, 0, stash, onorm, g2, fg, w_out, w1, w2, None, j, final)
            new_states.append(jnp.stack([s_f, s_b], axis=1))
            *stash, _ = _hg_fwd_call(lat, mods[i], 1, 1, g1, w_in, lbl, state_hgrn, j)
            lat, _ = _hg_bwd_call(lat, mods[i], 1, 1, stash, onorm, g2, fg, w_out, w1, w2, state_hgrn, j, final)
        else:
            cm_args = (g1, g2, fg, cm_w_in[j].astype(BF16), row(cm_ln_g[j]), row(cm_ln_b[j]),
                       cm_w_s[j].astype(BF16), cm_b_s[j].T, cm_w_out[j].astype(BF16), w1, w2, final)
            ctx = _cm_call(ctx, mods[i], 0, 0, *cm_args, tb=2 * MIX_CHUNK)
            lat = _cm_call(lat, mods[i], 1, 1, *cm_args, tb=2 * MIX_CHUNK)
    new_state = jnp.stack(new_states, axis=1).astype(x_prompt.dtype)
    return ctx, lat, new_state
```

```python
import functools

import jax
import jax.numpy as jnp
from jax import lax
from jax.experimental import pallas as pl
from jax.experimental.pallas import tpu as pltpu

F32 = jnp.float32
BF16 = jnp.bfloat16

EPS = 1e-6
N_MOD = 6
N_MIXERS = 2
HEADS = 8
HEAD_W = 128
PAIR_W = 2 * HEAD_W
MIX_CHUNK = 128
CM_GROUPS = 8
SCAN_BLOCK = 64
SCAN_CHUNK = 2 * SCAN_BLOCK
SUBLANES = 8
COND_ROWS = SUBLANES
SAFE_LOG2_DECAY = -115.0
HG_TOKEN_BLOCK = 256
CM_TOKEN_BLOCK = 512
VMEM_LIMIT = 56 * 1024 * 1024


def _silu(x):
    h = 0.5 * x
    return h * jnp.tanh(h) + h


def _forget_gate(z, lb):
    return 0.5 * (1.0 + lb) + (0.5 * (1.0 - lb)) * jnp.tanh(0.5 * z)


def _gelu_tanh(x):
    return 0.5 * x * (1.0 + jnp.tanh(0.7978845608028654 * (x + 0.044715 * (x * x * x))))


def _rms(x, g):
    return x * lax.rsqrt(jnp.mean(x * x, axis=-1, keepdims=True) + EPS) * g


def _dot(a, b):
    return jnp.dot(a, b, preferred_element_type=F32)


def _dot_nt(a, b):
    return lax.dot_general(a, b, (((1,), (1,)), ((), ())), preferred_element_type=F32)


def _dot_tn(a, b):
    return lax.dot_general(a, b, (((0,), (0,)), ((), ())), preferred_element_type=F32)


def _ada_kernel(cond_ref, w_ref, b_ref, out_ref):
    a = _silu(cond_ref[...]).astype(BF16)
    out_ref[...] = _dot(a, w_ref[...].astype(BF16)) + b_ref[...]


def _ada_call(cond, ada_w, ada_b):
    depth, d, _ = ada_w.shape
    return pl.pallas_call(
        _ada_kernel,
        grid=(depth, N_MOD),
        in_specs=[
            pl.BlockSpec((COND_ROWS, d), lambda i, n: (0, 0)),
            pl.BlockSpec((None, d, d), lambda i, n: (i, 0, n)),
            pl.BlockSpec((None, 1, d), lambda i, n: (i, 0, n)),
        ],
        out_specs=pl.BlockSpec((None, COND_ROWS, d), lambda i, n: (i, 0, n)),
        out_shape=jax.ShapeDtypeStruct((depth, COND_ROWS, N_MOD * d), F32),
        compiler_params=pltpu.CompilerParams(dimension_semantics=("arbitrary", "arbitrary")),
        name="ada",
    )(cond, ada_w, ada_b.reshape(depth, 1, N_MOD * d))


def _scan_masks(rev):
    c = SCAN_CHUNK
    t = lax.broadcasted_iota(jnp.int32, (c, 2 * c), 0)
    s = lax.broadcasted_iota(jnp.int32, (c, 2 * c), 1) & (c - 1)
    t_hi = t >= SCAN_BLOCK
    s_hi = s >= SCAN_BLOCK
    causal = (s >= t) if rev else (s <= t)
    diag = (t_hi == s_hi) & causal
    cross = (jnp.logical_not(t_hi) & s_hi) if rev else (t_hi & jnp.logical_not(s_hi))
    return diag, cross


def _scan_chunk(q, k, g, v, st_ref, attd_ref, od_ref, slow_refs, rev):
    c, d = q.shape
    n_pairs = d // PAIR_W
    row = lax.broadcasted_iota(jnp.int32, (c, 1), 0)
    hi = row >= SCAN_BLOCK
    near = hi if rev else jnp.logical_not(hi)
    far = jnp.logical_not(near)

    t = lax.broadcasted_iota(jnp.int32, (c, c), 0)
    s = lax.broadcasted_iota(jnp.int32, (c, c), 1)
    same = (t >= SCAN_BLOCK) == (s >= SCAN_BLOCK)
    tri = jnp.where(same & ((s >= t) if rev else (s <= t)), 1.0, 0.0).astype(BF16)
    g_hi = g.astype(BF16)
    g_lo = (g - g_hi.astype(F32)).astype(BF16)
    b_in = _dot(tri, g_hi) + _dot(tri, g_lo)

    half = SCAN_BLOCK // 2
    if rev:
        tot_lo, tot_hi = b_in[0:1], b_in[SCAN_BLOCK:SCAN_BLOCK + 1]
        mid_lo, mid_hi = b_in[half:half + 1], b_in[SCAN_BLOCK + half:SCAN_BLOCK + half + 1]
        tot_near, tot_far = tot_hi, tot_lo
    else:
        tot_lo, tot_hi = b_in[SCAN_BLOCK - 1:SCAN_BLOCK], b_in[c - 1:c]
        mid_lo, mid_hi = b_in[half - 1:half], b_in[SCAN_BLOCK + half - 1:SCAN_BLOCK + half]
        tot_near, tot_far = tot_lo, tot_hi
    tot_own = jnp.where(hi, tot_hi, tot_lo)
    mid_own = jnp.where(hi, mid_hi, mid_lo)

    p = jnp.exp2(b_in)
    e = jnp.exp2(tot_own - b_in)
    qh = q * p
    ke = k * e
    qi = (qh * jnp.where(far, jnp.exp2(tot_near), 1.0)).astype(BF16)
    ks = (ke * jnp.where(near, jnp.exp2(tot_far), 1.0)).astype(BF16)
    dec = jnp.exp2(tot_near + tot_far)
    qh16 = qh.astype(BF16)
    ke16 = ke.astype(BF16)
    v16 = v.astype(BF16)

    diag_mask, cross_mask = _scan_masks(rev)
    left = lax.broadcasted_iota(jnp.int32, (1, PAIR_W), 1) < HEAD_W
    bd_mask = (lax.broadcasted_iota(jnp.int32, (PAIR_W, PAIR_W), 0) < HEAD_W) == (
        lax.broadcasted_iota(jnp.int32, (PAIR_W, PAIR_W), 1) < HEAD_W)

    def by_head(a):
        zero = jnp.zeros_like(a)
        return jnp.concatenate([jnp.where(left, a, zero), jnp.where(left, zero, a)], axis=0)

    half_sums = jnp.minimum(jnp.minimum(mid_lo, tot_lo - mid_lo), jnp.minimum(mid_hi, tot_hi - mid_hi))
    fast = jnp.min(half_sums) >= SAFE_LOG2_DECAY

    @pl.when(fast)
    def _():
        qm16 = (q * jnp.exp2(b_in - mid_own)).astype(BF16)
        km16 = (k * jnp.exp2(mid_own - b_in)).astype(BF16)
        for pr in range(n_pairs):
            sl = slice(pr * PAIR_W, (pr + 1) * PAIR_W)
            att = _dot_nt(qm16[:, sl], by_head(km16[:, sl]))
            attd_ref[pr] = jnp.where(diag_mask, att, 0.0)

    @pl.when(jnp.logical_not(fast))
    def _():
        q_s, k_s, v_s, b_s = slow_refs
        attd_ref[...] = jnp.zeros_like(attd_ref)
        q_s[...] = q
        k_s[...] = k
        v_s[...] = v
        b_s[...] = b_in

        def body(grp, carry):
            t0 = pl.multiple_of(grp * SUBLANES, SUBLANES)
            q8 = q_s[pl.ds(t0, SUBLANES), :]
            b8 = b_s[pl.ds(t0, SUBLANES), :]
            out_rows = []
            for r in range(SUBLANES):
                ti = t0 + r
                prod = q8[r:r + 1] * k_s[...] * jnp.exp2(jnp.minimum(b8[r:r + 1] - b_s[...], 0.0))
                valid = ((row >= SCAN_BLOCK) == (ti >= SCAN_BLOCK)) & ((row >= ti) if rev else (row <= ti))
                heads = []
                for h in range(d // HEAD_W):
                    hs = slice(h * HEAD_W, (h + 1) * HEAD_W)
                    w = jnp.where(valid, jnp.sum(prod[:, hs], axis=-1, keepdims=True), 0.0)
                    heads.append(jnp.sum(w * v_s[:, hs], axis=0, keepdims=True))
                out_rows.append(jnp.concatenate(heads, axis=1))
            od_ref[pl.ds(t0, SUBLANES), :] = jnp.concatenate(out_rows, axis=0)
            return carry

        lax.fori_loop(0, c // SUBLANES, body, 0)

    outs = []
    for pr in range(n_pairs):
        sl = slice(pr * PAIR_W, (pr + 1) * PAIR_W)
        v2 = by_head(v16[:, sl])
        attx = _dot_nt(qh16[:, sl], by_head(ke16[:, sl]))
        att = (jnp.where(cross_mask, attx, 0.0) + attd_ref[pr]).astype(BF16)
        st = st_ref[pr]
        outs.append(_dot(att, v2) + _dot_nt(qi[:, sl], st.astype(BF16)))
        upd = _dot_tn(v16[:, sl], ks[:, sl])
        st_ref[pr] = st * dec[:, sl] + jnp.where(bd_mask, upd, 0.0)
    return jnp.concatenate(outs, axis=1), fast


def _state_init(st_ref, s0_ref):
    st_ref[...] = jnp.zeros_like(st_ref)
    if s0_ref is not None:
        for h in range(HEADS):
            o = (h % 2) * HEAD_W
            st_ref[h // 2, o:o + HEAD_W, o:o + HEAD_W] = s0_ref[h].T


def _state_write(st_ref, out_ref):
    for h in range(HEADS):
        o = (h % 2) * HEAD_W
        out_ref[h] = st_ref[h // 2, o:o + HEAD_W, o:o + HEAD_W].T


def _lower_bound(lbl_ref, layer, direction):
    n_layers = lbl_ref.shape[0] // 2
    rows = [lbl_ref[2 * i + direction:2 * i + direction + 1, :] for i in range(n_layers)]
    m = functools.reduce(jnp.maximum, rows)
    ex = [jnp.exp(r - m) for r in rows]
    return sum(ex[:layer + 1]) / sum(ex)


def _hg_fwd_kernel(*refs, has_s0, layer):
    if has_s0:
        x_ref, mod_ref, g_ref, w_ref, lbl_ref, s0_ref = refs[:6]
        rest = refs[6:]
    else:
        x_ref, mod_ref, g_ref, w_ref, lbl_ref = refs[:5]
        s0_ref = None
        rest = refs[5:]
    of_ref, q_ref, v_ref, fb_ref, gate_ref, sf_ref = rest[:6]
    st_ref, attd_ref, od_ref = rest[6:9]
    slow_refs = rest[9:13]
    j = pl.program_id(1)
    d = x_ref.shape[-1]

    @pl.when(j == 0)
    def _():
        _state_init(st_ref, s0_ref)

    sh, sc = mod_ref[0:1, :], mod_ref[1:2, :]
    h = (_rms(x_ref[...], g_ref[...]) * (1.0 + sc) + sh).astype(BF16)
    zq = _dot(h, w_ref[:, 0 * d:1 * d])
    zff = _dot(h, w_ref[:, 1 * d:2 * d])
    zfb = _dot(h, w_ref[:, 2 * d:3 * d])
    zi = _dot(h, w_ref[:, 3 * d:4 * d])
    zg = _dot(h, w_ref[:, 4 * d:5 * d])
    lb_f = _lower_bound(lbl_ref, layer, 0)
    lb_b = _lower_bound(lbl_ref, layer, 1)
    q = _silu(zq)
    f_f = _forget_gate(zff, lb_f)
    fb_ref[...] = _forget_gate(zfb, lb_b)
    q_ref[...] = q.astype(BF16)
    v_ref[...] = zi.astype(BF16)
    gate_ref[...] = _silu(zg).astype(BF16)
    k_f = 1.0 - f_f
    g_f = jnp.log2(f_f)

    for ci in range(x_ref.shape[0] // SCAN_CHUNK):
        rows = slice(ci * SCAN_CHUNK, (ci + 1) * SCAN_CHUNK)
        o, fast = _scan_chunk(q[rows], k_f[rows], g_f[rows], zi[rows], st_ref, attd_ref, od_ref, slow_refs,
                              rev=False)
        of_ref[rows, :] = o

        @pl.when(jnp.logical_not(fast))
        def _():
            of_ref[rows, :] += od_ref[...]

    @pl.when(j == pl.num_programs(1) - 1)
    def _():
        _state_write(st_ref, sf_ref)


def _const_spec(shape):
    nd = len(shape)
    return pl.BlockSpec(shape, lambda *_: (0,) * nd, pipeline_mode=pl.Buffered(1))


def _scan_scratch(d):
    c = SCAN_CHUNK
    return [
        pltpu.VMEM((d // PAIR_W, PAIR_W, PAIR_W), F32),
        pltpu.VMEM((d // PAIR_W, c, 2 * c), F32),
        pltpu.VMEM((c, d), F32),
        pltpu.VMEM((c, d), F32), pltpu.VMEM((c, d), F32),
        pltpu.VMEM((c, d), F32), pltpu.VMEM((c, d), F32),
    ]


def _hg_fwd_call(x, mod, mod_row0, mod_stride, g, w_in, lbl, state, layer_a):
    bn, l, d = x.shape
    c = min(HG_TOKEN_BLOCK, l)
    assert l % c == 0 and c % SCAN_CHUNK == 0
    n = l // c
    tok = lambda b, j: (b, j, 0)
    in_specs = [
        pl.BlockSpec((None, c, d), tok),
        pl.BlockSpec((None, N_MOD, d), lambda b, j: (mod_row0 + mod_stride * b, 0, 0)),
        _const_spec((1, d)),
        _const_spec(w_in.shape),
        _const_spec(lbl.shape),
    ]
    args = [x, mod, g, w_in, lbl]
    if state is not None:
        in_specs.append(pl.BlockSpec((None, None, None, HEADS, HEAD_W, HEAD_W),
                                     lambda b, j: (b, layer_a, 0, 0, 0, 0)))
        args.append(state)
    st_spec = pl.BlockSpec((None, HEADS, HEAD_W, HEAD_W), lambda b, j: (b, 0, 0, 0))
    return pl.pallas_call(
        functools.partial(_hg_fwd_kernel, has_s0=state is not None, layer=layer_a),
        grid=(bn, n),
        in_specs=in_specs,
        out_specs=[pl.BlockSpec((None, c, d), tok)] * 5 + [st_spec],
        out_shape=[
            jax.ShapeDtypeStruct((bn, l, d), F32),
            jax.ShapeDtypeStruct((bn, l, d), BF16),
            jax.ShapeDtypeStruct((bn, l, d), BF16),
            jax.ShapeDtypeStruct((bn, l, d), F32),
            jax.ShapeDtypeStruct((bn, l, d), BF16),
            jax.ShapeDtypeStruct((bn, HEADS, HEAD_W, HEAD_W), F32),
        ],
        scratch_shapes=_scan_scratch(d),
        compiler_params=pltpu.CompilerParams(
            dimension_semantics=("arbitrary", "arbitrary"), vmem_limit_bytes=VMEM_LIMIT),
        name="hg_fwd",
    )(*args)


def _mlp_tail(x1, mod_ref, g2, w1_ref, w2_ref):
    sh2, sc2, gt2 = mod_ref[3:4, :], mod_ref[4:5, :], mod_ref[5:6, :]
    h2 = (_rms(x1, g2) * (1.0 + sc2) + sh2).astype(BF16)
    a = jnp.maximum(_dot(h2, w1_ref[...]), 0.0)
    return x1 + gt2 * _dot((a * a).astype(BF16), w2_ref[...])


def _hg_bwd_kernel(*refs, has_s0, final):
    x_ref, mod_ref, of_ref, q_ref, v_ref, fb_ref, gate_ref, on_ref, g2_ref, fg_ref = refs[:10]
    wo_ref, w1_ref, w2_ref = refs[10:13]
    if has_s0:
        s0_ref = refs[13]
        rest = refs[14:]
    else:
        s0_ref = None
        rest = refs[13:]
    y_ref, sb_ref = rest[:2]
    st_ref, attd_ref, od_ref = rest[2:5]
    slow_refs = rest[5:9]
    j = pl.program_id(1)
    d = x_ref.shape[-1]

    @pl.when(j == 0)
    def _():
        _state_init(st_ref, s0_ref)

    for ci in reversed(range(x_ref.shape[0] // SCAN_CHUNK)):
        rows = slice(ci * SCAN_CHUNK, (ci + 1) * SCAN_CHUNK)
        f_b = fb_ref[rows, :]
        o_b, fast = _scan_chunk(q_ref[rows, :].astype(F32), 1.0 - f_b, jnp.log2(f_b),
                                v_ref[rows, :].astype(F32), st_ref, attd_ref, od_ref, slow_refs, rev=True)
        y_ref[rows, :] = of_ref[rows, :] + o_b

        @pl.when(jnp.logical_not(fast))
        def _():
            y_ref[rows, :] += od_ref[...]

    o = y_ref[...]
    parts = []
    for h in range(d // HEAD_W):
        oh = o[:, h * HEAD_W:(h + 1) * HEAD_W]
        parts.append(oh * lax.rsqrt(jnp.mean(oh * oh, axis=-1, keepdims=True) + EPS))
    o = jnp.concatenate(parts, axis=1) * on_ref[...] * gate_ref[...].astype(F32)
    gt1 = mod_ref[2:3, :]
    x1 = x_ref[...] + gt1 * _dot(o.astype(BF16), wo_ref[...])
    x2 = _mlp_tail(x1, mod_ref, g2_ref[...], w1_ref, w2_ref)
    y_ref[...] = _rms(x2, fg_ref[...]) if final else x2

    @pl.when(j == pl.num_programs(1) - 1)
    def _():
        _state_write(st_ref, sb_ref)


def _hg_bwd_call(x, mod, mod_row0, mod_stride, stash, onorm, g2, fg, w_out, w1, w2, state, layer_a, final):
    bn, l, d = x.shape
    c = min(HG_TOKEN_BLOCK, l)
    assert l % c == 0 and c % SCAN_CHUNK == 0
    n = l // c
    tok = lambda b, j: (b, n - 1 - j, 0)
    o_f, q, v, f_b, gate = stash
    in_specs = [
        pl.BlockSpec((None, c, d), tok),
        pl.BlockSpec((None, N_MOD, d), lambda b, j: (mod_row0 + mod_stride * b, 0, 0)),
    ] + [pl.BlockSpec((None, c, d), tok)] * 5 + [
        _const_spec((1, d)), _const_spec((1, d)), _const_spec((1, d)),
        _const_spec(w_out.shape), _const_spec(w1.shape), _const_spec(w2.shape),
    ]
    args = [x, mod, o_f, q, v, f_b, gate, onorm, g2, fg, w_out, w1, w2]
    if state is not None:
        in_specs.append(pl.BlockSpec((None, None, None, HEADS, HEAD_W, HEAD_W),
                                     lambda b, j: (b, layer_a, 1, 0, 0, 0)))
        args.append(state)
    st_spec = pl.BlockSpec((None, HEADS, HEAD_W, HEAD_W), lambda b, j: (b, 0, 0, 0))
    return pl.pallas_call(
        functools.partial(_hg_bwd_kernel, has_s0=state is not None, final=final),
        grid=(bn, n),
        in_specs=in_specs,
        out_specs=[pl.BlockSpec((None, c, d), tok), st_spec],
        out_shape=[
            jax.ShapeDtypeStruct((bn, l, d), F32),
            jax.ShapeDtypeStruct((bn, HEADS, HEAD_W, HEAD_W), F32),
        ],
        scratch_shapes=_scan_scratch(d),
        compiler_params=pltpu.CompilerParams(
            dimension_semantics=("arbitrary", "arbitrary"), vmem_limit_bytes=VMEM_LIMIT),
        name="hg_bwd",
    )(*args)


def _cm_kernel(x_ref, mod_ref, g1_ref, g2_ref, fg_ref, win_ref, lng_ref, lnb_ref, ws_ref, bst_ref,
               wo_ref, w1_ref, w2_ref, y_ref, *, final):
    tb, d = x_ref.shape
    n_chunks = tb // MIX_CHUNK
    gw = d // CM_GROUPS
    x = x_ref[...]
    sh, sc, gt1 = mod_ref[0:1, :], mod_ref[1:2, :], mod_ref[2:3, :]
    h = (_rms(x, g1_ref[...]) * (1.0 + sc) + sh).astype(BF16)
    u = _gelu_tanh(_dot(h, win_ref[:, :d]))
    vv = _gelu_tanh(_dot(h, win_ref[:, d:]))
    mu = jnp.mean(vv, axis=-1, keepdims=True)
    vc = vv - mu
    var = jnp.mean(vc * vc, axis=-1, keepdims=True)
    vn = (vc * lax.rsqrt(var + EPS) * lng_ref[...] + lnb_ref[...]).astype(BF16)
    per_group = []
    for g in range(CM_GROUPS):
        rhs = jnp.concatenate(
            [vn[ch * MIX_CHUNK:(ch + 1) * MIX_CHUNK, g * gw:(g + 1) * gw] for ch in range(n_chunks)], axis=1)
        per_group.append(_dot(ws_ref[g], rhs) + bst_ref[:, g:g + 1])
    s = jnp.concatenate(
        [jnp.concatenate([pg[:, ch * gw:(ch + 1) * gw] for pg in per_group], axis=1) for ch in range(n_chunks)],
        axis=0)
    x1 = x + gt1 * _dot((u * s).astype(BF16), wo_ref[...])
    x2 = _mlp_tail(x1, mod_ref, g2_ref[...], w1_ref, w2_ref)
    y_ref[...] = _rms(x2, fg_ref[...]) if final else x2


def _cm_call(x, mod, mod_row0, mod_stride, g1, g2, fg, w_in, ln_g, ln_b, w_s, b_st, w_out, w1, w2, final):
    bn, l, d = x.shape
    tb = min(CM_TOKEN_BLOCK, l)
    assert l % tb == 0 and tb % MIX_CHUNK == 0
    tok = lambda b, j: (b, j, 0)
    consts = [g1, g2, fg, w_in, ln_g, ln_b, w_s, b_st, w_out, w1, w2]
    return pl.pallas_call(
        functools.partial(_cm_kernel, final=final),
        grid=(bn, l // tb),
        in_specs=[
            pl.BlockSpec((None, tb, d), tok),
            pl.BlockSpec((None, N_MOD, d), lambda b, j: (mod_row0 + mod_stride * b, 0, 0)),
        ] + [_const_spec(a.shape) for a in consts],
        out_specs=pl.BlockSpec((None, tb, d), tok),
        out_shape=jax.ShapeDtypeStruct((bn, l, d), F32),
        compiler_params=pltpu.CompilerParams(
            dimension_semantics=("arbitrary", "arbitrary"), vmem_limit_bytes=VMEM_LIMIT),
        name="cm",
    )(x, mod, *consts)


def kernel(x_prompt, x_sample, state_hgrn, c, c_ctx, ada_w, ada_b, norm_mix_g, norm_mlp_g, mlp_w1, mlp_w2,
           hgrn_w_in, hgrn_lb_logits, hgrn_onorm_g, hgrn_w_out, cm_w_in, cm_ln_g, cm_ln_b, cm_w_s, cm_b_s,
           cm_w_out, final_norm_g):
    depth, d, _ = ada_w.shape
    n_lat = c.shape[0]
    assert d % PAIR_W == 0 and d // HEAD_W == HEADS and 1 + n_lat <= COND_ROWS
    assert x_prompt.shape[1] % SCAN_CHUNK == 0 and x_sample.shape[1] % SCAN_CHUNK == 0

    cond = jnp.concatenate([c_ctx[None, :], c, jnp.zeros((COND_ROWS - 1 - n_lat, d), F32)], axis=0)
    mods = _ada_call(cond, ada_w, ada_b).reshape(depth, COND_ROWS, N_MOD, d)

    row = lambda a: a.reshape(1, d)
    lbl = hgrn_lb_logits.reshape(-1, d)
    fg = row(final_norm_g)
    ctx, lat = x_prompt, x_sample
    new_states = []
    for i in range(depth):
        j = i // N_MIXERS
        final = i == depth - 1
        w1 = mlp_w1[i].astype(BF16)
        w2 = mlp_w2[i].astype(BF16)
        g1, g2 = row(norm_mix_g[i]), row(norm_mlp_g[i])
        if i % N_MIXERS == 0:
            w_in = hgrn_w_in[j].astype(BF16)
            w_out = hgrn_w_out[j].astype(BF16)
            onorm = row(hgrn_onorm_g[j])
            *stash, s_f = _hg_fwd_call(ctx, mods[i], 0, 0, g1, w_in, lbl, None, j)
            ctx, s_b = _hg_bwd_call(ctx, mods[i], 0, 0, stash, onorm, g2, fg, w_out, w1, w2, None, j, final)
            new_states.append(jnp.stack([s_f, s_b], axis=1))
            *stash, _ = _hg_fwd_call(lat, mods[i], 1, 1, g1, w_in, lbl, state_hgrn, j)
            lat, _ = _hg_bwd_call(lat, mods[i], 1, 1, stash, onorm, g2, fg, w_out, w1, w2, state_hgrn, j, final)
        else:
            cm_args = (g1, g2, fg, cm_w_in[j].astype(BF16), row(cm_ln_g[j]), row(cm_ln_b[j]),
                       cm_w_s[j].astype(BF16), cm_b_s[j].T, cm_w_out[j].astype(BF16), w1, w2, final)
            ctx = _cm_call(ctx, mods[i], 0, 0, *cm_args)
            lat = _cm_call(lat, mods[i], 1, 1, *cm_args)
    new_state = jnp.stack(new_states, axis=1).astype(x_prompt.dtype)
    return ctx, lat, new_state
```

```python
import functools

import jax
import jax.numpy as jnp
from jax import lax
from jax.experimental import pallas as pl
from jax.experimental.pallas import tpu as pltpu

F32 = jnp.float32
BF16 = jnp.bfloat16

EPS = 1e-6
N_MOD = 6
N_MIXERS = 2
HEADS = 8
HEAD_W = 128
PAIR_W = 2 * HEAD_W
MIX_CHUNK = 128
CM_GROUPS = 8
SCAN_BLOCK = 64
SCAN_CHUNK = 2 * SCAN_BLOCK
SUBLANES = 8
COND_ROWS = SUBLANES
SAFE_LOG2_DECAY = -115.0
HG_TOKEN_BLOCK = 256
CM_TOKEN_BLOCK = 512
VMEM_LIMIT = 56 * 1024 * 1024


def _silu(x):
    h = 0.5 * x
    return h * jnp.tanh(h) + h


def _forget_gate(z, lb):
    return 0.5 * (1.0 + lb) + (0.5 * (1.0 - lb)) * jnp.tanh(0.5 * z)


def _gelu_tanh(x):
    return 0.5 * x * (1.0 + jnp.tanh(0.7978845608028654 * (x + 0.044715 * (x * x * x))))


def _rms(x, g):
    return x * lax.rsqrt(jnp.mean(x * x, axis=-1, keepdims=True) + EPS) * g


def _dot(a, b):
    return jnp.dot(a, b, preferred_element_type=F32)


def _dot_nt(a, b):
    return lax.dot_general(a, b, (((1,), (1,)), ((), ())), preferred_element_type=F32)


def _dot_tn(a, b):
    return lax.dot_general(a, b, (((0,), (0,)), ((), ())), preferred_element_type=F32)


def _ada_kernel(cond_ref, w_ref, b_ref, out_ref):
    a = _silu(cond_ref[...]).astype(BF16)
    out_ref[...] = _dot(a, w_ref[...].astype(BF16)) + b_ref[...]


def _ada_call(cond, ada_w, ada_b):
    depth, d, _ = ada_w.shape
    return pl.pallas_call(
        _ada_kernel,
        grid=(depth, N_MOD),
        in_specs=[
            pl.BlockSpec((COND_ROWS, d), lambda i, n: (0, 0)),
            pl.BlockSpec((None, d, d), lambda i, n: (i, 0, n)),
            pl.BlockSpec((None, 1, d), lambda i, n: (i, 0, n)),
        ],
        out_specs=pl.BlockSpec((None, COND_ROWS, d), lambda i, n: (i, 0, n)),
        out_shape=jax.ShapeDtypeStruct((depth, COND_ROWS, N_MOD * d), F32),
        compiler_params=pltpu.CompilerParams(dimension_semantics=("arbitrary", "arbitrary")),
        name="ada",
    )(cond, ada_w, ada_b.reshape(depth, 1, N_MOD * d))


def _scan_masks(rev):
    c = SCAN_CHUNK
    t = lax.broadcasted_iota(jnp.int32, (c, 2 * c), 0)
    s = lax.broadcasted_iota(jnp.int32, (c, 2 * c), 1) & (c - 1)
    t_hi = t >= SCAN_BLOCK
    s_hi = s >= SCAN_BLOCK
    causal = (s >= t) if rev else (s <= t)
    diag = (t_hi == s_hi) & causal
    cross = (jnp.logical_not(t_hi) & s_hi) if rev else (t_hi & jnp.logical_not(s_hi))
    return diag, cross


def _block_cumsum(g, rev):
    c = g.shape[0]
    t = lax.broadcasted_iota(jnp.int32, (c, c), 0)
    s = lax.broadcasted_iota(jnp.int32, (c, c), 1)
    same = (t >= SCAN_BLOCK) == (s >= SCAN_BLOCK)
    tri = jnp.where(same & ((s >= t) if rev else (s <= t)), 1.0, 0.0).astype(BF16)
    g_hi = g.astype(BF16)
    g_lo = (g - g_hi.astype(F32)).astype(BF16)
    return _dot(tri, g_hi) + _dot(tri, g_lo)


def _scan_chunk(q, k, g, v, st_ref, rev):
    c, d = q.shape
    n_pairs = d // PAIR_W
    row = lax.broadcasted_iota(jnp.int32, (c, 1), 0)
    hi = row >= SCAN_BLOCK
    near = hi if rev else jnp.logical_not(hi)
    far = jnp.logical_not(near)
    b_in = _block_cumsum(g, rev)

    half = SCAN_BLOCK // 2
    if rev:
        tot_lo, tot_hi = b_in[0:1], b_in[SCAN_BLOCK:SCAN_BLOCK + 1]
        mid_lo, mid_hi = b_in[half:half + 1], b_in[SCAN_BLOCK + half:SCAN_BLOCK + half + 1]
        tot_near, tot_far = tot_hi, tot_lo
    else:
        tot_lo, tot_hi = b_in[SCAN_BLOCK - 1:SCAN_BLOCK], b_in[c - 1:c]
        mid_lo, mid_hi = b_in[half - 1:half], b_in[SCAN_BLOCK + half - 1:SCAN_BLOCK + half]
        tot_near, tot_far = tot_lo, tot_hi
    tot_own = jnp.where(hi, tot_hi, tot_lo)
    mid_own = jnp.where(hi, mid_hi, mid_lo)

    p = jnp.exp2(b_in)
    e = jnp.exp2(tot_own - b_in)
    qh = q * p
    ke = k * e
    qi = (qh * jnp.where(far, jnp.exp2(tot_near), 1.0)).astype(BF16)
    ks = (ke * jnp.where(near, jnp.exp2(tot_far), 1.0)).astype(BF16)
    dec = jnp.exp2(tot_near + tot_far)
    qh16 = qh.astype(BF16)
    ke16 = ke.astype(BF16)
    v16 = v.astype(BF16)
    qm16 = (q * jnp.exp2(b_in - mid_own)).astype(BF16)
    km16 = (k * jnp.exp2(mid_own - b_in)).astype(BF16)

    half_sums = jnp.minimum(jnp.minimum(mid_lo, tot_lo - mid_lo), jnp.minimum(mid_hi, tot_hi - mid_hi))
    fast_vec = jnp.min(half_sums, axis=-1, keepdims=True) >= SAFE_LOG2_DECAY
    fast = jnp.min(half_sums) >= SAFE_LOG2_DECAY

    diag_mask, cross_mask = _scan_masks(rev)
    diag_mask = diag_mask & fast_vec
    left = lax.broadcasted_iota(jnp.int32, (1, PAIR_W), 1) < HEAD_W
    bd_mask = (lax.broadcasted_iota(jnp.int32, (PAIR_W, PAIR_W), 0) < HEAD_W) == (
        lax.broadcasted_iota(jnp.int32, (PAIR_W, PAIR_W), 1) < HEAD_W)

    def by_head(a):
        zero = jnp.zeros_like(a)
        return jnp.concatenate([jnp.where(left, a, zero), jnp.where(left, zero, a)], axis=0)

    outs = []
    for pr in range(n_pairs):
        sl = slice(pr * PAIR_W, (pr + 1) * PAIR_W)
        attd = _dot_nt(qm16[:, sl], by_head(km16[:, sl]))
        attx = _dot_nt(qh16[:, sl], by_head(ke16[:, sl]))
        att = jnp.where(diag_mask, attd, jnp.where(cross_mask, attx, 0.0)).astype(BF16)
        st = st_ref[pr]
        outs.append(_dot(att, by_head(v16[:, sl])) + _dot_nt(qi[:, sl], st.astype(BF16)))
        upd = _dot_tn(v16[:, sl], ks[:, sl])
        st_ref[pr] = st * dec[:, sl] + jnp.where(bd_mask, upd, 0.0)
    return jnp.concatenate(outs, axis=1), fast


def _scan_slow_fix(q_ref, v_ref, f_ref, rows, o_ref, od_ref, slow_refs, rev):
    q_s, k_s, v_s, b_s = slow_refs
    c, d = od_ref.shape
    f = f_ref[rows, :]
    q_s[...] = q_ref[rows, :].astype(F32)
    k_s[...] = 1.0 - f
    v_s[...] = v_ref[rows, :].astype(F32)
    b_s[...] = _block_cumsum(jnp.log2(f), rev)
    row = lax.broadcasted_iota(jnp.int32, (c, 1), 0)

    def body(grp, carry):
        t0 = pl.multiple_of(grp * SUBLANES, SUBLANES)
        q8 = q_s[pl.ds(t0, SUBLANES), :]
        b8 = b_s[pl.ds(t0, SUBLANES), :]
        out_rows = []
        for r in range(SUBLANES):
            ti = t0 + r
            prod = q8[r:r + 1] * k_s[...] * jnp.exp2(jnp.minimum(b8[r:r + 1] - b_s[...], 0.0))
            valid = ((row >= SCAN_BLOCK) == (ti >= SCAN_BLOCK)) & ((row >= ti) if rev else (row <= ti))
            heads = []
            for h in range(d // HEAD_W):
                hs = slice(h * HEAD_W, (h + 1) * HEAD_W)
                w = jnp.where(valid, jnp.sum(prod[:, hs], axis=-1, keepdims=True), 0.0)
                heads.append(jnp.sum(w * v_s[:, hs], axis=0, keepdims=True))
            out_rows.append(jnp.concatenate(heads, axis=1))
        od_ref[pl.ds(t0, SUBLANES), :] = jnp.concatenate(out_rows, axis=0)
        return carry

    lax.fori_loop(0, c // SUBLANES, body, 0)
    o_ref[rows, :] += od_ref[...]


def _state_init(st_ref, s0_ref):
    st_ref[...] = jnp.zeros_like(st_ref)
    if s0_ref is not None:
        for h in range(HEADS):
            o = (h % 2) * HEAD_W
            st_ref[h // 2, o:o + HEAD_W, o:o + HEAD_W] = s0_ref[h].T


def _state_write(st_ref, out_ref):
    for h in range(HEADS):
        o = (h % 2) * HEAD_W
        out_ref[h] = st_ref[h // 2, o:o + HEAD_W, o:o + HEAD_W].T


def _lower_bound(lbl_ref, layer, direction):
    n_layers = lbl_ref.shape[0] // 2
    rows = [lbl_ref[2 * i + direction:2 * i + direction + 1, :] for i in range(n_layers)]
    m = functools.reduce(jnp.maximum, rows)
    ex = [jnp.exp(r - m) for r in rows]
    return sum(ex[:layer + 1]) / sum(ex)


def _chunk_rows(n_rows, rev):
    order = range(n_rows // SCAN_CHUNK)
    return [slice(ci * SCAN_CHUNK, (ci + 1) * SCAN_CHUNK) for ci in (reversed(order) if rev else order)]


def _scan_scratch(d):
    c = SCAN_CHUNK
    return [
        pltpu.VMEM((d // PAIR_W, PAIR_W, PAIR_W), F32),
        pltpu.VMEM((c, d), F32),
        pltpu.VMEM((c, d), F32), pltpu.VMEM((c, d), F32),
        pltpu.VMEM((c, d), F32), pltpu.VMEM((c, d), F32),
    ]


def _const_spec(shape):
    nd = len(shape)
    return pl.BlockSpec(shape, lambda *_: (0,) * nd, pipeline_mode=pl.Buffered(1))


def _hg_fwd_kernel(*refs, has_s0, layer):
    if has_s0:
        x_ref, mod_ref, g_ref, w_ref, lbl_ref, s0_ref = refs[:6]
        rest = refs[6:]
    else:
        x_ref, mod_ref, g_ref, w_ref, lbl_ref = refs[:5]
        s0_ref = None
        rest = refs[5:]
    of_ref, q_ref, v_ref, fb_ref, gate_ref, sf_ref = rest[:6]
    st_ref, od_ref = rest[6:8]
    slow_refs = rest[8:12]
    ff_ref = rest[12]
    j = pl.program_id(1)
    d = x_ref.shape[-1]

    @pl.when(j == 0)
    def _():
        _state_init(st_ref, s0_ref)

    sh, sc = mod_ref[0:1, :], mod_ref[1:2, :]
    h = (_rms(x_ref[...], g_ref[...]) * (1.0 + sc) + sh).astype(BF16)
    zq = _dot(h, w_ref[:, 0 * d:1 * d])
    zff = _dot(h, w_ref[:, 1 * d:2 * d])
    zfb = _dot(h, w_ref[:, 2 * d:3 * d])
    zi = _dot(h, w_ref[:, 3 * d:4 * d])
    zg = _dot(h, w_ref[:, 4 * d:5 * d])
    lb_f = _lower_bound(lbl_ref, layer, 0)
    lb_b = _lower_bound(lbl_ref, layer, 1)
    q = _silu(zq)
    f_f = _forget_gate(zff, lb_f)
    fb_ref[...] = _forget_gate(zfb, lb_b)
    q_ref[...] = q.astype(BF16)
    v_ref[...] = zi.astype(BF16)
    gate_ref[...] = _silu(zg).astype(BF16)
    ff_ref[...] = f_f
    k_f = 1.0 - f_f
    g_f = jnp.log2(f_f)

    flags = []
    for rows in _chunk_rows(x_ref.shape[0], rev=False):
        o, fast = _scan_chunk(q[rows], k_f[rows], g_f[rows], zi[rows], st_ref, rev=False)
        of_ref[rows, :] = o
        flags.append((rows, fast))
    for rows, fast in flags:
        @pl.when(jnp.logical_not(fast))
        def _():
            _scan_slow_fix(q_ref, v_ref, ff_ref, rows, of_ref, od_ref, slow_refs, rev=False)

    @pl.when(j == pl.num_programs(1) - 1)
    def _():
        _state_write(st_ref, sf_ref)


def _hg_fwd_call(x, mod, mod_row0, mod_stride, g, w_in, lbl, state, layer_a):
    bn, l, d = x.shape
    c = min(HG_TOKEN_BLOCK, l)
    assert l % c == 0 and c % SCAN_CHUNK == 0
    n = l // c
    tok = lambda b, j: (b, j, 0)
    in_specs = [
        pl.BlockSpec((None, c, d), tok),
        pl.BlockSpec((None, N_MOD, d), lambda b, j: (mod_row0 + mod_stride * b, 0, 0)),
        _const_spec((1, d)),
        _const_spec(w_in.shape),
        _const_spec(lbl.shape),
    ]
    args = [x, mod, g, w_in, lbl]
    if state is not None:
        in_specs.append(pl.BlockSpec((None, None, None, HEADS, HEAD_W, HEAD_W),
                                     lambda b, j: (b, layer_a, 0, 0, 0, 0)))
        args.append(state)
    st_spec = pl.BlockSpec((None, HEADS, HEAD_W, HEAD_W), lambda b, j: (b, 0, 0, 0))
    return pl.pallas_call(
        functools.partial(_hg_fwd_kernel, has_s0=state is not None, layer=layer_a),
        grid=(bn, n),
        in_specs=in_specs,
        out_specs=[pl.BlockSpec((None, c, d), tok)] * 5 + [st_spec],
        out_shape=[
            jax.ShapeDtypeStruct((bn, l, d), F32),
            jax.ShapeDtypeStruct((bn, l, d), BF16),
            jax.ShapeDtypeStruct((bn, l, d), BF16),
            jax.ShapeDtypeStruct((bn, l, d), F32),
            jax.ShapeDtypeStruct((bn, l, d), BF16),
            jax.ShapeDtypeStruct((bn, HEADS, HEAD_W, HEAD_W), F32),
        ],
        scratch_shapes=_scan_scratch(d) + [pltpu.VMEM((c, d), F32)],
        compiler_params=pltpu.CompilerParams(
            dimension_semantics=("arbitrary", "arbitrary"), vmem_limit_bytes=VMEM_LIMIT),
        name="hg_fwd",
    )(*args)


def _mlp_tail(x1, mod_ref, g2, w1_ref, w2_ref):
    sh2, sc2, gt2 = mod_ref[3:4, :], mod_ref[4:5, :], mod_ref[5:6, :]
    h2 = (_rms(x1, g2) * (1.0 + sc2) + sh2).astype(BF16)
    a = jnp.maximum(_dot(h2, w1_ref[...]), 0.0)
    return x1 + gt2 * _dot((a * a).astype(BF16), w2_ref[...])


def _hg_bwd_kernel(*refs, has_s0, final, n_blocks):
    x_ref, mod_ref, gate_ref, of_ref, q_ref, v_ref, fb_ref, on_ref, g2_ref, fg_ref = refs[:10]
    wo_ref, w1_ref, w2_ref = refs[10:13]
    if has_s0:
        s0_ref = refs[13]
        rest = refs[14:]
    else:
        s0_ref = None
        rest = refs[13:]
    y_ref, sb_ref = rest[:2]
    st_ref, od_ref = rest[2:4]
    slow_refs = rest[4:8]
    o_ref = rest[8]
    s = pl.program_id(0)
    total = pl.num_programs(0) - 1
    live = s < total
    j = jnp.minimum(s, total - 1) % n_blocks
    d = x_ref.shape[-1]

    @pl.when(s == 0)
    def _():
        o_ref[...] = jnp.zeros_like(o_ref)

    @pl.when(live & (j == 0))
    def _():
        _state_init(st_ref, s0_ref)

    o = o_ref[...]
    parts = []
    for h in range(d // HEAD_W):
        oh = o[:, h * HEAD_W:(h + 1) * HEAD_W]
        parts.append(oh * lax.rsqrt(jnp.mean(oh * oh, axis=-1, keepdims=True) + EPS))
    o = jnp.concatenate(parts, axis=1) * on_ref[...] * gate_ref[...].astype(F32)
    gt1 = mod_ref[2:3, :]
    x1 = x_ref[...] + gt1 * _dot(o.astype(BF16), wo_ref[...])
    x2 = _mlp_tail(x1, mod_ref, g2_ref[...], w1_ref, w2_ref)
    y_ref[...] = _rms(x2, fg_ref[...]) if final else x2

    flags = []
    for rows in _chunk_rows(q_ref.shape[0], rev=True):
        f_b = fb_ref[rows, :]
        o_b, fast = _scan_chunk(q_ref[rows, :].astype(F32), 1.0 - f_b, jnp.log2(f_b),
                                v_ref[rows, :].astype(F32), st_ref, rev=True)
        o_ref[rows, :] = of_ref[rows, :] + o_b
        flags.append((rows, fast))
    for rows, fast in flags:
        @pl.when(jnp.logical_not(fast))
        def _():
            _scan_slow_fix(q_ref, v_ref, fb_ref, rows, o_ref, od_ref, slow_refs, rev=True)

    @pl.when(live & (j == n_blocks - 1))
    def _():
        _state_write(st_ref, sb_ref)


def _hg_bwd_call(x, mod, mod_row0, mod_stride, stash, onorm, g2, fg, w_out, w1, w2, state, layer_a, final):
    bn, l, d = x.shape
    c = min(HG_TOKEN_BLOCK, l)
    assert l % c == 0 and c % SCAN_CHUNK == 0
    n = l // c
    total = bn * n

    def cur(s):
        t = jnp.minimum(s, total - 1)
        return t // n, n - 1 - t % n

    def prev(s):
        t = jnp.maximum(s - 1, 0)
        return t // n, n - 1 - t % n

    tok_cur = lambda s: (*cur(s), 0)
    tok_prev = lambda s: (*prev(s), 0)
    o_f, q, v, f_b, gate = stash
    in_specs = [
        pl.BlockSpec((None, c, d), tok_prev),
        pl.BlockSpec((None, N_MOD, d), lambda s: (mod_row0 + mod_stride * prev(s)[0], 0, 0)),
        pl.BlockSpec((None, c, d), tok_prev),
    ] + [pl.BlockSpec((None, c, d), tok_cur)] * 4 + [
        _const_spec((1, d)), _const_spec((1, d)), _const_spec((1, d)),
        _const_spec(w_out.shape), _const_spec(w1.shape), _const_spec(w2.shape),
    ]
    args = [x, mod, gate, o_f, q, v, f_b, onorm, g2, fg, w_out, w1, w2]
    if state is not None:
        in_specs.append(pl.BlockSpec((None, None, None, HEADS, HEAD_W, HEAD_W),
                                     lambda s: (cur(s)[0], layer_a, 1, 0, 0, 0)))
        args.append(state)
    st_spec = pl.BlockSpec((None, HEADS, HEAD_W, HEAD_W), lambda s: (cur(s)[0], 0, 0, 0))
    return pl.pallas_call(
        functools.partial(_hg_bwd_kernel, has_s0=state is not None, final=final, n_blocks=n),
        grid=(total + 1,),
        in_specs=in_specs,
        out_specs=[pl.BlockSpec((None, c, d), tok_prev), st_spec],
        out_shape=[
            jax.ShapeDtypeStruct((bn, l, d), F32),
            jax.ShapeDtypeStruct((bn, HEADS, HEAD_W, HEAD_W), F32),
        ],
        scratch_shapes=_scan_scratch(d) + [pltpu.VMEM((c, d), F32)],
        compiler_params=pltpu.CompilerParams(
            dimension_semantics=("arbitrary",), vmem_limit_bytes=VMEM_LIMIT),
        name="hg_bwd",
    )(*args)


def _cm_kernel(x_ref, mod_ref, g1_ref, g2_ref, fg_ref, win_ref, lng_ref, lnb_ref, ws_ref, bst_ref,
               wo_ref, w1_ref, w2_ref, y_ref, *, final):
    tb, d = x_ref.shape
    n_chunks = tb // MIX_CHUNK
    gw = d // CM_GROUPS
    x = x_ref[...]
    sh, sc, gt1 = mod_ref[0:1, :], mod_ref[1:2, :], mod_ref[2:3, :]
    h = (_rms(x, g1_ref[...]) * (1.0 + sc) + sh).astype(BF16)
    u = _gelu_tanh(_dot(h, win_ref[:, :d]))
    vv = _gelu_tanh(_dot(h, win_ref[:, d:]))
    mu = jnp.mean(vv, axis=-1, keepdims=True)
    vc = vv - mu
    var = jnp.mean(vc * vc, axis=-1, keepdims=True)
    vn = (vc * lax.rsqrt(var + EPS) * lng_ref[...] + lnb_ref[...]).astype(BF16)
    per_group = []
    for g in range(CM_GROUPS):
        rhs = jnp.concatenate(
            [vn[ch * MIX_CHUNK:(ch + 1) * MIX_CHUNK, g * gw:(g + 1) * gw] for ch in range(n_chunks)], axis=1)
        per_group.append(_dot(ws_ref[g], rhs) + bst_ref[:, g:g + 1])
    s = jnp.concatenate(
        [jnp.concatenate([pg[:, ch * gw:(ch + 1) * gw] for pg in per_group], axis=1) for ch in range(n_chunks)],
        axis=0)
    x1 = x + gt1 * _dot((u * s).astype(BF16), wo_ref[...])
    x2 = _mlp_tail(x1, mod_ref, g2_ref[...], w1_ref, w2_ref)
    y_ref[...] = _rms(x2, fg_ref[...]) if final else x2


def _cm_call(x, mod, mod_row0, mod_stride, g1, g2, fg, w_in, ln_g, ln_b, w_s, b_st, w_out, w1, w2, final):
    bn, l, d = x.shape
    tb = min(CM_TOKEN_BLOCK, l)
    assert l % tb == 0 and tb % MIX_CHUNK == 0
    tok = lambda b, j: (b, j, 0)
    consts = [g1, g2, fg, w_in, ln_g, ln_b, w_s, b_st, w_out, w1, w2]
    return pl.pallas_call(
        functools.partial(_cm_kernel, final=final),
        grid=(bn, l // tb),
        in_specs=[
            pl.BlockSpec((None, tb, d), tok),
            pl.BlockSpec((None, N_MOD, d), lambda b, j: (mod_row0 + mod_stride * b, 0, 0)),
        ] + [_const_spec(a.shape) for a in consts],
        out_specs=pl.BlockSpec((None, tb, d), tok),
        out_shape=jax.ShapeDtypeStruct((bn, l, d), F32),
        compiler_params=pltpu.CompilerParams(
            dimension_semantics=("arbitrary", "arbitrary"), vmem_limit_bytes=VMEM_LIMIT),
        name="cm",
    )(x, mod, *consts)


def kernel(x_prompt, x_sample, state_hgrn, c, c_ctx, ada_w, ada_b, norm_mix_g, norm_mlp_g, mlp_w1, mlp_w2,
           hgrn_w_in, hgrn_lb_logits, hgrn_onorm_g, hgrn_w_out, cm_w_in, cm_ln_g, cm_ln_b, cm_w_s, cm_b_s,
           cm_w_out, final_norm_g):
    depth, d, _ = ada_w.shape
    n_lat = c.shape[0]
    assert d % PAIR_W == 0 and d // HEAD_W == HEADS and 1 + n_lat <= COND_ROWS
    assert x_prompt.shape[1] % SCAN_CHUNK == 0 and x_sample.shape[1] % SCAN_CHUNK == 0

    cond = jnp.concatenate([c_ctx[None, :], c, jnp.zeros((COND_ROWS - 1 - n_lat, d), F32)], axis=0)
    mods = _ada_call(cond, ada_w, ada_b).reshape(depth, COND_ROWS, N_MOD, d)

    row = lambda a: a.reshape(1, d)
    lbl = hgrn_lb_logits.reshape(-1, d)
    fg = row(final_norm_g)
    ctx, lat = x_prompt, x_sample
    new_states = []
    for i in range(depth):
        j = i // N_MIXERS
        final = i == depth - 1
        w1 = mlp_w1[i].astype(BF16)
        w2 = mlp_w2[i].astype(BF16)
        g1, g2 = row(norm_mix_g[i]), row(norm_mlp_g[i])
        if i % N_MIXERS == 0:
            w_in = hgrn_w_in[j].astype(BF16)
            w_out = hgrn_w_out[j].astype(BF16)
            onorm = row(hgrn_onorm_g[j])
            *stash, s_f = _hg_fwd_call(ctx, mods[i], 0, 0, g1, w_in, lbl, None, j)
            ctx, s_b = _hg_bwd_call(ctx, mods[i], 0, 0, stash, onorm, g2, fg, w_out, w1, w2, None, j, final)
            new_states.append(jnp.stack([s_f, s_b], axis=1))
            *stash, _ = _hg_fwd_call(lat, mods[i], 1, 1, g1, w_in, lbl, state_hgrn, j)
            lat, _ = _hg_bwd_call(lat, mods[i], 1, 1, stash, onorm, g2, fg, w_out, w1, w2, state_hgrn, j, final)
        else:
            cm_args = (g1, g2, fg, cm_w_in[j].astype(BF16), row(cm_ln_g[j]), row(cm_ln_b[j]),
                       cm_w_s[j].astype(BF16), cm_b_s[j].T, cm_w_out[j].astype(BF16), w1, w2, final)
            ctx = _cm_call(ctx, mods[i], 0, 0, *cm_args)
            lat = _cm_call(lat, mods[i], 1, 1, *cm_args)
    new_state = jnp.stack(new_states, axis=1).astype(x_prompt.dtype)
    return ctx, lat, new_state
```

```python
import functools

import jax
import jax.numpy as jnp
from jax import lax
from jax.experimental import pallas as pl
from jax.experimental.pallas import tpu as pltpu

F32 = jnp.float32
BF16 = jnp.bfloat16

EPS = 1e-6
N_MOD = 6
N_MIXERS = 2
HEADS = 8
HEAD_W = 128
PAIR_W = 2 * HEAD_W
MIX_CHUNK = 128
CM_GROUPS = 8
SCAN_BLOCK = 64
SCAN_CHUNK = 2 * SCAN_BLOCK
SUBLANES = 8
COND_ROWS = SUBLANES
SAFE_LOG2_DECAY = -115.0
HG_TOKEN_BLOCK = 256
CM_TOKEN_BLOCK = 512
VMEM_LIMIT = 56 * 1024 * 1024


def _silu(x):
    h = 0.5 * x
    return h * jnp.tanh(h) + h


def _forget_gate(z, lb):
    return 0.5 * (1.0 + lb) + (0.5 * (1.0 - lb)) * jnp.tanh(0.5 * z)


def _gelu_tanh(x):
    return 0.5 * x * (1.0 + jnp.tanh(0.7978845608028654 * (x + 0.044715 * (x * x * x))))


def _rms(x, g):
    return x * lax.rsqrt(jnp.mean(x * x, axis=-1, keepdims=True) + EPS) * g


def _dot(a, b):
    return jnp.dot(a, b, preferred_element_type=F32)


def _dot_nt(a, b):
    return lax.dot_general(a, b, (((1,), (1,)), ((), ())), preferred_element_type=F32)


def _dot_tn(a, b):
    return lax.dot_general(a, b, (((0,), (0,)), ((), ())), preferred_element_type=F32)


def _ada_kernel(cond_ref, w_ref, b_ref, out_ref):
    a = _silu(cond_ref[...]).astype(BF16)
    out_ref[...] = _dot(a, w_ref[...].astype(BF16)) + b_ref[...]


def _ada_call(cond, ada_w, ada_b):
    depth, d, _ = ada_w.shape
    return pl.pallas_call(
        _ada_kernel,
        grid=(depth, N_MOD),
        in_specs=[
            pl.BlockSpec((COND_ROWS, d), lambda i, n: (0, 0)),
            pl.BlockSpec((None, d, d), lambda i, n: (i, 0, n)),
            pl.BlockSpec((None, 1, d), lambda i, n: (i, 0, n)),
        ],
        out_specs=pl.BlockSpec((None, COND_ROWS, d), lambda i, n: (i, 0, n)),
        out_shape=jax.ShapeDtypeStruct((depth, COND_ROWS, N_MOD * d), F32),
        compiler_params=pltpu.CompilerParams(dimension_semantics=("arbitrary", "arbitrary")),
        name="ada",
    )(cond, ada_w, ada_b.reshape(depth, 1, N_MOD * d))


def _scan_masks(rev):
    c = SCAN_CHUNK
    t = lax.broadcasted_iota(jnp.int32, (c, 2 * c), 0)
    s = lax.broadcasted_iota(jnp.int32, (c, 2 * c), 1) & (c - 1)
    t_hi = t >= SCAN_BLOCK
    s_hi = s >= SCAN_BLOCK
    causal = (s >= t) if rev else (s <= t)
    diag = (t_hi == s_hi) & causal
    cross = (jnp.logical_not(t_hi) & s_hi) if rev else (t_hi & jnp.logical_not(s_hi))
    return diag, cross


def _block_cumsum(g, rev):
    c = g.shape[0]
    t = lax.broadcasted_iota(jnp.int32, (c, 2 * c), 0)
    s = lax.broadcasted_iota(jnp.int32, (c, 2 * c), 1) & (c - 1)
    same = (t >= SCAN_BLOCK) == (s >= SCAN_BLOCK)
    tri = jnp.where(same & ((s >= t) if rev else (s <= t)), 1.0, 0.0).astype(BF16)
    g_hi = g.astype(BF16)
    g_lo = (g - g_hi.astype(F32)).astype(BF16)
    return _dot(tri, jnp.concatenate([g_hi, g_lo], axis=0))


def _scan_prep(q, k, g, v, rev):
    c, d = q.shape
    row = lax.broadcasted_iota(jnp.int32, (c, 1), 0)
    hi = row >= SCAN_BLOCK
    near = hi if rev else jnp.logical_not(hi)
    far = jnp.logical_not(near)
    b_in = _block_cumsum(g, rev)

    half = SCAN_BLOCK // 2
    if rev:
        tot_lo, tot_hi = b_in[0:1], b_in[SCAN_BLOCK:SCAN_BLOCK + 1]
        mid_lo, mid_hi = b_in[half:half + 1], b_in[SCAN_BLOCK + half:SCAN_BLOCK + half + 1]
        tot_near, tot_far = tot_hi, tot_lo
    else:
        tot_lo, tot_hi = b_in[SCAN_BLOCK - 1:SCAN_BLOCK], b_in[c - 1:c]
        mid_lo, mid_hi = b_in[half - 1:half], b_in[SCAN_BLOCK + half - 1:SCAN_BLOCK + half]
        tot_near, tot_far = tot_lo, tot_hi
    tot_own = jnp.where(hi, tot_hi, tot_lo)
    mid_own = jnp.where(hi, mid_hi, mid_lo)

    p = jnp.exp2(b_in)
    e = jnp.exp2(tot_own - b_in)
    qh = q * p
    ke = k * e
    qi = (qh * jnp.where(far, jnp.exp2(tot_near), 1.0)).astype(BF16)
    ks = (ke * jnp.where(near, jnp.exp2(tot_far), 1.0)).astype(BF16)
    dec = jnp.exp2(tot_near + tot_far)
    qh16 = qh.astype(BF16)
    ke16 = ke.astype(BF16)
    v16 = v.astype(BF16)
    qm16 = (q * jnp.exp2(b_in - mid_own)).astype(BF16)
    km16 = (k * jnp.exp2(mid_own - b_in)).astype(BF16)

    half_sums = jnp.minimum(jnp.minimum(mid_lo, tot_lo - mid_lo), jnp.minimum(mid_hi, tot_hi - mid_hi))
    fast_vec = jnp.min(half_sums, axis=-1, keepdims=True) >= SAFE_LOG2_DECAY
    fast = jnp.min(half_sums) >= SAFE_LOG2_DECAY

    return (qm16, km16, qh16, ke16, v16, qi, ks, dec, fast_vec), fast


def _scan_finish(operands, st_ref, rev):
    qm16, km16, qh16, ke16, v16, qi, ks, dec, fast_vec = operands
    n_pairs = qi.shape[1] // PAIR_W
    diag_mask, cross_mask = _scan_masks(rev)
    diag_mask = diag_mask & fast_vec
    left = lax.broadcasted_iota(jnp.int32, (1, PAIR_W), 1) < HEAD_W
    bd_mask = (lax.broadcasted_iota(jnp.int32, (PAIR_W, PAIR_W), 0) < HEAD_W) == (
        lax.broadcasted_iota(jnp.int32, (PAIR_W, PAIR_W), 1) < HEAD_W)

    def by_head(a):
        zero = jnp.zeros_like(a)
        return jnp.concatenate([jnp.where(left, a, zero), jnp.where(left, zero, a)], axis=0)

    outs = []
    for pr in range(n_pairs):
        sl = slice(pr * PAIR_W, (pr + 1) * PAIR_W)
        attd = _dot_nt(qm16[:, sl], by_head(km16[:, sl]))
        attx = _dot_nt(qh16[:, sl], by_head(ke16[:, sl]))
        att = jnp.where(diag_mask, attd, jnp.where(cross_mask, attx, 0.0)).astype(BF16)
        st = st_ref[pr]
        outs.append(_dot(att, by_head(v16[:, sl])) + _dot_nt(qi[:, sl], st.astype(BF16)))
        upd = _dot_tn(v16[:, sl], ks[:, sl])
        st_ref[pr] = st * dec[:, sl] + jnp.where(bd_mask, upd, 0.0)
    return jnp.concatenate(outs, axis=1)


def _scan_slow_fix(q_ref, v_ref, f_ref, rows, o_ref, od_ref, slow_refs, rev):
    q_s, k_s, v_s, b_s = slow_refs
    c, d = od_ref.shape
    f = f_ref[rows, :]
    q_s[...] = q_ref[rows, :].astype(F32)
    k_s[...] = 1.0 - f
    v_s[...] = v_ref[rows, :].astype(F32)
    b_s[...] = _block_cumsum(jnp.log2(f), rev)
    row = lax.broadcasted_iota(jnp.int32, (c, 1), 0)

    def body(grp, carry):
        t0 = pl.multiple_of(grp * SUBLANES, SUBLANES)
        q8 = q_s[pl.ds(t0, SUBLANES), :]
        b8 = b_s[pl.ds(t0, SUBLANES), :]
        out_rows = []
        for r in range(SUBLANES):
            ti = t0 + r
            prod = q8[r:r + 1] * k_s[...] * jnp.exp2(jnp.minimum(b8[r:r + 1] - b_s[...], 0.0))
            valid = ((row >= SCAN_BLOCK) == (ti >= SCAN_BLOCK)) & ((row >= ti) if rev else (row <= ti))
            heads = []
            for h in range(d // HEAD_W):
                hs = slice(h * HEAD_W, (h + 1) * HEAD_W)
                w = jnp.where(valid, jnp.sum(prod[:, hs], axis=-1, keepdims=True), 0.0)
                heads.append(jnp.sum(w * v_s[:, hs], axis=0, keepdims=True))
            out_rows.append(jnp.concatenate(heads, axis=1))
        od_ref[pl.ds(t0, SUBLANES), :] = jnp.concatenate(out_rows, axis=0)
        return carry

    lax.fori_loop(0, c // SUBLANES, body, 0)
    o_ref[rows, :] += od_ref[...]


def _state_init(st_ref, s0_ref):
    st_ref[...] = jnp.zeros_like(st_ref)
    if s0_ref is not None:
        for h in range(HEADS):
            o = (h % 2) * HEAD_W
            st_ref[h // 2, o:o + HEAD_W, o:o + HEAD_W] = s0_ref[h].T


def _state_write(st_ref, out_ref):
    for h in range(HEADS):
        o = (h % 2) * HEAD_W
        out_ref[h] = st_ref[h // 2, o:o + HEAD_W, o:o + HEAD_W].T


def _lower_bound(lbl_ref, layer, direction):
    n_layers = lbl_ref.shape[0] // 2
    rows = [lbl_ref[2 * i + direction:2 * i + direction + 1, :] for i in range(n_layers)]
    m = functools.reduce(jnp.maximum, rows)
    ex = [jnp.exp(r - m) for r in rows]
    return sum(ex[:layer + 1]) / sum(ex)


def _chunk_rows(n_rows, rev):
    order = range(n_rows // SCAN_CHUNK)
    return [slice(ci * SCAN_CHUNK, (ci + 1) * SCAN_CHUNK) for ci in (reversed(order) if rev else order)]


def _scan_scratch(d):
    c = SCAN_CHUNK
    return [
        pltpu.VMEM((d // PAIR_W, PAIR_W, PAIR_W), F32),
        pltpu.VMEM((c, d), F32),
        pltpu.VMEM((c, d), F32), pltpu.VMEM((c, d), F32),
        pltpu.VMEM((c, d), F32), pltpu.VMEM((c, d), F32),
    ]


def _const_spec(shape):
    nd = len(shape)
    return pl.BlockSpec(shape, lambda *_: (0,) * nd, pipeline_mode=pl.Buffered(1))


def _hg_fwd_kernel(*refs, has_s0, layer):
    if has_s0:
        x_ref, mod_ref, g_ref, w_ref, lbl_ref, s0_ref = refs[:6]
        rest = refs[6:]
    else:
        x_ref, mod_ref, g_ref, w_ref, lbl_ref = refs[:5]
        s0_ref = None
        rest = refs[5:]
    of_ref, q_ref, v_ref, fb_ref, gate_ref, sf_ref = rest[:6]
    st_ref, od_ref = rest[6:8]
    slow_refs = rest[8:12]
    ff_ref = rest[12]
    j = pl.program_id(1)
    d = x_ref.shape[-1]

    @pl.when(j == 0)
    def _():
        _state_init(st_ref, s0_ref)

    sh, sc = mod_ref[0:1, :], mod_ref[1:2, :]
    h = (_rms(x_ref[...], g_ref[...]) * (1.0 + sc) + sh).astype(BF16)
    proj = lambda i: _dot(h, w_ref[:, i * d:(i + 1) * d])
    lb_f = _lower_bound(lbl_ref, layer, 0)
    lb_b = _lower_bound(lbl_ref, layer, 1)
    q = _silu(proj(0))
    f_f = _forget_gate(proj(1), lb_f)
    zi = proj(3)
    q_ref[...] = q.astype(BF16)
    v_ref[...] = zi.astype(BF16)
    ff_ref[...] = f_f
    k_f = 1.0 - f_f
    g_f = jnp.log2(f_f)

    def backward_gate():
        fb_ref[...] = _forget_gate(proj(2), lb_b)

    def output_gate():
        gate_ref[...] = _silu(proj(4)).astype(BF16)

    fillers = [backward_gate, output_gate]
    flags = []
    for rows in _chunk_rows(x_ref.shape[0], rev=False):
        operands, fast = _scan_prep(q[rows], k_f[rows], g_f[rows], zi[rows], rev=False)
        if fillers:
            fillers.pop(0)()
        of_ref[rows, :] = _scan_finish(operands, st_ref, rev=False)
        flags.append((rows, fast))
    for filler in fillers:
        filler()
    for rows, fast in flags:
        @pl.when(jnp.logical_not(fast))
        def _():
            _scan_slow_fix(q_ref, v_ref, ff_ref, rows, of_ref, od_ref, slow_refs, rev=False)

    @pl.when(j == pl.num_programs(1) - 1)
    def _():
        _state_write(st_ref, sf_ref)


def _hg_fwd_call(x, mod, mod_row0, mod_stride, g, w_in, lbl, state, layer_a):
    bn, l, d = x.shape
    c = min(HG_TOKEN_BLOCK, l)
    assert l % c == 0 and c % SCAN_CHUNK == 0
    n = l // c
    tok = lambda b, j: (b, j, 0)
    in_specs = [
        pl.BlockSpec((None, c, d), tok),
        pl.BlockSpec((None, N_MOD, d), lambda b, j: (mod_row0 + mod_stride * b, 0, 0)),
        _const_spec((1, d)),
        _const_spec(w_in.shape),
        _const_spec(lbl.shape),
    ]
    args = [x, mod, g, w_in, lbl]
    if state is not None:
        in_specs.append(pl.BlockSpec((None, None, None, HEADS, HEAD_W, HEAD_W),
                                     lambda b, j: (b, layer_a, 0, 0, 0, 0)))
        args.append(state)
    st_spec = pl.BlockSpec((None, HEADS, HEAD_W, HEAD_W), lambda b, j: (b, 0, 0, 0))
    return pl.pallas_call(
        functools.partial(_hg_fwd_kernel, has_s0=state is not None, layer=layer_a),
        grid=(bn, n),
        in_specs=in_specs,
        out_specs=[pl.BlockSpec((None, c, d), tok)] * 5 + [st_spec],
        out_shape=[
            jax.ShapeDtypeStruct((bn, l, d), F32),
            jax.ShapeDtypeStruct((bn, l, d), BF16),
            jax.ShapeDtypeStruct((bn, l, d), BF16),
            jax.ShapeDtypeStruct((bn, l, d), F32),
            jax.ShapeDtypeStruct((bn, l, d), BF16),
            jax.ShapeDtypeStruct((bn, HEADS, HEAD_W, HEAD_W), F32),
        ],
        scratch_shapes=_scan_scratch(d) + [pltpu.VMEM((c, d), F32)],
        compiler_params=pltpu.CompilerParams(
            dimension_semantics=("arbitrary", "arbitrary"), vmem_limit_bytes=VMEM_LIMIT),
        name="hg_fwd",
    )(*args)


def _mlp_tail(x1, mod_ref, g2, w1_ref, w2_ref):
    sh2, sc2, gt2 = mod_ref[3:4, :], mod_ref[4:5, :], mod_ref[5:6, :]
    h2 = (_rms(x1, g2) * (1.0 + sc2) + sh2).astype(BF16)
    a = jnp.maximum(_dot(h2, w1_ref[...]), 0.0)
    return x1 + gt2 * _dot((a * a).astype(BF16), w2_ref[...])


def _hg_bwd_kernel(*refs, has_s0, final, n_blocks):
    x_ref, mod_ref, gate_ref, of_ref, q_ref, v_ref, fb_ref, on_ref, g2_ref, fg_ref = refs[:10]
    wo_ref, w1_ref, w2_ref = refs[10:13]
    if has_s0:
        s0_ref = refs[13]
        rest = refs[14:]
    else:
        s0_ref = None
        rest = refs[13:]
    y_ref, sb_ref = rest[:2]
    st_ref, od_ref = rest[2:4]
    slow_refs = rest[4:8]
    o_ref = rest[8]
    s = pl.program_id(0)
    total = pl.num_programs(0) - 1
    live = s < total
    j = jnp.minimum(s, total - 1) % n_blocks
    d = x_ref.shape[-1]

    @pl.when(s == 0)
    def _():
        o_ref[...] = jnp.zeros_like(o_ref)

    @pl.when(live & (j == 0))
    def _():
        _state_init(st_ref, s0_ref)

    o = o_ref[...]
    parts = []
    for h in range(d // HEAD_W):
        oh = o[:, h * HEAD_W:(h + 1) * HEAD_W]
        parts.append(oh * lax.rsqrt(jnp.mean(oh * oh, axis=-1, keepdims=True) + EPS))
    o = jnp.concatenate(parts, axis=1) * on_ref[...] * gate_ref[...].astype(F32)
    gt1 = mod_ref[2:3, :]
    x1 = x_ref[...] + gt1 * _dot(o.astype(BF16), wo_ref[...])
    sh2, sc2, gt2 = mod_ref[3:4, :], mod_ref[4:5, :], mod_ref[5:6, :]
    h2 = (_rms(x1, g2_ref[...]) * (1.0 + sc2) + sh2).astype(BF16)

    chunk_rows = _chunk_rows(q_ref.shape[0], rev=True)
    n_slabs = 2 * len(chunk_rows)
    slab_w = w1_ref.shape[1] // n_slabs

    def mlp_slab(i):
        a = jnp.maximum(_dot(h2, w1_ref[:, i * slab_w:(i + 1) * slab_w]), 0.0)
        return _dot((a * a).astype(BF16), w2_ref[i * slab_w:(i + 1) * slab_w, :])

    mlp = None
    flags = []
    for ci, rows in enumerate(chunk_rows):
        f_b = fb_ref[rows, :]
        operands, fast = _scan_prep(q_ref[rows, :].astype(F32), 1.0 - f_b, jnp.log2(f_b),
                                    v_ref[rows, :].astype(F32), rev=True)
        part = mlp_slab(2 * ci)
        mlp = part if mlp is None else mlp + part
        o_ref[rows, :] = of_ref[rows, :] + _scan_finish(operands, st_ref, rev=True)
        mlp = mlp + mlp_slab(2 * ci + 1)
        flags.append((rows, fast))
    x2 = x1 + gt2 * mlp
    y_ref[...] = _rms(x2, fg_ref[...]) if final else x2

    for rows, fast in flags:
        @pl.when(jnp.logical_not(fast))
        def _():
            _scan_slow_fix(q_ref, v_ref, fb_ref, rows, o_ref, od_ref, slow_refs, rev=True)

    @pl.when(live & (j == n_blocks - 1))
    def _():
        _state_write(st_ref, sb_ref)


def _hg_bwd_call(x, mod, mod_row0, mod_stride, stash, onorm, g2, fg, w_out, w1, w2, state, layer_a, final):
    bn, l, d = x.shape
    c = min(HG_TOKEN_BLOCK, l)
    assert l % c == 0 and c % SCAN_CHUNK == 0
    n = l // c
    total = bn * n

    def cur(s):
        t = jnp.minimum(s, total - 1)
        return t // n, n - 1 - t % n

    def prev(s):
        t = jnp.maximum(s - 1, 0)
        return t // n, n - 1 - t % n

    tok_cur = lambda s: (*cur(s), 0)
    tok_prev = lambda s: (*prev(s), 0)
    o_f, q, v, f_b, gate = stash
    in_specs = [
        pl.BlockSpec((None, c, d), tok_prev),
        pl.BlockSpec((None, N_MOD, d), lambda s: (mod_row0 + mod_stride * prev(s)[0], 0, 0)),
        pl.BlockSpec((None, c, d), tok_prev),
    ] + [pl.BlockSpec((None, c, d), tok_cur)] * 4 + [
        _const_spec((1, d)), _const_spec((1, d)), _const_spec((1, d)),
        _const_spec(w_out.shape), _const_spec(w1.shape), _const_spec(w2.shape),
    ]
    args = [x, mod, gate, o_f, q, v, f_b, onorm, g2, fg, w_out, w1, w2]
    if state is not None:
        in_specs.append(pl.BlockSpec((None, None, None, HEADS, HEAD_W, HEAD_W),
                                     lambda s: (cur(s)[0], layer_a, 1, 0, 0, 0)))
        args.append(state)
    st_spec = pl.BlockSpec((None, HEADS, HEAD_W, HEAD_W), lambda s: (cur(s)[0], 0, 0, 0))
    return pl.pallas_call(
        functools.partial(_hg_bwd_kernel, has_s0=state is not None, final=final, n_blocks=n),
        grid=(total + 1,),
        in_specs=in_specs,
        out_specs=[pl.BlockSpec((None, c, d), tok_prev), st_spec],
        out_shape=[
            jax.ShapeDtypeStruct((bn, l, d), F32),
            jax.ShapeDtypeStruct((bn, HEADS, HEAD_W, HEAD_W), F32),
        ],
        scratch_shapes=_scan_scratch(d) + [pltpu.VMEM((c, d), F32)],
        compiler_params=pltpu.CompilerParams(
            dimension_semantics=("arbitrary",), vmem_limit_bytes=VMEM_LIMIT),
        name="hg_bwd",
    )(*args)


def _cm_kernel(x_ref, mod_ref, g1_ref, g2_ref, fg_ref, win_ref, lng_ref, lnb_ref, ws_ref, bst_ref,
               wo_ref, w1_ref, w2_ref, y_ref, *, final):
    tb, d = x_ref.shape
    n_chunks = tb // MIX_CHUNK
    gw = d // CM_GROUPS
    x = x_ref[...]
    sh, sc, gt1 = mod_ref[0:1, :], mod_ref[1:2, :], mod_ref[2:3, :]
    h = (_rms(x, g1_ref[...]) * (1.0 + sc) + sh).astype(BF16)
    u = _gelu_tanh(_dot(h, win_ref[:, :d]))
    vv = _gelu_tanh(_dot(h, win_ref[:, d:]))
    mu = jnp.mean(vv, axis=-1, keepdims=True)
    vc = vv - mu
    var = jnp.mean(vc * vc, axis=-1, keepdims=True)
    vn = (vc * lax.rsqrt(var + EPS) * lng_ref[...] + lnb_ref[...]).astype(BF16)
    per_group = []
    for g in range(CM_GROUPS):
        rhs = jnp.concatenate(
            [vn[ch * MIX_CHUNK:(ch + 1) * MIX_CHUNK, g * gw:(g + 1) * gw] for ch in range(n_chunks)], axis=1)
        per_group.append(_dot(ws_ref[g], rhs) + bst_ref[:, g:g + 1])
    s = jnp.concatenate(
        [jnp.concatenate([pg[:, ch * gw:(ch + 1) * gw] for pg in per_group], axis=1) for ch in range(n_chunks)],
        axis=0)
    x1 = x + gt1 * _dot((u * s).astype(BF16), wo_ref[...])
    x2 = _mlp_tail(x1, mod_ref, g2_ref[...], w1_ref, w2_ref)
    y_ref[...] = _rms(x2, fg_ref[...]) if final else x2


def _cm_call(x, mod, mod_row0, mod_stride, g1, g2, fg, w_in, ln_g, ln_b, w_s, b_st, w_out, w1, w2, final):
    bn, l, d = x.shape
    tb = min(CM_TOKEN_BLOCK, l)
    assert l % tb == 0 and tb % MIX_CHUNK == 0
    tok = lambda b, j: (b, j, 0)
    consts = [g1, g2, fg, w_in, ln_g, ln_b, w_s, b_st, w_out, w1, w2]
    return pl.pallas_call(
        functools.partial(_cm_kernel, final=final),
        grid=(bn, l // tb),
        in_specs=[
            pl.BlockSpec((None, tb, d), tok),
            pl.BlockSpec((None, N_MOD, d), lambda b, j: (mod_row0 + mod_stride * b, 0, 0)),
        ] + [_const_spec(a.shape) for a in consts],
        out_specs=pl.BlockSpec((None, tb, d), tok),
        out_shape=jax.ShapeDtypeStruct((bn, l, d), F32),
        compiler_params=pltpu.CompilerParams(
            dimension_semantics=("arbitrary", "arbitrary"), vmem_limit_bytes=VMEM_LIMIT),
        name="cm",
    )(x, mod, *consts)


def kernel(x_prompt, x_sample, state_hgrn, c, c_ctx, ada_w, ada_b, norm_mix_g, norm_mlp_g, mlp_w1, mlp_w2,
           hgrn_w_in, hgrn_lb_logits, hgrn_onorm_g, hgrn_w_out, cm_w_in, cm_ln_g, cm_ln_b, cm_w_s, cm_b_s,
           cm_w_out, final_norm_g):
    depth, d, _ = ada_w.shape
    n_lat = c.shape[0]
    assert d % PAIR_W == 0 and d // HEAD_W == HEADS and 1 + n_lat <= COND_ROWS
    assert x_prompt.shape[1] % SCAN_CHUNK == 0 and x_sample.shape[1] % SCAN_CHUNK == 0

    cond = jnp.concatenate([c_ctx[None, :], c, jnp.zeros((COND_ROWS - 1 - n_lat, d), F32)], axis=0)
    mods = _ada_call(cond, ada_w, ada_b).reshape(depth, COND_ROWS, N_MOD, d)

    row = lambda a: a.reshape(1, d)
    lbl = hgrn_lb_logits.reshape(-1, d)
    fg = row(final_norm_g)
    ctx, lat = x_prompt, x_sample
    new_states = []
    for i in range(depth):
        j = i // N_MIXERS
        final = i == depth - 1
        w1 = mlp_w1[i].astype(BF16)
        w2 = mlp_w2[i].astype(BF16)
        g1, g2 = row(norm_mix_g[i]), row(norm_mlp_g[i])
        if i % N_MIXERS == 0:
            w_in = hgrn_w_in[j].astype(BF16)
            w_out = hgrn_w_out[j].astype(BF16)
            onorm = row(hgrn_onorm_g[j])
            *stash, s_f = _hg_fwd_call(ctx, mods[i], 0, 0, g1, w_in, lbl, None, j)
            ctx, s_b = _hg_bwd_call(ctx, mods[i], 0, 0, stash, onorm, g2, fg, w_out, w1, w2, None, j, final)
            new_states.append(jnp.stack([s_f, s_b], axis=1))
            *stash, _ = _hg_fwd_call(lat, mods[i], 1, 1, g1, w_in, lbl, state_hgrn, j)
            lat, _ = _hg_bwd_call(lat, mods[i], 1, 1, stash, onorm, g2, fg, w_out, w1, w2, state_hgrn, j, final)
        else:
            cm_args = (g1, g2, fg, cm_w_in[j].astype(BF16), row(cm_ln_g[j]), row(cm_ln_b[j]),
                       cm_w_s[j].astype(BF16), cm_b_s[j].T, cm_w_out[j].astype(BF16), w1, w2, final)
            ctx = _cm_call(ctx, mods[i], 0, 0, *cm_args)
            lat = _cm_call(lat, mods[i], 1, 1, *cm_args)
    new_state = jnp.stack(new_states, axis=1).astype(x_prompt.dtype)
    return ctx, lat, new_state
```

```python
import functools

import jax
import jax.numpy as jnp
from jax import lax
from jax.experimental import pallas as pl
from jax.experimental.pallas import tpu as pltpu

F32 = jnp.float32
BF16 = jnp.bfloat16

EPS = 1e-6
N_MOD = 6
N_MIXERS = 2
HEADS = 8
HEAD_W = 128
PAIR_W = 2 * HEAD_W
MIX_CHUNK = 128
CM_GROUPS = 8
SCAN_BLOCK = 64
SCAN_CHUNK = 2 * SCAN_BLOCK
SUBLANES = 8
COND_ROWS = SUBLANES
SAFE_LOG2_DECAY = -115.0
HG_TOKEN_BLOCK = 512
CM_TOKEN_BLOCK = 512
VMEM_LIMIT = 56 * 1024 * 1024


def _silu(x):
    h = 0.5 * x
    return h * jnp.tanh(h) + h


def _forget_gate(z, lb):
    return 0.5 * (1.0 + lb) + (0.5 * (1.0 - lb)) * jnp.tanh(0.5 * z)


def _gelu_tanh(x):
    return 0.5 * x * (1.0 + jnp.tanh(0.7978845608028654 * (x + 0.044715 * (x * x * x))))


def _rms(x, g):
    return x * lax.rsqrt(jnp.mean(x * x, axis=-1, keepdims=True) + EPS) * g


def _dot(a, b):
    return jnp.dot(a, b, preferred_element_type=F32)


def _dot_nt(a, b):
    return lax.dot_general(a, b, (((1,), (1,)), ((), ())), preferred_element_type=F32)


def _dot_tn(a, b):
    return lax.dot_general(a, b, (((0,), (0,)), ((), ())), preferred_element_type=F32)


def _ada_kernel(cond_ref, w_ref, b_ref, out_ref):
    a = _silu(cond_ref[...]).astype(BF16)
    out_ref[...] = _dot(a, w_ref[...].astype(BF16)) + b_ref[...]


def _ada_call(cond, ada_w, ada_b):
    depth, d, _ = ada_w.shape
    return pl.pallas_call(
        _ada_kernel,
        grid=(depth, N_MOD),
        in_specs=[
            pl.BlockSpec((COND_ROWS, d), lambda i, n: (0, 0)),
            pl.BlockSpec((None, d, d), lambda i, n: (i, 0, n)),
            pl.BlockSpec((None, 1, d), lambda i, n: (i, 0, n)),
        ],
        out_specs=pl.BlockSpec((None, COND_ROWS, d), lambda i, n: (i, 0, n)),
        out_shape=jax.ShapeDtypeStruct((depth, COND_ROWS, N_MOD * d), F32),
        compiler_params=pltpu.CompilerParams(dimension_semantics=("arbitrary", "arbitrary")),
        name="ada",
    )(cond, ada_w, ada_b.reshape(depth, 1, N_MOD * d))


def _scan_masks(rev):
    c = SCAN_CHUNK
    t = lax.broadcasted_iota(jnp.int32, (c, 2 * c), 0)
    s = lax.broadcasted_iota(jnp.int32, (c, 2 * c), 1) & (c - 1)
    causal = (s >= t) if rev else (s <= t)
    diag = ((t >= SCAN_BLOCK) == (s >= SCAN_BLOCK)) & causal
    s_row = lax.broadcasted_iota(jnp.int32, (1, 2 * c), 1) & (c - 1)
    near_keys = (s_row >= SCAN_BLOCK) if rev else (s_row < SCAN_BLOCK)
    return diag, near_keys


def _block_cumsum(g, rev):
    c = g.shape[0]
    t = lax.broadcasted_iota(jnp.int32, (c, 2 * c), 0)
    s = lax.broadcasted_iota(jnp.int32, (c, 2 * c), 1) & (c - 1)
    same = (t >= SCAN_BLOCK) == (s >= SCAN_BLOCK)
    tri = jnp.where(same & ((s >= t) if rev else (s <= t)), 1.0, 0.0).astype(BF16)
    g_hi = g.astype(BF16)
    g_lo = (g - g_hi.astype(F32)).astype(BF16)
    return _dot(tri, jnp.concatenate([g_hi, g_lo], axis=0))


def _scan_prep(q, k, g, v, rev):
    c, d = q.shape
    b_in = _block_cumsum(g, rev)
    b_lo, b_hi = b_in[:SCAN_BLOCK], b_in[SCAN_BLOCK:]

    def rows(lo_part, hi_part):
        return jnp.concatenate([lo_part, hi_part], axis=0)

    half = SCAN_BLOCK // 2
    if rev:
        tot_lo, tot_hi = b_lo[0:1], b_hi[0:1]
        mid_lo, mid_hi = b_lo[half:half + 1], b_hi[half:half + 1]
    else:
        tot_lo, tot_hi = b_lo[SCAN_BLOCK - 1:], b_hi[SCAN_BLOCK - 1:]
        mid_lo, mid_hi = b_lo[half - 1:half], b_hi[half - 1:half]

    qh = q * jnp.exp2(b_in)
    ke = k * jnp.exp2(rows(tot_lo - b_lo, tot_hi - b_hi))
    if rev:
        qi = rows(qh[:SCAN_BLOCK] * jnp.exp2(tot_hi), qh[SCAN_BLOCK:])
        ks = rows(ke[:SCAN_BLOCK], ke[SCAN_BLOCK:] * jnp.exp2(tot_lo))
    else:
        qi = rows(qh[:SCAN_BLOCK], qh[SCAN_BLOCK:] * jnp.exp2(tot_lo))
        ks = rows(ke[:SCAN_BLOCK] * jnp.exp2(tot_hi), ke[SCAN_BLOCK:])
    qi = qi.astype(BF16)
    ks = ks.astype(BF16)
    dec = jnp.exp2(tot_lo + tot_hi)
    qh16 = qh.astype(BF16)
    ke16 = ke.astype(BF16)
    v16 = v.astype(BF16)
    qm16 = (q * jnp.exp2(rows(b_lo - mid_lo, b_hi - mid_hi))).astype(BF16)
    km16 = (k * jnp.exp2(rows(mid_lo - b_lo, mid_hi - b_hi))).astype(BF16)

    half_sums = jnp.minimum(jnp.minimum(mid_lo, tot_lo - mid_lo), jnp.minimum(mid_hi, tot_hi - mid_hi))
    fast_vec = jnp.min(half_sums, axis=-1, keepdims=True) >= SAFE_LOG2_DECAY
    fast = jnp.min(half_sums) >= SAFE_LOG2_DECAY

    return (qm16, km16, qh16, ke16, v16, qi, ks, dec, fast_vec), fast


def _scan_finish(operands, st_ref, rev):
    qm16, km16, qh16, ke16, v16, qi, ks, dec, fast_vec = operands
    c = qi.shape[0]
    n_pairs = qi.shape[1] // PAIR_W
    diag_mask, near_keys = _scan_masks(rev)
    diag_mask = diag_mask & fast_vec
    lo, hi = slice(0, SCAN_BLOCK), slice(SCAN_BLOCK, c)
    near, far = (hi, lo) if rev else (lo, hi)
    left = lax.broadcasted_iota(jnp.int32, (1, PAIR_W), 1) < HEAD_W

    def by_head(a):
        zero = jnp.zeros_like(a)
        return jnp.concatenate([jnp.where(left, a, zero), jnp.where(left, zero, a)], axis=0)

    outs = []
    for pr in range(n_pairs):
        sl = slice(pr * PAIR_W, (pr + 1) * PAIR_W)
        attd = _dot_nt(qm16[:, sl], by_head(km16[:, sl]))
        attx = _dot_nt(qh16[far, sl], by_head(ke16[:, sl]))
        att_near = jnp.where(diag_mask[near], attd[near], 0.0)
        att_far = jnp.where(diag_mask[far], attd[far], jnp.where(near_keys, attx, 0.0))
        att = jnp.concatenate([att_far, att_near] if rev else [att_near, att_far], axis=0)
        st = st_ref[pr]
        v_p = v16[:, sl]
        o = _dot(att.astype(BF16), by_head(v_p)) + _dot_nt(qi[:, sl], by_head(st.astype(BF16)))
        outs.append(o)
        v_stack = jnp.concatenate([v_p[:, :HEAD_W], v_p[:, HEAD_W:]], axis=0)
        st_ref[pr] = st * dec[:, sl] + _dot_tn(v_stack, by_head(ks[:, sl]))
    return jnp.concatenate(outs, axis=1)


def _scan_slow_fix(q_ref, v_ref, f_ref, rows, o_ref, od_ref, slow_refs, rev):
    q_s, k_s, v_s, b_s = slow_refs
    c, d = od_ref.shape
    f = f_ref[rows, :]
    q_s[...] = q_ref[rows, :].astype(F32)
    k_s[...] = 1.0 - f
    v_s[...] = v_ref[rows, :].astype(F32)
    b_s[...] = _block_cumsum(jnp.log2(f), rev)
    row = lax.broadcasted_iota(jnp.int32, (c, 1), 0)

    def body(grp, carry):
        t0 = pl.multiple_of(grp * SUBLANES, SUBLANES)
        q8 = q_s[pl.ds(t0, SUBLANES), :]
        b8 = b_s[pl.ds(t0, SUBLANES), :]
        out_rows = []
        for r in range(SUBLANES):
            ti = t0 + r
            prod = q8[r:r + 1] * k_s[...] * jnp.exp2(jnp.minimum(b8[r:r + 1] - b_s[...], 0.0))
            valid = ((row >= SCAN_BLOCK) == (ti >= SCAN_BLOCK)) & ((row >= ti) if rev else (row <= ti))
            heads = []
            for h in range(d // HEAD_W):
                hs = slice(h * HEAD_W, (h + 1) * HEAD_W)
                w = jnp.where(valid, jnp.sum(prod[:, hs], axis=-1, keepdims=True), 0.0)
                heads.append(jnp.sum(w * v_s[:, hs], axis=0, keepdims=True))
            out_rows.append(jnp.concatenate(heads, axis=1))
        od_ref[pl.ds(t0, SUBLANES), :] = jnp.concatenate(out_rows, axis=0)
        return carry

    lax.fori_loop(0, c // SUBLANES, body, 0)
    o_ref[rows, :] += od_ref[...]


def _state_init(st_ref, s0_ref):
    if s0_ref is None:
        st_ref[...] = jnp.zeros_like(st_ref)
    else:
        for h in range(HEADS):
            o = (h % 2) * HEAD_W
            st_ref[h // 2, :, o:o + HEAD_W] = s0_ref[h].T


def _state_write(st_ref, out_ref):
    for h in range(HEADS):
        o = (h % 2) * HEAD_W
        out_ref[h] = st_ref[h // 2, :, o:o + HEAD_W].T


def _lower_bound(lbl_ref, layer, direction):
    n_layers = lbl_ref.shape[0] // 2
    rows = [lbl_ref[2 * i + direction:2 * i + direction + 1, :] for i in range(n_layers)]
    m = functools.reduce(jnp.maximum, rows)
    ex = [jnp.exp(r - m) for r in rows]
    return sum(ex[:layer + 1]) / sum(ex)


def _chunk_rows(n_rows, rev):
    order = range(n_rows // SCAN_CHUNK)
    return [slice(ci * SCAN_CHUNK, (ci + 1) * SCAN_CHUNK) for ci in (reversed(order) if rev else order)]


def _scan_scratch(d):
    c = SCAN_CHUNK
    return [
        pltpu.VMEM((d // PAIR_W, HEAD_W, PAIR_W), F32),
        pltpu.VMEM((c, d), F32),
        pltpu.VMEM((c, d), F32), pltpu.VMEM((c, d), F32),
        pltpu.VMEM((c, d), F32), pltpu.VMEM((c, d), F32),
    ]


def _const_spec(shape):
    nd = len(shape)
    return pl.BlockSpec(shape, lambda *_: (0,) * nd, pipeline_mode=pl.Buffered(1))


def _hg_fwd_kernel(*refs, has_s0, layer):
    if has_s0:
        x_ref, mod_ref, g_ref, w_ref, lbl_ref, s0_ref = refs[:6]
        rest = refs[6:]
    else:
        x_ref, mod_ref, g_ref, w_ref, lbl_ref = refs[:5]
        s0_ref = None
        rest = refs[5:]
    of_ref, q_ref, v_ref, fb_ref, gate_ref, sf_ref = rest[:6]
    st_ref, od_ref = rest[6:8]
    slow_refs = rest[8:12]
    ff_ref = rest[12]
    j = pl.program_id(1)
    d = x_ref.shape[-1]

    @pl.when(j == 0)
    def _():
        _state_init(st_ref, s0_ref)

    sh, sc = mod_ref[0:1, :], mod_ref[1:2, :]
    h = (_rms(x_ref[...], g_ref[...]) * (1.0 + sc) + sh).astype(BF16)
    proj = lambda i: _dot(h, w_ref[:, i * d:(i + 1) * d])
    lb_f = _lower_bound(lbl_ref, layer, 0)
    lb_b = _lower_bound(lbl_ref, layer, 1)
    q = _silu(proj(0))
    f_f = _forget_gate(proj(1), lb_f)
    zi = proj(3)
    q_ref[...] = q.astype(BF16)
    v_ref[...] = zi.astype(BF16)
    ff_ref[...] = f_f
    k_f = 1.0 - f_f
    g_f = jnp.log2(f_f)

    def backward_gate():
        fb_ref[...] = _forget_gate(proj(2), lb_b)

    def output_gate():
        gate_ref[...] = _silu(proj(4)).astype(BF16)

    fillers = [backward_gate, output_gate]
    flags = []
    for rows in _chunk_rows(x_ref.shape[0], rev=False):
        operands, fast = _scan_prep(q[rows], k_f[rows], g_f[rows], zi[rows], rev=False)
        if fillers:
            fillers.pop(0)()
        of_ref[rows, :] = _scan_finish(operands, st_ref, rev=False)
        flags.append((rows, fast))
    for filler in fillers:
        filler()
    for rows, fast in flags:
        @pl.when(jnp.logical_not(fast))
        def _():
            _scan_slow_fix(q_ref, v_ref, ff_ref, rows, of_ref, od_ref, slow_refs, rev=False)

    @pl.when(j == pl.num_programs(1) - 1)
    def _():
        _state_write(st_ref, sf_ref)


def _hg_fwd_call(x, mod, mod_row0, mod_stride, g, w_in, lbl, state, layer_a):
    bn, l, d = x.shape
    c = min(HG_TOKEN_BLOCK, l)
    assert l % c == 0 and c % SCAN_CHUNK == 0
    n = l // c
    tok = lambda b, j: (b, j, 0)
    in_specs = [
        pl.BlockSpec((None, c, d), tok),
        pl.BlockSpec((None, N_MOD, d), lambda b, j: (mod_row0 + mod_stride * b, 0, 0)),
        _const_spec((1, d)),
        _const_spec(w_in.shape),
        _const_spec(lbl.shape),
    ]
    args = [x, mod, g, w_in, lbl]
    if state is not None:
        in_specs.append(pl.BlockSpec((None, None, None, HEADS, HEAD_W, HEAD_W),
                                     lambda b, j: (b, layer_a, 0, 0, 0, 0)))
        args.append(state)
    st_spec = pl.BlockSpec((None, HEADS, HEAD_W, HEAD_W), lambda b, j: (b, 0, 0, 0))
    return pl.pallas_call(
        functools.partial(_hg_fwd_kernel, has_s0=state is not None, layer=layer_a),
        grid=(bn, n),
        in_specs=in_specs,
        out_specs=[pl.BlockSpec((None, c, d), tok)] * 5 + [st_spec],
        out_shape=[
            jax.ShapeDtypeStruct((bn, l, d), F32),
            jax.ShapeDtypeStruct((bn, l, d), BF16),
            jax.ShapeDtypeStruct((bn, l, d), BF16),
            jax.ShapeDtypeStruct((bn, l, d), F32),
            jax.ShapeDtypeStruct((bn, l, d), BF16),
            jax.ShapeDtypeStruct((bn, HEADS, HEAD_W, HEAD_W), F32),
        ],
        scratch_shapes=_scan_scratch(d) + [pltpu.VMEM((c, d), F32)],
        compiler_params=pltpu.CompilerParams(
            dimension_semantics=("arbitrary", "arbitrary"), vmem_limit_bytes=VMEM_LIMIT),
        name="hg_fwd",
    )(*args)


def _mlp_tail(x1, mod_ref, g2, w1_ref, w2_ref):
    sh2, sc2, gt2 = mod_ref[3:4, :], mod_ref[4:5, :], mod_ref[5:6, :]
    h2 = (_rms(x1, g2) * (1.0 + sc2) + sh2).astype(BF16)
    a = jnp.maximum(_dot(h2, w1_ref[...]), 0.0)
    return x1 + gt2 * _dot((a * a).astype(BF16), w2_ref[...])


def _hg_bwd_kernel(*refs, has_s0, final, n_blocks):
    x_ref, mod_ref, gate_ref, of_ref, q_ref, v_ref, fb_ref, on_ref, g2_ref, fg_ref = refs[:10]
    wo_ref, w1_ref, w2_ref = refs[10:13]
    if has_s0:
        s0_ref = refs[13]
        rest = refs[14:]
    else:
        s0_ref = None
        rest = refs[13:]
    y_ref, sb_ref = rest[:2]
    st_ref, od_ref = rest[2:4]
    slow_refs = rest[4:8]
    o_ref = rest[8]
    s = pl.program_id(0)
    total = pl.num_programs(0) - 1
    live = s < total
    j = jnp.minimum(s, total - 1) % n_blocks
    d = x_ref.shape[-1]

    @pl.when(live & (j == 0))
    def _():
        _state_init(st_ref, s0_ref)

    chunk_rows = _chunk_rows(q_ref.shape[0], rev=True)
    n_slabs = 2 * len(chunk_rows)
    slab_w = w1_ref.shape[1] // n_slabs

    def step(finishing, scanning):
        if finishing:
            o = o_ref[...]
            parts = []
            for h in range(d // HEAD_W):
                oh = o[:, h * HEAD_W:(h + 1) * HEAD_W]
                parts.append(oh * lax.rsqrt(jnp.mean(oh * oh, axis=-1, keepdims=True) + EPS))
            o = jnp.concatenate(parts, axis=1) * on_ref[...] * gate_ref[...].astype(F32)
            gt1 = mod_ref[2:3, :]
            x1 = x_ref[...] + gt1 * _dot(o.astype(BF16), wo_ref[...])
            sh2, sc2, gt2 = mod_ref[3:4, :], mod_ref[4:5, :], mod_ref[5:6, :]
            h2 = (_rms(x1, g2_ref[...]) * (1.0 + sc2) + sh2).astype(BF16)

        def mlp_slab(i):
            a = jnp.maximum(_dot(h2, w1_ref[:, i * slab_w:(i + 1) * slab_w]), 0.0)
            return _dot((a * a).astype(BF16), w2_ref[i * slab_w:(i + 1) * slab_w, :])

        mlp = 0.0
        flags = []
        for ci, rows in enumerate(chunk_rows):
            if scanning:
                f_b = fb_ref[rows, :]
                operands, fast = _scan_prep(q_ref[rows, :].astype(F32), 1.0 - f_b, jnp.log2(f_b),
                                            v_ref[rows, :].astype(F32), rev=True)
                flags.append((rows, fast))
            if finishing:
                mlp = mlp + mlp_slab(2 * ci)
            if scanning:
                o_ref[rows, :] = of_ref[rows, :] + _scan_finish(operands, st_ref, rev=True)
            if finishing:
                mlp = mlp + mlp_slab(2 * ci + 1)
        if finishing:
            x2 = x1 + gt2 * mlp
            y_ref[...] = _rms(x2, fg_ref[...]) if final else x2
        for rows, fast in flags:
            @pl.when(jnp.logical_not(fast))
            def _():
                _scan_slow_fix(q_ref, v_ref, fb_ref, rows, o_ref, od_ref, slow_refs, rev=True)

    pl.when(s == 0)(lambda: step(finishing=False, scanning=True))
    pl.when((s > 0) & live)(lambda: step(finishing=True, scanning=True))
    pl.when(s == total)(lambda: step(finishing=True, scanning=False))

    @pl.when(live & (j == n_blocks - 1))
    def _():
        _state_write(st_ref, sb_ref)


def _hg_bwd_call(x, mod, mod_row0, mod_stride, stash, onorm, g2, fg, w_out, w1, w2, state, layer_a, final):
    bn, l, d = x.shape
    c = min(HG_TOKEN_BLOCK, l)
    assert l % c == 0 and c % SCAN_CHUNK == 0
    n = l // c
    total = bn * n

    def cur(s):
        t = jnp.minimum(s, total - 1)
        return t // n, n - 1 - t % n

    def prev(s):
        t = jnp.maximum(s - 1, 0)
        return t // n, n - 1 - t % n

    tok_cur = lambda s: (*cur(s), 0)
    tok_prev = lambda s: (*prev(s), 0)
    o_f, q, v, f_b, gate = stash
    in_specs = [
        pl.BlockSpec((None, c, d), tok_prev),
        pl.BlockSpec((None, N_MOD, d), lambda s: (mod_row0 + mod_stride * prev(s)[0], 0, 0)),
        pl.BlockSpec((None, c, d), tok_prev),
    ] + [pl.BlockSpec((None, c, d), tok_cur)] * 4 + [
        _const_spec((1, d)), _const_spec((1, d)), _const_spec((1, d)),
        _const_spec(w_out.shape), _const_spec(w1.shape), _const_spec(w2.shape),
    ]
    args = [x, mod, gate, o_f, q, v, f_b, onorm, g2, fg, w_out, w1, w2]
    if state is not None:
        in_specs.append(pl.BlockSpec((None, None, None, HEADS, HEAD_W, HEAD_W),
                                     lambda s: (cur(s)[0], layer_a, 1, 0, 0, 0)))
        args.append(state)
    st_spec = pl.BlockSpec((None, HEADS, HEAD_W, HEAD_W), lambda s: (cur(s)[0], 0, 0, 0))
    return pl.pallas_call(
        functools.partial(_hg_bwd_kernel, has_s0=state is not None, final=final, n_blocks=n),
        grid=(total + 1,),
        in_specs=in_specs,
        out_specs=[pl.BlockSpec((None, c, d), tok_prev), st_spec],
        out_shape=[
            jax.ShapeDtypeStruct((bn, l, d), F32),
            jax.ShapeDtypeStruct((bn, HEADS, HEAD_W, HEAD_W), F32),
        ],
        scratch_shapes=_scan_scratch(d) + [pltpu.VMEM((c, d), F32)],
        compiler_params=pltpu.CompilerParams(
            dimension_semantics=("arbitrary",), vmem_limit_bytes=VMEM_LIMIT),
        name="hg_bwd",
    )(*args)


def _cm_kernel(x_ref, mod_ref, g1_ref, g2_ref, fg_ref, win_ref, lng_ref, lnb_ref, ws_ref, bst_ref,
               wo_ref, w1_ref, w2_ref, y_ref, *, final):
    tb, d = x_ref.shape
    n_chunks = tb // MIX_CHUNK
    gw = d // CM_GROUPS
    x = x_ref[...]
    sh, sc, gt1 = mod_ref[0:1, :], mod_ref[1:2, :], mod_ref[2:3, :]
    h = (_rms(x, g1_ref[...]) * (1.0 + sc) + sh).astype(BF16)
    u = _gelu_tanh(_dot(h, win_ref[:, :d]))
    vv = _gelu_tanh(_dot(h, win_ref[:, d:]))
    mu = jnp.mean(vv, axis=-1, keepdims=True)
    vc = vv - mu
    var = jnp.mean(vc * vc, axis=-1, keepdims=True)
    vn = (vc * lax.rsqrt(var + EPS) * lng_ref[...] + lnb_ref[...]).astype(BF16)
    per_group = []
    for g in range(CM_GROUPS):
        rhs = jnp.concatenate(
            [vn[ch * MIX_CHUNK:(ch + 1) * MIX_CHUNK, g * gw:(g + 1) * gw] for ch in range(n_chunks)], axis=1)
        per_group.append(_dot(ws_ref[g], rhs) + bst_ref[:, g:g + 1])
    s = jnp.concatenate(
        [jnp.concatenate([pg[:, ch * gw:(ch + 1) * gw] for pg in per_group], axis=1) for ch in range(n_chunks)],
        axis=0)
    x1 = x + gt1 * _dot((u * s).astype(BF16), wo_ref[...])
    x2 = _mlp_tail(x1, mod_ref, g2_ref[...], w1_ref, w2_ref)
    y_ref[...] = _rms(x2, fg_ref[...]) if final else x2


def _cm_call(x, mod, mod_row0, mod_stride, g1, g2, fg, w_in, ln_g, ln_b, w_s, b_st, w_out, w1, w2, final):
    out_shape = x.shape
    if mod_stride == 0 and x.shape[1] % MIX_CHUNK == 0:
        x = x.reshape(1, -1, x.shape[-1])
    bn, l, d = x.shape
    tb = min(CM_TOKEN_BLOCK, l)
    assert l % tb == 0 and tb % MIX_CHUNK == 0
    tok = lambda b, j: (b, j, 0)
    consts = [g1, g2, fg, w_in, ln_g, ln_b, w_s, b_st, w_out, w1, w2]
    return pl.pallas_call(
        functools.partial(_cm_kernel, final=final),
        grid=(bn, l // tb),
        in_specs=[
            pl.BlockSpec((None, tb, d), tok),
            pl.BlockSpec((None, N_MOD, d), lambda b, j: (mod_row0 + mod_stride * b, 0, 0)),
        ] + [_const_spec(a.shape) for a in consts],
        out_specs=pl.BlockSpec((None, tb, d), tok),
        out_shape=jax.ShapeDtypeStruct((bn, l, d), F32),
        compiler_params=pltpu.CompilerParams(
            dimension_semantics=("arbitrary", "arbitrary"), vmem_limit_bytes=VMEM_LIMIT),
        name="cm",
    )(x, mod, *consts).reshape(out_shape)


def kernel(x_prompt, x_sample, state_hgrn, c, c_ctx, ada_w, ada_b, norm_mix_g, norm_mlp_g, mlp_w1, mlp_w2,
           hgrn_w_in, hgrn_lb_logits, hgrn_onorm_g, hgrn_w_out, cm_w_in, cm_ln_g, cm_ln_b, cm_w_s, cm_b_s,
           cm_w_out, final_norm_g):
    depth, d, _ = ada_w.shape
    n_lat = c.shape[0]
    assert d % PAIR_W == 0 and d // HEAD_W == HEADS and 1 + n_lat <= COND_ROWS
    assert x_prompt.shape[1] % SCAN_CHUNK == 0 and x_sample.shape[1] % SCAN_CHUNK == 0

    cond = jnp.concatenate([c_ctx[None, :], c, jnp.zeros((COND_ROWS - 1 - n_lat, d), F32)], axis=0)
    mods = _ada_call(cond, ada_w, ada_b).reshape(depth, COND_ROWS, N_MOD, d)

    row = lambda a: a.reshape(1, d)
    lbl = hgrn_lb_logits.reshape(-1, d)
    fg = row(final_norm_g)
    ctx, lat = x_prompt, x_sample
    new_states = []
    for i in range(depth):
        j = i // N_MIXERS
        final = i == depth - 1
        w1 = mlp_w1[i].astype(BF16)
        w2 = mlp_w2[i].astype(BF16)
        g1, g2 = row(norm_mix_g[i]), row(norm_mlp_g[i])
        if i % N_MIXERS == 0:
            w_in = hgrn_w_in[j].astype(BF16)
            w_out = hgrn_w_out[j].astype(BF16)
            onorm = row(hgrn_onorm_g[j])
            *stash, s_f = _hg_fwd_call(ctx, mods[i], 0, 0, g1, w_in, lbl, None, j)
            ctx, s_b = _hg_bwd_call(ctx, mods[i], 0, 0, stash, onorm, g2, fg, w_out, w1, w2, None, j, final)
            new_states.append(jnp.stack([s_f, s_b], axis=1))
            *stash, _ = _hg_fwd_call(lat, mods[i], 1, 1, g1, w_in, lbl, state_hgrn, j)
            lat, _ = _hg_bwd_call(lat, mods[i], 1, 1, stash, onorm, g2, fg, w_out, w1, w2, state_hgrn, j, final)
        else:
            cm_args = (g1, g2, fg, cm_w_in[j].astype(BF16), row(cm_ln_g[j]), row(cm_ln_b[j]),
                       cm_w_s[j].astype(BF16), cm_b_s[j].T, cm_w_out[j].astype(BF16), w1, w2, final)
            ctx = _cm_call(ctx, mods[i], 0, 0, *cm_args)
            lat = _cm_call(lat, mods[i], 1, 1, *cm_args)
    new_state = jnp.stack(new_states, axis=1).astype(x_prompt.dtype)
    return ctx, lat, new_state
```

```python
import functools

import jax
import jax.numpy as jnp
from jax import lax
from jax.experimental import pallas as pl
from jax.experimental.pallas import tpu as pltpu

F32 = jnp.float32
BF16 = jnp.bfloat16

EPS = 1e-6
N_MOD = 6
N_MIXERS = 2
HEADS = 8
HEAD_W = 128
PAIR_W = 2 * HEAD_W
MIX_CHUNK = 128
CM_GROUPS = 8
SCAN_BLOCK = 64
SCAN_CHUNK = 2 * SCAN_BLOCK
SUBLANES = 8
COND_ROWS = SUBLANES
SAFE_LOG2_DECAY = -115.0
HG_TOKEN_BLOCK = 512
CM_TOKEN_BLOCK = 512
VMEM_LIMIT = 56 * 1024 * 1024


def _silu(x):
    h = 0.5 * x
    return h * jnp.tanh(h) + h


def _forget_gate(z, lb):
    return 0.5 * (1.0 + lb) + (0.5 * (1.0 - lb)) * jnp.tanh(0.5 * z)


def _gelu_tanh(x):
    return 0.5 * x * (1.0 + jnp.tanh(0.7978845608028654 * (x + 0.044715 * (x * x * x))))


def _rms(x, g):
    return x * lax.rsqrt(jnp.mean(x * x, axis=-1, keepdims=True) + EPS) * g


def _dot(a, b):
    return jnp.dot(a, b, preferred_element_type=F32)


def _dot_nt(a, b):
    return lax.dot_general(a, b, (((1,), (1,)), ((), ())), preferred_element_type=F32)


def _dot_tn(a, b):
    return lax.dot_general(a, b, (((0,), (0,)), ((), ())), preferred_element_type=F32)


def _ada_kernel(cond_ref, w_ref, b_ref, out_ref):
    a = _silu(cond_ref[...]).astype(BF16)
    out_ref[...] = _dot(a, w_ref[...].astype(BF16)) + b_ref[...]


def _ada_call(cond, ada_w, ada_b):
    depth, d, _ = ada_w.shape
    return pl.pallas_call(
        _ada_kernel,
        grid=(depth, N_MOD),
        in_specs=[
            pl.BlockSpec((COND_ROWS, d), lambda i, n: (0, 0)),
            pl.BlockSpec((None, d, d), lambda i, n: (i, 0, n)),
            pl.BlockSpec((None, 1, d), lambda i, n: (i, 0, n)),
        ],
        out_specs=pl.BlockSpec((None, COND_ROWS, d), lambda i, n: (i, 0, n)),
        out_shape=jax.ShapeDtypeStruct((depth, COND_ROWS, N_MOD * d), F32),
        compiler_params=pltpu.CompilerParams(dimension_semantics=("arbitrary", "arbitrary")),
        name="ada",
    )(cond, ada_w, ada_b.reshape(depth, 1, N_MOD * d))


def _scan_masks(rev):
    c = SCAN_CHUNK
    t = lax.broadcasted_iota(jnp.int32, (c, 2 * c), 0)
    s = lax.broadcasted_iota(jnp.int32, (c, 2 * c), 1) & (c - 1)
    causal = (s >= t) if rev else (s <= t)
    diag = ((t >= SCAN_BLOCK) == (s >= SCAN_BLOCK)) & causal
    s_row = lax.broadcasted_iota(jnp.int32, (1, 2 * c), 1) & (c - 1)
    near_keys = (s_row >= SCAN_BLOCK) if rev else (s_row < SCAN_BLOCK)
    return diag, near_keys


def _block_cumsum(g, rev):
    c = g.shape[0]
    t = lax.broadcasted_iota(jnp.int32, (c, 2 * c), 0)
    s = lax.broadcasted_iota(jnp.int32, (c, 2 * c), 1) & (c - 1)
    same = (t >= SCAN_BLOCK) == (s >= SCAN_BLOCK)
    tri = jnp.where(same & ((s >= t) if rev else (s <= t)), 1.0, 0.0).astype(BF16)
    g_hi = g.astype(BF16)
    g_lo = (g - g_hi.astype(F32)).astype(BF16)
    return _dot(tri, jnp.concatenate([g_hi, g_lo], axis=0))


def _scan_prep(q, k, g, v, rev):
    c, d = q.shape
    b_in = _block_cumsum(g, rev)
    b_lo, b_hi = b_in[:SCAN_BLOCK], b_in[SCAN_BLOCK:]

    def rows(lo_part, hi_part):
        return jnp.concatenate([lo_part, hi_part], axis=0)

    half = SCAN_BLOCK // 2
    if rev:
        tot_lo, tot_hi = b_lo[0:1], b_hi[0:1]
        mid_lo, mid_hi = b_lo[half:half + 1], b_hi[half:half + 1]
    else:
        tot_lo, tot_hi = b_lo[SCAN_BLOCK - 1:], b_hi[SCAN_BLOCK - 1:]
        mid_lo, mid_hi = b_lo[half - 1:half], b_hi[half - 1:half]

    qh = q * jnp.exp2(b_in)
    ke = k * jnp.exp2(rows(tot_lo - b_lo, tot_hi - b_hi))
    if rev:
        qi = rows(qh[:SCAN_BLOCK] * jnp.exp2(tot_hi), qh[SCAN_BLOCK:])
        ks = rows(ke[:SCAN_BLOCK], ke[SCAN_BLOCK:] * jnp.exp2(tot_lo))
    else:
        qi = rows(qh[:SCAN_BLOCK], qh[SCAN_BLOCK:] * jnp.exp2(tot_lo))
        ks = rows(ke[:SCAN_BLOCK] * jnp.exp2(tot_hi), ke[SCAN_BLOCK:])
    qi = qi.astype(BF16)
    ks = ks.astype(BF16)
    dec = jnp.exp2(tot_lo + tot_hi)
    qh16 = qh.astype(BF16)
    ke16 = ke.astype(BF16)
    v16 = v.astype(BF16)
    qm16 = (q * jnp.exp2(rows(b_lo - mid_lo, b_hi - mid_hi))).astype(BF16)
    km16 = (k * jnp.exp2(rows(mid_lo - b_lo, mid_hi - b_hi))).astype(BF16)

    half_sums = jnp.minimum(jnp.minimum(mid_lo, tot_lo - mid_lo), jnp.minimum(mid_hi, tot_hi - mid_hi))
    fast_vec = jnp.min(half_sums, axis=-1, keepdims=True) >= SAFE_LOG2_DECAY
    fast = jnp.min(half_sums) >= SAFE_LOG2_DECAY

    return (qm16, km16, qh16, ke16, v16, qi, ks, dec, fast_vec), fast


def _scan_finish(operands, st_ref, rev):
    qm16, km16, qh16, ke16, v16, qi, ks, dec, fast_vec = operands
    c = qi.shape[0]
    n_pairs = qi.shape[1] // PAIR_W
    diag_mask, near_keys = _scan_masks(rev)
    diag_mask = diag_mask & fast_vec
    lo, hi = slice(0, SCAN_BLOCK), slice(SCAN_BLOCK, c)
    near, far = (hi, lo) if rev else (lo, hi)
    left = lax.broadcasted_iota(jnp.int32, (1, PAIR_W), 1) < HEAD_W

    def by_head(a):
        zero = jnp.zeros_like(a)
        return jnp.concatenate([jnp.where(left, a, zero), jnp.where(left, zero, a)], axis=0)

    outs = []
    for pr in range(n_pairs):
        sl = slice(pr * PAIR_W, (pr + 1) * PAIR_W)
        attd = _dot_nt(qm16[:, sl], by_head(km16[:, sl]))
        attx = _dot_nt(qh16[far, sl], by_head(ke16[:, sl]))
        att_near = jnp.where(diag_mask[near], attd[near], 0.0)
        att_far = jnp.where(diag_mask[far], attd[far], jnp.where(near_keys, attx, 0.0))
        att = jnp.concatenate([att_far, att_near] if rev else [att_near, att_far], axis=0)
        st = st_ref[pr]
        v_p = v16[:, sl]
        o = _dot(att.astype(BF16), by_head(v_p)) + _dot_nt(qi[:, sl], by_head(st.astype(BF16)))
        outs.append(o)
        v_stack = jnp.concatenate([v_p[:, :HEAD_W], v_p[:, HEAD_W:]], axis=0)
        st_ref[pr] = st * dec[:, sl] + _dot_tn(v_stack, by_head(ks[:, sl]))
    return jnp.concatenate(outs, axis=1)


def _scan_slow_fix(q_ref, v_ref, f_ref, rows, o_ref, od_ref, slow_refs, rev):
    q_s, k_s, v_s, b_s = slow_refs
    c, d = od_ref.shape
    f = f_ref[rows, :]
    q_s[...] = q_ref[rows, :].astype(F32)
    k_s[...] = 1.0 - f
    v_s[...] = v_ref[rows, :].astype(F32)
    b_s[...] = _block_cumsum(jnp.log2(f), rev)
    row = lax.broadcasted_iota(jnp.int32, (c, 1), 0)

    def body(grp, carry):
        t0 = pl.multiple_of(grp * SUBLANES, SUBLANES)
        q8 = q_s[pl.ds(t0, SUBLANES), :]
        b8 = b_s[pl.ds(t0, SUBLANES), :]
        out_rows = []
        for r in range(SUBLANES):
            ti = t0 + r
            prod = q8[r:r + 1] * k_s[...] * jnp.exp2(jnp.minimum(b8[r:r + 1] - b_s[...], 0.0))
            valid = ((row >= SCAN_BLOCK) == (ti >= SCAN_BLOCK)) & ((row >= ti) if rev else (row <= ti))
            heads = []
            for h in range(d // HEAD_W):
                hs = slice(h * HEAD_W, (h + 1) * HEAD_W)
                w = jnp.where(valid, jnp.sum(prod[:, hs], axis=-1, keepdims=True), 0.0)
                heads.append(jnp.sum(w * v_s[:, hs], axis=0, keepdims=True))
            out_rows.append(jnp.concatenate(heads, axis=1))
        od_ref[pl.ds(t0, SUBLANES), :] = jnp.concatenate(out_rows, axis=0)
        return carry

    lax.fori_loop(0, c // SUBLANES, body, 0)
    o_ref[rows, :] += od_ref[...]


def _scan_slow_fixes(fast_ref, q_ref, v_ref, f_ref, o_ref, od_ref, slow_refs, rev):
    def body(ci, carry):
        @pl.when(fast_ref[ci] == 0)
        def _():
            rows = pl.ds(pl.multiple_of(ci * SCAN_CHUNK, SCAN_CHUNK), SCAN_CHUNK)
            _scan_slow_fix(q_ref, v_ref, f_ref, rows, o_ref, od_ref, slow_refs, rev)
        return carry

    lax.fori_loop(0, fast_ref.shape[0], body, 0)


def _state_init(st_ref, s0_ref):
    if s0_ref is None:
        st_ref[...] = jnp.zeros_like(st_ref)
    else:
        for h in range(HEADS):
            o = (h % 2) * HEAD_W
            st_ref[h // 2, :, o:o + HEAD_W] = s0_ref[h].T


def _state_write(st_ref, out_ref):
    for h in range(HEADS):
        o = (h % 2) * HEAD_W
        out_ref[h] = st_ref[h // 2, :, o:o + HEAD_W].T


def _lower_bound(lbl_ref, layer, direction):
    n_layers = lbl_ref.shape[0] // 2
    rows = [lbl_ref[2 * i + direction:2 * i + direction + 1, :] for i in range(n_layers)]
    m = functools.reduce(jnp.maximum, rows)
    ex = [jnp.exp(r - m) for r in rows]
    return sum(ex[:layer + 1]) / sum(ex)


def _chunk_rows(n_rows, rev):
    order = range(n_rows // SCAN_CHUNK)
    return [slice(ci * SCAN_CHUNK, (ci + 1) * SCAN_CHUNK) for ci in (reversed(order) if rev else order)]


def _scan_scratch(d):
    c = SCAN_CHUNK
    return [
        pltpu.VMEM((d // PAIR_W, HEAD_W, PAIR_W), F32),
        pltpu.VMEM((c, d), F32),
        pltpu.VMEM((c, d), F32), pltpu.VMEM((c, d), F32),
        pltpu.VMEM((c, d), F32), pltpu.VMEM((c, d), F32),
    ]


def _const_spec(shape):
    nd = len(shape)
    return pl.BlockSpec(shape, lambda *_: (0,) * nd, pipeline_mode=pl.Buffered(1))


def _hg_fwd_kernel(*refs, has_s0, layer):
    if has_s0:
        x_ref, mod_ref, g_ref, w_ref, lbl_ref, s0_ref = refs[:6]
        rest = refs[6:]
    else:
        x_ref, mod_ref, g_ref, w_ref, lbl_ref = refs[:5]
        s0_ref = None
        rest = refs[5:]
    of_ref, q_ref, v_ref, fb_ref, gate_ref, sf_ref = rest[:6]
    st_ref, od_ref = rest[6:8]
    slow_refs = rest[8:12]
    ff_ref, fast_ref = rest[12:14]
    j = pl.program_id(1)
    d = x_ref.shape[-1]

    @pl.when(j == 0)
    def _():
        _state_init(st_ref, s0_ref)

    sh, sc = mod_ref[0:1, :], mod_ref[1:2, :]
    h = (_rms(x_ref[...], g_ref[...]) * (1.0 + sc) + sh).astype(BF16)
    proj = lambda i: _dot(h, w_ref[:, i * d:(i + 1) * d])
    lb_f = _lower_bound(lbl_ref, layer, 0)
    lb_b = _lower_bound(lbl_ref, layer, 1)
    q = _silu(proj(0))
    f_f = _forget_gate(proj(1), lb_f)
    zi = proj(3)
    q_ref[...] = q.astype(BF16)
    v_ref[...] = zi.astype(BF16)
    ff_ref[...] = f_f
    k_f = 1.0 - f_f
    g_f = jnp.log2(f_f)

    def backward_gate():
        fb_ref[...] = _forget_gate(proj(2), lb_b)

    def output_gate():
        gate_ref[...] = _silu(proj(4)).astype(BF16)

    fillers = [backward_gate, output_gate]
    for rows in _chunk_rows(x_ref.shape[0], rev=False):
        operands, fast = _scan_prep(q[rows], k_f[rows], g_f[rows], zi[rows], rev=False)
        fast_ref[rows.start // SCAN_CHUNK] = fast.astype(jnp.int32)
        if fillers:
            fillers.pop(0)()
        of_ref[rows, :] = _scan_finish(operands, st_ref, rev=False)
    for filler in fillers:
        filler()
    _scan_slow_fixes(fast_ref, q_ref, v_ref, ff_ref, of_ref, od_ref, slow_refs, rev=False)

    @pl.when(j == pl.num_programs(1) - 1)
    def _():
        _state_write(st_ref, sf_ref)


def _hg_fwd_call(x, mod, mod_row0, mod_stride, g, w_in, lbl, state, layer_a):
    bn, l, d = x.shape
    c = min(HG_TOKEN_BLOCK, l)
    assert l % c == 0 and c % SCAN_CHUNK == 0
    n = l // c
    tok = lambda b, j: (b, j, 0)
    in_specs = [
        pl.BlockSpec((None, c, d), tok),
        pl.BlockSpec((None, N_MOD, d), lambda b, j: (mod_row0 + mod_stride * b, 0, 0)),
        _const_spec((1, d)),
        _const_spec(w_in.shape),
        _const_spec(lbl.shape),
    ]
    args = [x, mod, g, w_in, lbl]
    if state is not None:
        in_specs.append(pl.BlockSpec((None, None, None, HEADS, HEAD_W, HEAD_W),
                                     lambda b, j: (b, layer_a, 0, 0, 0, 0)))
        args.append(state)
    st_spec = pl.BlockSpec((None, HEADS, HEAD_W, HEAD_W), lambda b, j: (b, 0, 0, 0))
    return pl.pallas_call(
        functools.partial(_hg_fwd_kernel, has_s0=state is not None, layer=layer_a),
        grid=(bn, n),
        in_specs=in_specs,
        out_specs=[pl.BlockSpec((None, c, d), tok)] * 5 + [st_spec],
        out_shape=[
            jax.ShapeDtypeStruct((bn, l, d), F32),
            jax.ShapeDtypeStruct((bn, l, d), BF16),
            jax.ShapeDtypeStruct((bn, l, d), BF16),
            jax.ShapeDtypeStruct((bn, l, d), F32),
            jax.ShapeDtypeStruct((bn, l, d), BF16),
            jax.ShapeDtypeStruct((bn, HEADS, HEAD_W, HEAD_W), F32),
        ],
        scratch_shapes=_scan_scratch(d) + [
            pltpu.VMEM((c, d), F32),
            pltpu.SMEM((c // SCAN_CHUNK,), jnp.int32),
        ],
        compiler_params=pltpu.CompilerParams(
            dimension_semantics=("arbitrary", "arbitrary"), vmem_limit_bytes=VMEM_LIMIT),
        name="hg_fwd",
    )(*args)


def _mlp_tail(x1, mod_ref, g2, w1_ref, w2_ref):
    sh2, sc2, gt2 = mod_ref[3:4, :], mod_ref[4:5, :], mod_ref[5:6, :]
    h2 = (_rms(x1, g2) * (1.0 + sc2) + sh2).astype(BF16)
    a = jnp.maximum(_dot(h2, w1_ref[...]), 0.0)
    return x1 + gt2 * _dot((a * a).astype(BF16), w2_ref[...])


def _hg_bwd_kernel(*refs, has_s0, final, n_blocks):
    x_ref, mod_ref, gate_ref, of_ref, q_ref, v_ref, fb_ref, on_ref, g2_ref, fg_ref = refs[:10]
    wo_ref, w1_ref, w2_ref = refs[10:13]
    if has_s0:
        s0_ref = refs[13]
        rest = refs[14:]
    else:
        s0_ref = None
        rest = refs[13:]
    y_ref, sb_ref = rest[:2]
    st_ref, od_ref = rest[2:4]
    slow_refs = rest[4:8]
    o_ref, fast_ref = rest[8:10]
    s = pl.program_id(0)
    total = pl.num_programs(0) - 1
    live = s < total
    j = jnp.minimum(s, total - 1) % n_blocks
    d = x_ref.shape[-1]

    @pl.when(s == 0)
    def _():
        o_ref[...] = jnp.zeros_like(o_ref)

    @pl.when(live & (j == 0))
    def _():
        _state_init(st_ref, s0_ref)

    chunk_rows = _chunk_rows(q_ref.shape[0], rev=True)
    n_slabs = 2 * len(chunk_rows)
    slab_w = w1_ref.shape[1] // n_slabs

    def step(finishing, scanning):
        if finishing:
            o = o_ref[...]
            parts = []
            for h in range(d // HEAD_W):
                oh = o[:, h * HEAD_W:(h + 1) * HEAD_W]
                parts.append(oh * lax.rsqrt(jnp.mean(oh * oh, axis=-1, keepdims=True) + EPS))
            o = jnp.concatenate(parts, axis=1) * on_ref[...] * gate_ref[...].astype(F32)
            gt1 = mod_ref[2:3, :]
            x1 = x_ref[...] + gt1 * _dot(o.astype(BF16), wo_ref[...])
            sh2, sc2, gt2 = mod_ref[3:4, :], mod_ref[4:5, :], mod_ref[5:6, :]
            h2 = (_rms(x1, g2_ref[...]) * (1.0 + sc2) + sh2).astype(BF16)

        def mlp_slab(i):
            a = jnp.maximum(_dot(h2, w1_ref[:, i * slab_w:(i + 1) * slab_w]), 0.0)
            return _dot((a * a).astype(BF16), w2_ref[i * slab_w:(i + 1) * slab_w, :])

        mlp = 0.0
        for ci, rows in enumerate(chunk_rows):
            if scanning:
                f_b = fb_ref[rows, :]
                operands, fast = _scan_prep(q_ref[rows, :].astype(F32), 1.0 - f_b, jnp.log2(f_b),
                                            v_ref[rows, :].astype(F32), rev=True)
                fast_ref[rows.start // SCAN_CHUNK] = fast.astype(jnp.int32)
            if finishing:
                mlp = mlp + mlp_slab(2 * ci)
            if scanning:
                o_ref[rows, :] = of_ref[rows, :] + _scan_finish(operands, st_ref, rev=True)
            if finishing:
                mlp = mlp + mlp_slab(2 * ci + 1)
        if finishing:
            x2 = x1 + gt2 * mlp
            y_ref[...] = _rms(x2, fg_ref[...]) if final else x2

    step(finishing=True, scanning=True)
    _scan_slow_fixes(fast_ref, q_ref, v_ref, fb_ref, o_ref, od_ref, slow_refs, rev=True)

    @pl.when(live & (j == n_blocks - 1))
    def _():
        _state_write(st_ref, sb_ref)


def _hg_bwd_call(x, mod, mod_row0, mod_stride, stash, onorm, g2, fg, w_out, w1, w2, state, layer_a, final):
    bn, l, d = x.shape
    c = min(HG_TOKEN_BLOCK, l)
    assert l % c == 0 and c % SCAN_CHUNK == 0
    n = l // c
    total = bn * n

    def cur(s):
        t = jnp.minimum(s, total - 1)
        return t // n, n - 1 - t % n

    def prev(s):
        t = jnp.maximum(s - 1, 0)
        return t // n, n - 1 - t % n

    tok_cur = lambda s: (*cur(s), 0)
    tok_prev = lambda s: (*prev(s), 0)
    o_f, q, v, f_b, gate = stash
    in_specs = [
        pl.BlockSpec((None, c, d), tok_prev),
        pl.BlockSpec((None, N_MOD, d), lambda s: (mod_row0 + mod_stride * prev(s)[0], 0, 0)),
        pl.BlockSpec((None, c, d), tok_prev),
    ] + [pl.BlockSpec((None, c, d), tok_cur)] * 4 + [
        _const_spec((1, d)), _const_spec((1, d)), _const_spec((1, d)),
        _const_spec(w_out.shape), _const_spec(w1.shape), _const_spec(w2.shape),
    ]
    args = [x, mod, gate, o_f, q, v, f_b, onorm, g2, fg, w_out, w1, w2]
    if state is not None:
        in_specs.append(pl.BlockSpec((None, None, None, HEADS, HEAD_W, HEAD_W),
                                     lambda s: (cur(s)[0], layer_a, 1, 0, 0, 0)))
        args.append(state)
    st_spec = pl.BlockSpec((None, HEADS, HEAD_W, HEAD_W), lambda s: (cur(s)[0], 0, 0, 0))
    return pl.pallas_call(
        functools.partial(_hg_bwd_kernel, has_s0=state is not None, final=final, n_blocks=n),
        grid=(total + 1,),
        in_specs=in_specs,
        out_specs=[pl.BlockSpec((None, c, d), tok_prev), st_spec],
        out_shape=[
            jax.ShapeDtypeStruct((bn, l, d), F32),
            jax.ShapeDtypeStruct((bn, HEADS, HEAD_W, HEAD_W), F32),
        ],
        scratch_shapes=_scan_scratch(d) + [
            pltpu.VMEM((c, d), F32),
            pltpu.SMEM((c // SCAN_CHUNK,), jnp.int32),
        ],
        compiler_params=pltpu.CompilerParams(
            dimension_semantics=("arbitrary",), vmem_limit_bytes=VMEM_LIMIT),
        name="hg_bwd",
    )(*args)


def _cm_kernel(x_ref, mod_ref, g1_ref, g2_ref, fg_ref, win_ref, lng_ref, lnb_ref, ws_ref, bst_ref,
               wo_ref, w1_ref, w2_ref, y_ref, *, final):
    tb, d = x_ref.shape
    n_chunks = tb // MIX_CHUNK
    gw = d // CM_GROUPS
    x = x_ref[...]
    sh, sc, gt1 = mod_ref[0:1, :], mod_ref[1:2, :], mod_ref[2:3, :]
    h = (_rms(x, g1_ref[...]) * (1.0 + sc) + sh).astype(BF16)
    u = _gelu_tanh(_dot(h, win_ref[:, :d]))
    vv = _gelu_tanh(_dot(h, win_ref[:, d:]))
    mu = jnp.mean(vv, axis=-1, keepdims=True)
    vc = vv - mu
    var = jnp.mean(vc * vc, axis=-1, keepdims=True)
    vn = (vc * lax.rsqrt(var + EPS) * lng_ref[...] + lnb_ref[...]).astype(BF16)
    per_group = []
    for g in range(CM_GROUPS):
        rhs = jnp.concatenate(
            [vn[ch * MIX_CHUNK:(ch + 1) * MIX_CHUNK, g * gw:(g + 1) * gw] for ch in range(n_chunks)], axis=1)
        per_group.append(_dot(ws_ref[g], rhs) + bst_ref[:, g:g + 1])
    s = jnp.concatenate(
        [jnp.concatenate([pg[:, ch * gw:(ch + 1) * gw] for pg in per_group], axis=1) for ch in range(n_chunks)],
        axis=0)
    x1 = x + gt1 * _dot((u * s).astype(BF16), wo_ref[...])
    x2 = _mlp_tail(x1, mod_ref, g2_ref[...], w1_ref, w2_ref)
    y_ref[...] = _rms(x2, fg_ref[...]) if final else x2


def _cm_call(x, mod, mod_row0, mod_stride, g1, g2, fg, w_in, ln_g, ln_b, w_s, b_st, w_out, w1, w2, final):
    out_shape = x.shape
    if mod_stride == 0 and x.shape[1] % MIX_CHUNK == 0:
        x = x.reshape(1, -1, x.shape[-1])
    bn, l, d = x.shape
    tb = min(CM_TOKEN_BLOCK, l)
    assert l % tb == 0 and tb % MIX_CHUNK == 0
    tok = lambda b, j: (b, j, 0)
    consts = [g1, g2, fg, w_in, ln_g, ln_b, w_s, b_st, w_out, w1, w2]
    return pl.pallas_call(
        functools.partial(_cm_kernel, final=final),
        grid=(bn, l // tb),
        in_specs=[
            pl.BlockSpec((None, tb, d), tok),
            pl.BlockSpec((None, N_MOD, d), lambda b, j: (mod_row0 + mod_stride * b, 0, 0)),
        ] + [_const_spec(a.shape) for a in consts],
        out_specs=pl.BlockSpec((None, tb, d), tok),
        out_shape=jax.ShapeDtypeStruct((bn, l, d), F32),
        compiler_params=pltpu.CompilerParams(
            dimension_semantics=("arbitrary", "arbitrary"), vmem_limit_bytes=VMEM_LIMIT),
        name="cm",
    )(x, mod, *consts).reshape(out_shape)


def kernel(x_prompt, x_sample, state_hgrn, c, c_ctx, ada_w, ada_b, norm_mix_g, norm_mlp_g, mlp_w1, mlp_w2,
           hgrn_w_in, hgrn_lb_logits, hgrn_onorm_g, hgrn_w_out, cm_w_in, cm_ln_g, cm_ln_b, cm_w_s, cm_b_s,
           cm_w_out, final_norm_g):
    depth, d, _ = ada_w.shape
    n_lat = c.shape[0]
    assert d % PAIR_W == 0 and d // HEAD_W == HEADS and 1 + n_lat <= COND_ROWS
    assert x_prompt.shape[1] % SCAN_CHUNK == 0 and x_sample.shape[1] % SCAN_CHUNK == 0

    cond = jnp.concatenate([c_ctx[None, :], c, jnp.zeros((COND_ROWS - 1 - n_lat, d), F32)], axis=0)
    mods = _ada_call(cond, ada_w, ada_b).reshape(depth, COND_ROWS, N_MOD, d)

    row = lambda a: a.reshape(1, d)
    lbl = hgrn_lb_logits.reshape(-1, d)
    fg = row(final_norm_g)
    ctx, lat = x_prompt, x_sample
    new_states = []
    for i in range(depth):
        j = i // N_MIXERS
        final = i == depth - 1
        w1 = mlp_w1[i].astype(BF16)
        w2 = mlp_w2[i].astype(BF16)
        g1, g2 = row(norm_mix_g[i]), row(norm_mlp_g[i])
        if i % N_MIXERS == 0:
            w_in = hgrn_w_in[j].astype(BF16)
            w_out = hgrn_w_out[j].astype(BF16)
            onorm = row(hgrn_onorm_g[j])
            *stash, s_f = _hg_fwd_call(ctx, mods[i], 0, 0, g1, w_in, lbl, None, j)
            ctx, s_b = _hg_bwd_call(ctx, mods[i], 0, 0, stash, onorm, g2, fg, w_out, w1, w2, None, j, final)
            new_states.append(jnp.stack([s_f, s_b], axis=1))
            *stash, _ = _hg_fwd_call(lat, mods[i], 1, 1, g1, w_in, lbl, state_hgrn, j)
            lat, _ = _hg_bwd_call(lat, mods[i], 1, 1, stash, onorm, g2, fg, w_out, w1, w2, state_hgrn, j, final)
        else:
            cm_args = (g1, g2, fg, cm_w_in[j].astype(BF16), row(cm_ln_g[j]), row(cm_ln_b[j]),
                       cm_w_s[j].astype(BF16), cm_b_s[j].T, cm_w_out[j].astype(BF16), w1, w2, final)
            ctx = _cm_call(ctx, mods[i], 0, 0, *cm_args)
            lat = _cm_call(lat, mods[i], 1, 1, *cm_args)
    new_state = jnp.stack(new_states, axis=1).astype(x_prompt.dtype)
    return ctx, lat, new_state
```

```python
import functools

import jax
import jax.numpy as jnp
from jax import lax
from jax.experimental import pallas as pl
from jax.experimental.pallas import tpu as pltpu

F32 = jnp.float32
BF16 = jnp.bfloat16

EPS = 1e-6
N_MOD = 6
N_MIXERS = 2
HEADS = 8
HEAD_W = 128
PAIR_W = 2 * HEAD_W
MIX_CHUNK = 128
CM_GROUPS = 8
SCAN_BLOCK = 64
SCAN_CHUNK = 2 * SCAN_BLOCK
SUBLANES = 8
COND_ROWS = SUBLANES
SAFE_LOG2_DECAY = -115.0
HG_TOKEN_BLOCK = 512
CM_TOKEN_BLOCK = 512
VMEM_LIMIT = 56 * 1024 * 1024


def _silu(x):
    h = 0.5 * x
    return h * jnp.tanh(h) + h


def _forget_gate(z, lb):
    return 0.5 * (1.0 + lb) + (0.5 * (1.0 - lb)) * jnp.tanh(0.5 * z)


def _gelu_tanh(x):
    h = 0.5 * x
    t = jnp.tanh(x * (0.7978845608028654 + (0.7978845608028654 * 0.044715) * (x * x)))
    return h * t + h


def _rms(x, g):
    return x * lax.rsqrt(jnp.mean(x * x, axis=-1, keepdims=True) + EPS) * g


def _dot(a, b):
    return jnp.dot(a, b, preferred_element_type=F32)


def _dot_nt(a, b):
    return lax.dot_general(a, b, (((1,), (1,)), ((), ())), preferred_element_type=F32)


def _dot_tn(a, b):
    return lax.dot_general(a, b, (((0,), (0,)), ((), ())), preferred_element_type=F32)


def _ada_kernel(cond_ref, w_ref, b_ref, out_ref):
    a = _silu(cond_ref[...]).astype(BF16)
    out_ref[...] = _dot(a, w_ref[...].astype(BF16)) + b_ref[...]


def _ada_call(cond, ada_w, ada_b):
    depth, d, _ = ada_w.shape
    return pl.pallas_call(
        _ada_kernel,
        grid=(depth, N_MOD),
        in_specs=[
            pl.BlockSpec((COND_ROWS, d), lambda i, n: (0, 0)),
            pl.BlockSpec((None, d, d), lambda i, n: (i, 0, n)),
            pl.BlockSpec((None, 1, d), lambda i, n: (i, 0, n)),
        ],
        out_specs=pl.BlockSpec((None, COND_ROWS, d), lambda i, n: (i, 0, n)),
        out_shape=jax.ShapeDtypeStruct((depth, COND_ROWS, N_MOD * d), F32),
        compiler_params=pltpu.CompilerParams(dimension_semantics=("arbitrary", "arbitrary")),
        name="ada",
    )(cond, ada_w, ada_b.reshape(depth, 1, N_MOD * d))


def _scan_masks(rev):
    c = SCAN_CHUNK
    t = lax.broadcasted_iota(jnp.int32, (c, 2 * c), 0)
    s = lax.broadcasted_iota(jnp.int32, (c, 2 * c), 1) & (c - 1)
    causal = (s >= t) if rev else (s <= t)
    diag = ((t >= SCAN_BLOCK) == (s >= SCAN_BLOCK)) & causal
    s_row = lax.broadcasted_iota(jnp.int32, (1, 2 * c), 1) & (c - 1)
    near_keys = (s_row >= SCAN_BLOCK) if rev else (s_row < SCAN_BLOCK)
    return diag, near_keys


def _block_cumsum(g, rev):
    c = g.shape[0]
    t = lax.broadcasted_iota(jnp.int32, (c, 2 * c), 0)
    s = lax.broadcasted_iota(jnp.int32, (c, 2 * c), 1) & (c - 1)
    same = (t >= SCAN_BLOCK) == (s >= SCAN_BLOCK)
    tri = jnp.where(same & ((s >= t) if rev else (s <= t)), 1.0, 0.0).astype(BF16)
    g_hi = g.astype(BF16)
    g_lo = (g - g_hi.astype(F32)).astype(BF16)
    return _dot(tri, jnp.concatenate([g_hi, g_lo], axis=0))


def _scan_prep(q, k, g, v, rev):
    c, d = q.shape
    b_in = _block_cumsum(g, rev)
    b_lo, b_hi = b_in[:SCAN_BLOCK], b_in[SCAN_BLOCK:]

    def rows(lo_part, hi_part):
        return jnp.concatenate([lo_part, hi_part], axis=0)

    half = SCAN_BLOCK // 2
    if rev:
        tot_lo, tot_hi = b_lo[0:1], b_hi[0:1]
        mid_lo, mid_hi = b_lo[half:half + 1], b_hi[half:half + 1]
    else:
        tot_lo, tot_hi = b_lo[SCAN_BLOCK - 1:], b_hi[SCAN_BLOCK - 1:]
        mid_lo, mid_hi = b_lo[half - 1:half], b_hi[half - 1:half]

    qh = q * jnp.exp2(b_in)
    ke = k * jnp.exp2(rows(tot_lo - b_lo, tot_hi - b_hi))
    if rev:
        qi = rows(qh[:SCAN_BLOCK] * jnp.exp2(tot_hi), qh[SCAN_BLOCK:])
        ks = rows(ke[:SCAN_BLOCK], ke[SCAN_BLOCK:] * jnp.exp2(tot_lo))
    else:
        qi = rows(qh[:SCAN_BLOCK], qh[SCAN_BLOCK:] * jnp.exp2(tot_lo))
        ks = rows(ke[:SCAN_BLOCK] * jnp.exp2(tot_hi), ke[SCAN_BLOCK:])
    qi = qi.astype(BF16)
    ks = ks.astype(BF16)
    dec = jnp.exp2(tot_lo + tot_hi)
    qh16 = qh.astype(BF16)
    ke16 = ke.astype(BF16)
    v16 = v.astype(BF16)
    qm16 = (q * jnp.exp2(rows(b_lo - mid_lo, b_hi - mid_hi))).astype(BF16)
    km16 = (k * jnp.exp2(rows(mid_lo - b_lo, mid_hi - b_hi))).astype(BF16)

    half_sums = jnp.minimum(jnp.minimum(mid_lo, tot_lo - mid_lo), jnp.minimum(mid_hi, tot_hi - mid_hi))
    fast_vec = jnp.min(half_sums, axis=-1, keepdims=True) >= SAFE_LOG2_DECAY
    fast = jnp.min(half_sums) >= SAFE_LOG2_DECAY

    return (qm16, km16, qh16, ke16, v16, qi, ks, dec, fast_vec), fast


def _scan_finish(operands, st_ref, rev):
    qm16, km16, qh16, ke16, v16, qi, ks, dec, fast_vec = operands
    c = qi.shape[0]
    n_pairs = qi.shape[1] // PAIR_W
    diag_mask, near_keys = _scan_masks(rev)
    diag_mask = diag_mask & fast_vec
    lo, hi = slice(0, SCAN_BLOCK), slice(SCAN_BLOCK, c)
    near, far = (hi, lo) if rev else (lo, hi)
    left = lax.broadcasted_iota(jnp.int32, (1, PAIR_W), 1) < HEAD_W

    def by_head(a):
        zero = jnp.zeros_like(a)
        return jnp.concatenate([jnp.where(left, a, zero), jnp.where(left, zero, a)], axis=0)

    outs = []
    for pr in range(n_pairs):
        sl = slice(pr * PAIR_W, (pr + 1) * PAIR_W)
        attd = _dot_nt(qm16[:, sl], by_head(km16[:, sl]))
        attx = _dot_nt(qh16[far, sl], by_head(ke16[:, sl]))
        att_near = jnp.where(diag_mask[near], attd[near], 0.0)
        att_far = jnp.where(diag_mask[far], attd[far], jnp.where(near_keys, attx, 0.0))
        att = jnp.concatenate([att_far, att_near] if rev else [att_near, att_far], axis=0)
        st = st_ref[pr]
        v_p = v16[:, sl]
        o = _dot(att.astype(BF16), by_head(v_p)) + _dot_nt(qi[:, sl], by_head(st.astype(BF16)))
        outs.append(o)
        v_stack = jnp.concatenate([v_p[:, :HEAD_W], v_p[:, HEAD_W:]], axis=0)
        st_ref[pr] = st * dec[:, sl] + _dot_tn(v_stack, by_head(ks[:, sl]))
    return jnp.concatenate(outs, axis=1)


def _scan_slow_fix(q_ref, v_ref, f_ref, rows, o_ref, od_ref, slow_refs, rev):
    q_s, k_s, v_s, b_s = slow_refs
    c, d = od_ref.shape
    f = f_ref[rows, :]
    q_s[...] = q_ref[rows, :].astype(F32)
    k_s[...] = 1.0 - f
    v_s[...] = v_ref[rows, :].astype(F32)
    b_s[...] = _block_cumsum(jnp.log2(f), rev)
    row = lax.broadcasted_iota(jnp.int32, (c, 1), 0)

    def body(grp, carry):
        t0 = pl.multiple_of(grp * SUBLANES, SUBLANES)
        q8 = q_s[pl.ds(t0, SUBLANES), :]
        b8 = b_s[pl.ds(t0, SUBLANES), :]
        out_rows = []
        for r in range(SUBLANES):
            ti = t0 + r
            prod = q8[r:r + 1] * k_s[...] * jnp.exp2(jnp.minimum(b8[r:r + 1] - b_s[...], 0.0))
            valid = ((row >= SCAN_BLOCK) == (ti >= SCAN_BLOCK)) & ((row >= ti) if rev else (row <= ti))
            heads = []
            for h in range(d // HEAD_W):
                hs = slice(h * HEAD_W, (h + 1) * HEAD_W)
                w = jnp.where(valid, jnp.sum(prod[:, hs], axis=-1, keepdims=True), 0.0)
                heads.append(jnp.sum(w * v_s[:, hs], axis=0, keepdims=True))
            out_rows.append(jnp.concatenate(heads, axis=1))
        od_ref[pl.ds(t0, SUBLANES), :] = jnp.concatenate(out_rows, axis=0)
        return carry

    lax.fori_loop(0, c // SUBLANES, body, 0)
    o_ref[rows, :] += od_ref[...]


def _scan_slow_fixes(fast_ref, q_ref, v_ref, f_ref, o_ref, od_ref, slow_refs, rev):
    def body(ci, carry):
        @pl.when(fast_ref[ci] == 0)
        def _():
            rows = pl.ds(pl.multiple_of(ci * SCAN_CHUNK, SCAN_CHUNK), SCAN_CHUNK)
            _scan_slow_fix(q_ref, v_ref, f_ref, rows, o_ref, od_ref, slow_refs, rev)
        return carry

    lax.fori_loop(0, fast_ref.shape[0], body, 0)


def _state_init(st_ref, s0_ref):
    if s0_ref is None:
        st_ref[...] = jnp.zeros_like(st_ref)
    else:
        for h in range(HEADS):
            o = (h % 2) * HEAD_W
            st_ref[h // 2, :, o:o + HEAD_W] = s0_ref[h].T


def _state_write(st_ref, out_ref):
    for h in range(HEADS):
        o = (h % 2) * HEAD_W
        out_ref[h] = st_ref[h // 2, :, o:o + HEAD_W].T


def _lower_bound(lbl_ref, layer, direction):
    n_layers = lbl_ref.shape[0] // 2
    rows = [lbl_ref[2 * i + direction:2 * i + direction + 1, :] for i in range(n_layers)]
    m = functools.reduce(jnp.maximum, rows)
    ex = [jnp.exp(r - m) for r in rows]
    return sum(ex[:layer + 1]) / sum(ex)


def _chunk_rows(n_rows, rev):
    order = range(n_rows // SCAN_CHUNK)
    return [slice(ci * SCAN_CHUNK, (ci + 1) * SCAN_CHUNK) for ci in (reversed(order) if rev else order)]


def _scan_scratch(d):
    c = SCAN_CHUNK
    return [
        pltpu.VMEM((d // PAIR_W, HEAD_W, PAIR_W), F32),
        pltpu.VMEM((c, d), F32),
        pltpu.VMEM((c, d), F32), pltpu.VMEM((c, d), F32),
        pltpu.VMEM((c, d), F32), pltpu.VMEM((c, d), F32),
    ]


def _const_spec(shape):
    nd = len(shape)
    return pl.BlockSpec(shape, lambda *_: (0,) * nd, pipeline_mode=pl.Buffered(1))


def _hg_fwd_kernel(*refs, has_s0, layer):
    if has_s0:
        x_ref, mod_ref, g_ref, w_ref, lbl_ref, s0_ref = refs[:6]
        rest = refs[6:]
    else:
        x_ref, mod_ref, g_ref, w_ref, lbl_ref = refs[:5]
        s0_ref = None
        rest = refs[5:]
    of_ref, q_ref, v_ref, fb_ref, gate_ref, sf_ref = rest[:6]
    st_ref, od_ref = rest[6:8]
    slow_refs = rest[8:12]
    ff_ref, fast_ref = rest[12:14]
    j = pl.program_id(1)
    d = x_ref.shape[-1]

    @pl.when(j == 0)
    def _():
        _state_init(st_ref, s0_ref)

    sh, sc = mod_ref[0:1, :], mod_ref[1:2, :]
    h = (_rms(x_ref[...], g_ref[...]) * (1.0 + sc) + sh).astype(BF16)
    proj = lambda i: _dot(h, w_ref[:, i * d:(i + 1) * d])
    lb_f = _lower_bound(lbl_ref, layer, 0)
    lb_b = _lower_bound(lbl_ref, layer, 1)
    q = _silu(proj(0))
    f_f = _forget_gate(proj(1), lb_f)
    zi = proj(3)
    q_ref[...] = q.astype(BF16)
    v_ref[...] = zi.astype(BF16)
    ff_ref[...] = f_f
    k_f = 1.0 - f_f
    g_f = jnp.log2(f_f)

    def backward_gate():
        fb_ref[...] = _forget_gate(proj(2), lb_b)

    def output_gate():
        gate_ref[...] = _silu(proj(4)).astype(BF16)

    fillers = [backward_gate, output_gate]
    for rows in _chunk_rows(x_ref.shape[0], rev=False):
        operands, fast = _scan_prep(q[rows], k_f[rows], g_f[rows], zi[rows], rev=False)
        fast_ref[rows.start // SCAN_CHUNK] = fast.astype(jnp.int32)
        if fillers:
            fillers.pop(0)()
        of_ref[rows, :] = _scan_finish(operands, st_ref, rev=False)
    for filler in fillers:
        filler()
    _scan_slow_fixes(fast_ref, q_ref, v_ref, ff_ref, of_ref, od_ref, slow_refs, rev=False)

    @pl.when(j == pl.num_programs(1) - 1)
    def _():
        _state_write(st_ref, sf_ref)


def _hg_fwd_call(x, mod, mod_row0, mod_stride, g, w_in, lbl, state, layer_a):
    bn, l, d = x.shape
    c = min(HG_TOKEN_BLOCK, l)
    assert l % c == 0 and c % SCAN_CHUNK == 0
    n = l // c
    tok = lambda b, j: (b, j, 0)
    in_specs = [
        pl.BlockSpec((None, c, d), tok),
        pl.BlockSpec((None, N_MOD, d), lambda b, j: (mod_row0 + mod_stride * b, 0, 0)),
        _const_spec((1, d)),
        _const_spec(w_in.shape),
        _const_spec(lbl.shape),
    ]
    args = [x, mod, g, w_in, lbl]
    if state is not None:
        in_specs.append(pl.BlockSpec((None, None, None, HEADS, HEAD_W, HEAD_W),
                                     lambda b, j: (b, layer_a, 0, 0, 0, 0)))
        args.append(state)
    st_spec = pl.BlockSpec((None, HEADS, HEAD_W, HEAD_W), lambda b, j: (b, 0, 0, 0))
    return pl.pallas_call(
        functools.partial(_hg_fwd_kernel, has_s0=state is not None, layer=layer_a),
        grid=(bn, n),
        in_specs=in_specs,
        out_specs=[pl.BlockSpec((None, c, d), tok)] * 5 + [st_spec],
        out_shape=[
            jax.ShapeDtypeStruct((bn, l, d), F32),
            jax.ShapeDtypeStruct((bn, l, d), BF16),
            jax.ShapeDtypeStruct((bn, l, d), BF16),
            jax.ShapeDtypeStruct((bn, l, d), F32),
            jax.ShapeDtypeStruct((bn, l, d), BF16),
            jax.ShapeDtypeStruct((bn, HEADS, HEAD_W, HEAD_W), F32),
        ],
        scratch_shapes=_scan_scratch(d) + [
            pltpu.VMEM((c, d), F32),
            pltpu.SMEM((c // SCAN_CHUNK,), jnp.int32),
        ],
        compiler_params=pltpu.CompilerParams(
            dimension_semantics=("arbitrary", "arbitrary"), vmem_limit_bytes=VMEM_LIMIT),
        name="hg_fwd",
    )(*args)


def _mlp_tail(x1, mod_ref, g2, w1_ref, w2_ref):
    sh2, sc2, gt2 = mod_ref[3:4, :], mod_ref[4:5, :], mod_ref[5:6, :]
    h2 = (_rms(x1, g2) * (1.0 + sc2) + sh2).astype(BF16)
    a = jnp.maximum(_dot(h2, w1_ref[...]), 0.0)
    return x1 + gt2 * _dot((a * a).astype(BF16), w2_ref[...])


def _hg_bwd_kernel(*refs, has_s0, final, n_blocks):
    x_ref, mod_ref, gate_ref, of_ref, q_ref, v_ref, fb_ref, on_ref, g2_ref, fg_ref = refs[:10]
    wo_ref, w1_ref, w2_ref = refs[10:13]
    if has_s0:
        s0_ref = refs[13]
        rest = refs[14:]
    else:
        s0_ref = None
        rest = refs[13:]
    y_ref, sb_ref = rest[:2]
    st_ref, od_ref = rest[2:4]
    slow_refs = rest[4:8]
    o_ref, fast_ref = rest[8:10]
    s = pl.program_id(0)
    total = pl.num_programs(0) - 1
    live = s < total
    j = jnp.minimum(s, total - 1) % n_blocks
    d = x_ref.shape[-1]

    @pl.when(s == 0)
    def _():
        o_ref[...] = jnp.zeros_like(o_ref)

    @pl.when(live & (j == 0))
    def _():
        _state_init(st_ref, s0_ref)

    chunk_rows = _chunk_rows(q_ref.shape[0], rev=True)
    n_slabs = 2 * len(chunk_rows)
    slab_w = w1_ref.shape[1] // n_slabs


    o = o_ref[...]
    parts = []
    for h in range(d // HEAD_W):
        oh = o[:, h * HEAD_W:(h + 1) * HEAD_W]
        parts.append(oh * lax.rsqrt(jnp.mean(oh * oh, axis=-1, keepdims=True) + EPS))
    o = jnp.concatenate(parts, axis=1) * on_ref[...] * gate_ref[...].astype(F32)
    gt1 = mod_ref[2:3, :]
    x1 = x_ref[...] + gt1 * _dot(o.astype(BF16), wo_ref[...])
    sh2, sc2, gt2 = mod_ref[3:4, :], mod_ref[4:5, :], mod_ref[5:6, :]
    h2 = (_rms(x1, g2_ref[...]) * (1.0 + sc2) + sh2).astype(BF16)

    def mlp_slab(i):
        a = jnp.maximum(_dot(h2, w1_ref[:, i * slab_w:(i + 1) * slab_w]), 0.0)
        return _dot((a * a).astype(BF16), w2_ref[i * slab_w:(i + 1) * slab_w, :])

    mlp = 0.0
    for ci, rows in enumerate(chunk_rows):
        f_b = fb_ref[rows, :]
        operands, fast = _scan_prep(q_ref[rows, :].astype(F32), 1.0 - f_b, jnp.log2(f_b),
                                    v_ref[rows, :].astype(F32), rev=True)
        fast_ref[rows.start // SCAN_CHUNK] = fast.astype(jnp.int32)
        mlp = mlp + mlp_slab(2 * ci)
        o_ref[rows, :] = of_ref[rows, :] + _scan_finish(operands, st_ref, rev=True)
        mlp = mlp + mlp_slab(2 * ci + 1)
    x2 = x1 + gt2 * mlp
    y_ref[...] = _rms(x2, fg_ref[...]) if final else x2
    _scan_slow_fixes(fast_ref, q_ref, v_ref, fb_ref, o_ref, od_ref, slow_refs, rev=True)

    @pl.when(live & (j == n_blocks - 1))
    def _():
        _state_write(st_ref, sb_ref)


def _hg_bwd_call(x, mod, mod_row0, mod_stride, stash, onorm, g2, fg, w_out, w1, w2, state, layer_a, final):
    bn, l, d = x.shape
    c = min(HG_TOKEN_BLOCK, l)
    assert l % c == 0 and c % SCAN_CHUNK == 0
    n = l // c
    total = bn * n

    def cur(s):
        t = jnp.minimum(s, total - 1)
        return t // n, n - 1 - t % n

    def prev(s):
        t = jnp.maximum(s - 1, 0)
        return t // n, n - 1 - t % n

    tok_cur = lambda s: (*cur(s), 0)
    tok_prev = lambda s: (*prev(s), 0)
    o_f, q, v, f_b, gate = stash
    in_specs = [
        pl.BlockSpec((None, c, d), tok_prev),
        pl.BlockSpec((None, N_MOD, d), lambda s: (mod_row0 + mod_stride * prev(s)[0], 0, 0)),
        pl.BlockSpec((None, c, d), tok_prev),
    ] + [pl.BlockSpec((None, c, d), tok_cur)] * 4 + [
        _const_spec((1, d)), _const_spec((1, d)), _const_spec((1, d)),
        _const_spec(w_out.shape), _const_spec(w1.shape), _const_spec(w2.shape),
    ]
    args = [x, mod, gate, o_f, q, v, f_b, onorm, g2, fg, w_out, w1, w2]
    if state is not None:
        in_specs.append(pl.BlockSpec((None, None, None, HEADS, HEAD_W, HEAD_W),
                                     lambda s: (cur(s)[0], layer_a, 1, 0, 0, 0)))
        args.append(state)
    st_spec = pl.BlockSpec((None, HEADS, HEAD_W, HEAD_W), lambda s: (cur(s)[0], 0, 0, 0))
    return pl.pallas_call(
        functools.partial(_hg_bwd_kernel, has_s0=state is not None, final=final, n_blocks=n),
        grid=(total + 1,),
        in_specs=in_specs,
        out_specs=[pl.BlockSpec((None, c, d), tok_prev), st_spec],
        out_shape=[
            jax.ShapeDtypeStruct((bn, l, d), F32),
            jax.ShapeDtypeStruct((bn, HEADS, HEAD_W, HEAD_W), F32),
        ],
        scratch_shapes=_scan_scratch(d) + [
            pltpu.VMEM((c, d), F32),
            pltpu.SMEM((c // SCAN_CHUNK,), jnp.int32),
        ],
        compiler_params=pltpu.CompilerParams(
            dimension_semantics=("arbitrary",), vmem_limit_bytes=VMEM_LIMIT),
        name="hg_bwd",
    )(*args)


def _cm_kernel(x_ref, mod_ref, g1_ref, g2_ref, fg_ref, win_ref, lng_ref, lnb_ref, ws_ref, bst_ref,
               wo_ref, w1_ref, w2_ref, y_ref, *, final):
    tb, d = x_ref.shape
    n_chunks = tb // MIX_CHUNK
    gw = d // CM_GROUPS
    x = x_ref[...]
    sh, sc, gt1 = mod_ref[0:1, :], mod_ref[1:2, :], mod_ref[2:3, :]
    h = (_rms(x, g1_ref[...]) * (1.0 + sc) + sh).astype(BF16)
    u = _gelu_tanh(_dot(h, win_ref[:, :d]))
    vv = _gelu_tanh(_dot(h, win_ref[:, d:]))
    mu = jnp.mean(vv, axis=-1, keepdims=True)
    vc = vv - mu
    var = jnp.mean(vc * vc, axis=-1, keepdims=True)
    vn = (vc * lax.rsqrt(var + EPS) * lng_ref[...] + lnb_ref[...]).astype(BF16)
    per_group = []
    for g in range(CM_GROUPS):
        rhs = jnp.concatenate(
            [vn[ch * MIX_CHUNK:(ch + 1) * MIX_CHUNK, g * gw:(g + 1) * gw] for ch in range(n_chunks)], axis=1)
        per_group.append(_dot(ws_ref[g], rhs) + bst_ref[:, g:g + 1])
    s = jnp.concatenate(
        [jnp.concatenate([pg[:, ch * gw:(ch + 1) * gw] for pg in per_group], axis=1) for ch in range(n_chunks)],
        axis=0)
    x1 = x + gt1 * _dot((u * s).astype(BF16), wo_ref[...])
    x2 = _mlp_tail(x1, mod_ref, g2_ref[...], w1_ref, w2_ref)
    y_ref[...] = _rms(x2, fg_ref[...]) if final else x2


def _cm_call(x, mod, mod_row0, mod_stride, g1, g2, fg, w_in, ln_g, ln_b, w_s, b_st, w_out, w1, w2, final):
    out_shape = x.shape
    if mod_stride == 0 and x.shape[1] % MIX_CHUNK == 0:
        x = x.reshape(1, -1, x.shape[-1])
    bn, l, d = x.shape
    tb = min(CM_TOKEN_BLOCK, l)
    assert l % tb == 0 and tb % MIX_CHUNK == 0
    tok = lambda b, j: (b, j, 0)
    consts = [g1, g2, fg, w_in, ln_g, ln_b, w_s, b_st, w_out, w1, w2]
    return pl.pallas_call(
        functools.partial(_cm_kernel, final=final),
        grid=(bn, l // tb),
        in_specs=[
            pl.BlockSpec((None, tb, d), tok),
            pl.BlockSpec((None, N_MOD, d), lambda b, j: (mod_row0 + mod_stride * b, 0, 0)),
        ] + [_const_spec(a.shape) for a in consts],
        out_specs=pl.BlockSpec((None, tb, d), tok),
        out_shape=jax.ShapeDtypeStruct((bn, l, d), F32),
        compiler_params=pltpu.CompilerParams(
            dimension_semantics=("arbitrary", "arbitrary"), vmem_limit_bytes=VMEM_LIMIT),
        name="cm",
    )(x, mod, *consts).reshape(out_shape)


def kernel(x_prompt, x_sample, state_hgrn, c, c_ctx, ada_w, ada_b, norm_mix_g, norm_mlp_g, mlp_w1, mlp_w2,
           hgrn_w_in, hgrn_lb_logits, hgrn_onorm_g, hgrn_w_out, cm_w_in, cm_ln_g, cm_ln_b, cm_w_s, cm_b_s,
           cm_w_out, final_norm_g):
    depth, d, _ = ada_w.shape
    n_lat = c.shape[0]
    assert d % PAIR_W == 0 and d // HEAD_W == HEADS and 1 + n_lat <= COND_ROWS
    assert x_prompt.shape[1] % SCAN_CHUNK == 0 and x_sample.shape[1] % SCAN_CHUNK == 0

    cond = jnp.concatenate([c_ctx[None, :], c, jnp.zeros((COND_ROWS - 1 - n_lat, d), F32)], axis=0)
    mods = _ada_call(cond, ada_w, ada_b).reshape(depth, COND_ROWS, N_MOD, d)

    row = lambda a: a.reshape(1, d)
    lbl = hgrn_lb_logits.reshape(-1, d)
    fg = row(final_norm_g)
    ctx, lat = x_prompt, x_sample
    new_states = []
    for i in range(depth):
        j = i // N_MIXERS
        final = i == depth - 1
        w1 = mlp_w1[i].astype(BF16)
        w2 = mlp_w2[i].astype(BF16)
        g1, g2 = row(norm_mix_g[i]), row(norm_mlp_g[i])
        if i % N_MIXERS == 0:
            w_in = hgrn_w_in[j].astype(BF16)
            w_out = hgrn_w_out[j].astype(BF16)
            onorm = row(hgrn_onorm_g[j])
            *stash, s_f = _hg_fwd_call(ctx, mods[i], 0, 0, g1, w_in, lbl, None, j)
            ctx, s_b = _hg_bwd_call(ctx, mods[i], 0, 0, stash, onorm, g2, fg, w_out, w1, w2, None, j, final)
            new_states.append(jnp.stack([s_f, s_b], axis=1))
            *stash, _ = _hg_fwd_call(lat, mods[i], 1, 1, g1, w_in, lbl, state_hgrn, j)
            lat, _ = _hg_bwd_call(lat, mods[i], 1, 1, stash, onorm, g2, fg, w_out, w1, w2, state_hgrn, j, final)
        else:
            cm_args = (g1, g2, fg, cm_w_in[j].astype(BF16), row(cm_ln_g[j]), row(cm_ln_b[j]),
                       cm_w_s[j].astype(BF16), cm_b_s[j].T, cm_w_out[j].astype(BF16), w1, w2, final)
            ctx = _cm_call(ctx, mods[i], 0, 0, *cm_args)
            lat = _cm_call(lat, mods[i], 1, 1, *cm_args)
    new_state = jnp.stack(new_states, axis=1).astype(x_prompt.dtype)
    return ctx, lat, new_state
```

```python
import functools

import jax
import jax.numpy as jnp
from jax import lax
from jax.experimental import pallas as pl
from jax.experimental.pallas import tpu as pltpu

F32 = jnp.float32
BF16 = jnp.bfloat16

EPS = 1e-6
N_MOD = 6
N_MIXERS = 2
HEADS = 8
HEAD_W = 128
PAIR_W = 2 * HEAD_W
MIX_CHUNK = 128
CM_GROUPS = 8
SCAN_BLOCK = 64
SCAN_CHUNK = 2 * SCAN_BLOCK
SUBLANES = 8
COND_ROWS = SUBLANES
SAFE_LOG2_DECAY = -115.0
HG_TOKEN_BLOCK = 512
CM_TOKEN_BLOCK = 512
VMEM_LIMIT = 56 * 1024 * 1024


def _silu(x):
    h = 0.5 * x
    return h * jnp.tanh(h) + h


def _forget_gate(z, lb):
    return 0.5 * (1.0 + lb) + (0.5 * (1.0 - lb)) * jnp.tanh(0.5 * z)


def _gelu_tanh(x):
    h = 0.5 * x
    t = jnp.tanh(x * (0.7978845608028654 + (0.7978845608028654 * 0.044715) * (x * x)))
    return h * t + h


def _rms(x, g):
    return x * lax.rsqrt(jnp.mean(x * x, axis=-1, keepdims=True) + EPS) * g


def _dot(a, b):
    return jnp.dot(a, b, preferred_element_type=F32)


def _dot_nt(a, b):
    return lax.dot_general(a, b, (((1,), (1,)), ((), ())), preferred_element_type=F32)


def _dot_tn(a, b):
    return lax.dot_general(a, b, (((0,), (0,)), ((), ())), preferred_element_type=F32)


def _ada_kernel(cond_ref, w_ref, b_ref, out_ref):
    a = _silu(cond_ref[...]).astype(BF16)
    out_ref[...] = _dot(a, w_ref[...].astype(BF16)) + b_ref[...]


def _ada_call(cond, ada_w, ada_b):
    depth, d, _ = ada_w.shape
    return pl.pallas_call(
        _ada_kernel,
        grid=(depth, N_MOD),
        in_specs=[
            pl.BlockSpec((COND_ROWS, d), lambda i, n: (0, 0)),
            pl.BlockSpec((None, d, d), lambda i, n: (i, 0, n)),
            pl.BlockSpec((None, 1, d), lambda i, n: (i, 0, n)),
        ],
        out_specs=pl.BlockSpec((None, COND_ROWS, d), lambda i, n: (i, 0, n)),
        out_shape=jax.ShapeDtypeStruct((depth, COND_ROWS, N_MOD * d), F32),
        compiler_params=pltpu.CompilerParams(dimension_semantics=("arbitrary", "arbitrary")),
        name="ada",
    )(cond, ada_w, ada_b.reshape(depth, 1, N_MOD * d))


def _scan_masks(rev):
    c = SCAN_CHUNK
    t = lax.broadcasted_iota(jnp.int32, (c, 2 * c), 0)
    s = lax.broadcasted_iota(jnp.int32, (c, 2 * c), 1) & (c - 1)
    causal = (s >= t) if rev else (s <= t)
    diag = ((t >= SCAN_BLOCK) == (s >= SCAN_BLOCK)) & causal
    s_row = lax.broadcasted_iota(jnp.int32, (1, 2 * c), 1) & (c - 1)
    near_keys = (s_row >= SCAN_BLOCK) if rev else (s_row < SCAN_BLOCK)
    return diag, near_keys


def _block_cumsum(g, rev):
    c = g.shape[0]
    t = lax.broadcasted_iota(jnp.int32, (c, 2 * c), 0)
    s = lax.broadcasted_iota(jnp.int32, (c, 2 * c), 1) & (c - 1)
    same = (t >= SCAN_BLOCK) == (s >= SCAN_BLOCK)
    tri = jnp.where(same & ((s >= t) if rev else (s <= t)), 1.0, 0.0).astype(BF16)
    g_hi = g.astype(BF16)
    g_lo = (g - g_hi.astype(F32)).astype(BF16)
    return _dot(tri, jnp.concatenate([g_hi, g_lo], axis=0))


def _scan_prep(q, k, g, v, rev):
    c, d = q.shape
    b_in = _block_cumsum(g, rev)
    b_lo, b_hi = b_in[:SCAN_BLOCK], b_in[SCAN_BLOCK:]

    def rows(lo_part, hi_part):
        return jnp.concatenate([lo_part, hi_part], axis=0)

    half = SCAN_BLOCK // 2
    if rev:
        tot_lo, tot_hi = b_lo[0:1], b_hi[0:1]
        mid_lo, mid_hi = b_lo[half:half + 1], b_hi[half:half + 1]
    else:
        tot_lo, tot_hi = b_lo[SCAN_BLOCK - 1:], b_hi[SCAN_BLOCK - 1:]
        mid_lo, mid_hi = b_lo[half - 1:half], b_hi[half - 1:half]

    qh = q * jnp.exp2(b_in)
    ke = k * jnp.exp2(rows(tot_lo - b_lo, tot_hi - b_hi))
    if rev:
        qi = rows(qh[:SCAN_BLOCK] * jnp.exp2(tot_hi), qh[SCAN_BLOCK:])
        ks = rows(ke[:SCAN_BLOCK], ke[SCAN_BLOCK:] * jnp.exp2(tot_lo))
    else:
        qi = rows(qh[:SCAN_BLOCK], qh[SCAN_BLOCK:] * jnp.exp2(tot_lo))
        ks = rows(ke[:SCAN_BLOCK] * jnp.exp2(tot_hi), ke[SCAN_BLOCK:])
    qi = qi.astype(BF16)
    ks = ks.astype(BF16)
    dec = jnp.exp2(tot_lo + tot_hi)
    qh16 = qh.astype(BF16)
    ke16 = ke.astype(BF16)
    v16 = v.astype(BF16)
    qm16 = (q * jnp.exp2(rows(b_lo - mid_lo, b_hi - mid_hi))).astype(BF16)
    km16 = (k * jnp.exp2(rows(mid_lo - b_lo, mid_hi - b_hi))).astype(BF16)

    half_sums = jnp.minimum(jnp.minimum(mid_lo, tot_lo - mid_lo), jnp.minimum(mid_hi, tot_hi - mid_hi))
    fast_vec = jnp.min(half_sums, axis=-1, keepdims=True) >= SAFE_LOG2_DECAY
    fast = jnp.min(half_sums) >= SAFE_LOG2_DECAY

    return (qm16, km16, qh16, ke16, v16, qi, ks, dec, fast_vec), fast


def _scan_finish(operands, st_ref, rev):
    qm16, km16, qh16, ke16, v16, qi, ks, dec, fast_vec = operands
    c = qi.shape[0]
    n_pairs = qi.shape[1] // PAIR_W
    diag_mask, near_keys = _scan_masks(rev)
    diag_mask = diag_mask & fast_vec
    lo, hi = slice(0, SCAN_BLOCK), slice(SCAN_BLOCK, c)
    near, far = (hi, lo) if rev else (lo, hi)
    left = lax.broadcasted_iota(jnp.int32, (1, PAIR_W), 1) < HEAD_W

    def by_head(a):
        zero = jnp.zeros_like(a)
        return jnp.concatenate([jnp.where(left, a, zero), jnp.where(left, zero, a)], axis=0)

    outs = []
    for pr in range(n_pairs):
        sl = slice(pr * PAIR_W, (pr + 1) * PAIR_W)
        attd = _dot_nt(qm16[:, sl], by_head(km16[:, sl]))
        attx = _dot_nt(qh16[far, sl], by_head(ke16[:, sl]))
        att_near = jnp.where(diag_mask[near], attd[near], 0.0)
        att_far = jnp.where(diag_mask[far], attd[far], jnp.where(near_keys, attx, 0.0))
        att = jnp.concatenate([att_far, att_near] if rev else [att_near, att_far], axis=0)
        st = st_ref[pr]
        v_p = v16[:, sl]
        o = _dot(att.astype(BF16), by_head(v_p)) + _dot_nt(qi[:, sl], by_head(st.astype(BF16)))
        outs.append(o)
        v_stack = jnp.concatenate([v_p[:, :HEAD_W], v_p[:, HEAD_W:]], axis=0)
        st_ref[pr] = st * dec[:, sl] + _dot_tn(v_stack, by_head(ks[:, sl]))
    return jnp.concatenate(outs, axis=1)


def _scan_slow_fix(q_ref, v_ref, f_ref, rows, o_ref, od_ref, slow_refs, rev):
    q_s, k_s, v_s, b_s = slow_refs
    c, d = od_ref.shape
    f = f_ref[rows, :]
    q_s[...] = q_ref[rows, :].astype(F32)
    k_s[...] = 1.0 - f
    v_s[...] = v_ref[rows, :].astype(F32)
    b_s[...] = _block_cumsum(jnp.log2(f), rev)
    row = lax.broadcasted_iota(jnp.int32, (c, 1), 0)

    def body(grp, carry):
        t0 = pl.multiple_of(grp * SUBLANES, SUBLANES)
        q8 = q_s[pl.ds(t0, SUBLANES), :]
        b8 = b_s[pl.ds(t0, SUBLANES), :]
        out_rows = []
        for r in range(SUBLANES):
            ti = t0 + r
            prod = q8[r:r + 1] * k_s[...] * jnp.exp2(jnp.minimum(b8[r:r + 1] - b_s[...], 0.0))
            valid = ((row >= SCAN_BLOCK) == (ti >= SCAN_BLOCK)) & ((row >= ti) if rev else (row <= ti))
            heads = []
            for h in range(d // HEAD_W):
                hs = slice(h * HEAD_W, (h + 1) * HEAD_W)
                w = jnp.where(valid, jnp.sum(prod[:, hs], axis=-1, keepdims=True), 0.0)
                heads.append(jnp.sum(w * v_s[:, hs], axis=0, keepdims=True))
            out_rows.append(jnp.concatenate(heads, axis=1))
        od_ref[pl.ds(t0, SUBLANES), :] = jnp.concatenate(out_rows, axis=0)
        return carry

    lax.fori_loop(0, c // SUBLANES, body, 0)
    o_ref[rows, :] += od_ref[...]


def _scan_slow_fixes(fast_ref, q_ref, v_ref, f_ref, o_ref, od_ref, slow_refs, rev):
    def body(ci, carry):
        @pl.when(fast_ref[ci] == 0)
        def _():
            rows = pl.ds(pl.multiple_of(ci * SCAN_CHUNK, SCAN_CHUNK), SCAN_CHUNK)
            _scan_slow_fix(q_ref, v_ref, f_ref, rows, o_ref, od_ref, slow_refs, rev)
        return carry

    lax.fori_loop(0, fast_ref.shape[0], body, 0)


def _state_init(st_ref, s0_ref):
    if s0_ref is None:
        st_ref[...] = jnp.zeros_like(st_ref)
    else:
        for h in range(HEADS):
            o = (h % 2) * HEAD_W
            st_ref[h // 2, :, o:o + HEAD_W] = s0_ref[h].T


def _state_write(st_ref, out_ref):
    for h in range(HEADS):
        o = (h % 2) * HEAD_W
        out_ref[h] = st_ref[h // 2, :, o:o + HEAD_W].T


def _lower_bound(lbl_ref, layer, direction):
    n_layers = lbl_ref.shape[0] // 2
    rows = [lbl_ref[2 * i + direction:2 * i + direction + 1, :] for i in range(n_layers)]
    m = functools.reduce(jnp.maximum, rows)
    ex = [jnp.exp(r - m) for r in rows]
    return sum(ex[:layer + 1]) / sum(ex)


def _chunk_rows(n_rows, rev):
    order = range(n_rows // SCAN_CHUNK)
    return [slice(ci * SCAN_CHUNK, (ci + 1) * SCAN_CHUNK) for ci in (reversed(order) if rev else order)]


def _scan_scratch(d):
    c = SCAN_CHUNK
    return [
        pltpu.VMEM((d // PAIR_W, HEAD_W, PAIR_W), F32),
        pltpu.VMEM((c, d), F32),
        pltpu.VMEM((c, d), F32), pltpu.VMEM((c, d), F32),
        pltpu.VMEM((c, d), F32), pltpu.VMEM((c, d), F32),
    ]


def _const_spec(shape):
    nd = len(shape)
    return pl.BlockSpec(shape, lambda *_: (0,) * nd, pipeline_mode=pl.Buffered(1))


def _seqs_per_block(n_seq, seq_rows, mod_stride, state):
    k = HG_TOKEN_BLOCK // seq_rows
    return k if k > 1 and mod_stride == 0 and state is None and n_seq % k == 0 else 1


def _hg_fwd_kernel(*refs, has_s0, layer, seq_rows):
    if has_s0:
        x_ref, mod_ref, g_ref, w_ref, lbl_ref, s0_ref = refs[:6]
        rest = refs[6:]
    else:
        x_ref, mod_ref, g_ref, w_ref, lbl_ref = refs[:5]
        s0_ref = None
        rest = refs[5:]
    of_ref, q_ref, v_ref, fb_ref, gate_ref, sf_ref = rest[:6]
    st_ref, od_ref = rest[6:8]
    slow_refs = rest[8:12]
    ff_ref, fast_ref = rest[12:14]
    j = pl.program_id(1)
    d = x_ref.shape[-1]
    whole_seqs = seq_rows <= x_ref.shape[0]

    if not whole_seqs:
        @pl.when(j == 0)
        def _():
            _state_init(st_ref, s0_ref)

    sh, sc = mod_ref[0:1, :], mod_ref[1:2, :]
    h = (_rms(x_ref[...], g_ref[...]) * (1.0 + sc) + sh).astype(BF16)
    proj = lambda i: _dot(h, w_ref[:, i * d:(i + 1) * d])
    lb_f = _lower_bound(lbl_ref, layer, 0)
    lb_b = _lower_bound(lbl_ref, layer, 1)
    q = _silu(proj(0))
    f_f = _forget_gate(proj(1), lb_f)
    zi = proj(3)
    q_ref[...] = q.astype(BF16)
    v_ref[...] = zi.astype(BF16)
    ff_ref[...] = f_f
    k_f = 1.0 - f_f
    g_f = jnp.log2(f_f)

    def backward_gate():
        fb_ref[...] = _forget_gate(proj(2), lb_b)

    def output_gate():
        gate_ref[...] = _silu(proj(4)).astype(BF16)

    fillers = [backward_gate, output_gate]
    for rows in _chunk_rows(x_ref.shape[0], rev=False):
        operands, fast = _scan_prep(q[rows], k_f[rows], g_f[rows], zi[rows], rev=False)
        fast_ref[rows.start // SCAN_CHUNK] = fast.astype(jnp.int32)
        if fillers:
            fillers.pop(0)()
        if whole_seqs and rows.start % seq_rows == 0:
            _state_init(st_ref, s0_ref)
        of_ref[rows, :] = _scan_finish(operands, st_ref, rev=False)
        if whole_seqs and rows.stop % seq_rows == 0:
            _state_write(st_ref, sf_ref.at[rows.start // seq_rows])
    for filler in fillers:
        filler()
    _scan_slow_fixes(fast_ref, q_ref, v_ref, ff_ref, of_ref, od_ref, slow_refs, rev=False)

    if not whole_seqs:
        @pl.when(j == pl.num_programs(1) - 1)
        def _():
            _state_write(st_ref, sf_ref.at[0])


def _hg_fwd_call(x, mod, mod_row0, mod_stride, g, w_in, lbl, state, layer_a):
    n_seq, seq_rows, d = x.shape
    k = _seqs_per_block(n_seq, seq_rows, mod_stride, state)
    x = x.reshape(n_seq // k, k * seq_rows, d)
    bn, l, _ = x.shape
    c = min(HG_TOKEN_BLOCK, l)
    assert l % c == 0 and c % SCAN_CHUNK == 0
    n = l // c
    tok = lambda b, j: (b, j, 0)
    in_specs = [
        pl.BlockSpec((None, c, d), tok),
        pl.BlockSpec((None, N_MOD, d), lambda b, j: (mod_row0 + mod_stride * b, 0, 0)),
        _const_spec((1, d)),
        _const_spec(w_in.shape),
        _const_spec(lbl.shape),
    ]
    args = [x, mod, g, w_in, lbl]
    if state is not None:
        in_specs.append(pl.BlockSpec((None, None, None, HEADS, HEAD_W, HEAD_W),
                                     lambda b, j: (b, layer_a, 0, 0, 0, 0)))
        args.append(state)
    st_spec = pl.BlockSpec((k, HEADS, HEAD_W, HEAD_W), lambda b, j: (b, 0, 0, 0))
    return pl.pallas_call(
        functools.partial(_hg_fwd_kernel, has_s0=state is not None, layer=layer_a, seq_rows=seq_rows),
        grid=(bn, n),
        in_specs=in_specs,
        out_specs=[pl.BlockSpec((None, c, d), tok)] * 5 + [st_spec],
        out_shape=[
            jax.ShapeDtypeStruct((bn, l, d), F32),
            jax.ShapeDtypeStruct((bn, l, d), BF16),
            jax.ShapeDtypeStruct((bn, l, d), BF16),
            jax.ShapeDtypeStruct((bn, l, d), F32),
            jax.ShapeDtypeStruct((bn, l, d), BF16),
            jax.ShapeDtypeStruct((n_seq, HEADS, HEAD_W, HEAD_W), F32),
        ],
        scratch_shapes=_scan_scratch(d) + [
            pltpu.VMEM((c, d), F32),
            pltpu.SMEM((c // SCAN_CHUNK,), jnp.int32),
        ],
        compiler_params=pltpu.CompilerParams(
            dimension_semantics=("arbitrary", "arbitrary"), vmem_limit_bytes=VMEM_LIMIT),
        name="hg_fwd",
    )(*args)


def _mlp_tail(x1, mod_ref, g2, w1_ref, w2_ref):
    sh2, sc2, gt2 = mod_ref[3:4, :], mod_ref[4:5, :], mod_ref[5:6, :]
    h2 = (_rms(x1, g2) * (1.0 + sc2) + sh2).astype(BF16)
    a = jnp.maximum(_dot(h2, w1_ref[...]), 0.0)
    return x1 + gt2 * _dot((a * a).astype(BF16), w2_ref[...])


def _hg_bwd_kernel(*refs, has_s0, final, n_blocks, seq_rows):
    x_ref, mod_ref, gate_ref, of_ref, q_ref, v_ref, fb_ref, on_ref, g2_ref, fg_ref = refs[:10]
    wo_ref, w1_ref, w2_ref = refs[10:13]
    if has_s0:
        s0_ref = refs[13]
        rest = refs[14:]
    else:
        s0_ref = None
        rest = refs[13:]
    y_ref, sb_ref = rest[:2]
    st_ref, od_ref = rest[2:4]
    slow_refs = rest[4:8]
    o_ref, fast_ref = rest[8:10]
    s = pl.program_id(0)
    total = pl.num_programs(0) - 1
    live = s < total
    j = jnp.minimum(s, total - 1) % n_blocks
    d = x_ref.shape[-1]

    @pl.when(s == 0)
    def _():
        o_ref[...] = jnp.zeros_like(o_ref)

    whole_seqs = seq_rows <= q_ref.shape[0]
    if not whole_seqs:
        @pl.when(live & (j == 0))
        def _():
            _state_init(st_ref, s0_ref)

    chunk_rows = _chunk_rows(q_ref.shape[0], rev=True)
    n_slabs = 2 * len(chunk_rows)
    slab_w = w1_ref.shape[1] // n_slabs


    o = o_ref[...]
    parts = []
    for h in range(d // HEAD_W):
        oh = o[:, h * HEAD_W:(h + 1) * HEAD_W]
        parts.append(oh * lax.rsqrt(jnp.mean(oh * oh, axis=-1, keepdims=True) + EPS))
    o = jnp.concatenate(parts, axis=1) * on_ref[...] * gate_ref[...].astype(F32)
    gt1 = mod_ref[2:3, :]
    x1 = x_ref[...] + gt1 * _dot(o.astype(BF16), wo_ref[...])
    sh2, sc2, gt2 = mod_ref[3:4, :], mod_ref[4:5, :], mod_ref[5:6, :]
    h2 = (_rms(x1, g2_ref[...]) * (1.0 + sc2) + sh2).astype(BF16)

    def mlp_slab(i):
        a = jnp.maximum(_dot(h2, w1_ref[:, i * slab_w:(i + 1) * slab_w]), 0.0)
        return _dot((a * a).astype(BF16), w2_ref[i * slab_w:(i + 1) * slab_w, :])

    mlp = 0.0
    for ci, rows in enumerate(chunk_rows):
        f_b = fb_ref[rows, :]
        operands, fast = _scan_prep(q_ref[rows, :].astype(F32), 1.0 - f_b, jnp.log2(f_b),
                                    v_ref[rows, :].astype(F32), rev=True)
        fast_ref[rows.start // SCAN_CHUNK] = fast.astype(jnp.int32)
        mlp = mlp + mlp_slab(2 * ci)
        if whole_seqs and rows.stop % seq_rows == 0:
            _state_init(st_ref, s0_ref)
        o_ref[rows, :] = of_ref[rows, :] + _scan_finish(operands, st_ref, rev=True)
        if whole_seqs and rows.start % seq_rows == 0:
            _state_write(st_ref, sb_ref.at[rows.start // seq_rows])
        mlp = mlp + mlp_slab(2 * ci + 1)
    x2 = x1 + gt2 * mlp
    y_ref[...] = _rms(x2, fg_ref[...]) if final else x2
    _scan_slow_fixes(fast_ref, q_ref, v_ref, fb_ref, o_ref, od_ref, slow_refs, rev=True)

    if not whole_seqs:
        @pl.when(live & (j == n_blocks - 1))
        def _():
            _state_write(st_ref, sb_ref.at[0])


def _hg_bwd_call(x, mod, mod_row0, mod_stride, stash, onorm, g2, fg, w_out, w1, w2, state, layer_a, final):
    out_shape = x.shape
    n_seq, seq_rows, d = x.shape
    k = _seqs_per_block(n_seq, seq_rows, mod_stride, state)
    x = x.reshape(n_seq // k, k * seq_rows, d)
    bn, l, _ = x.shape
    c = min(HG_TOKEN_BLOCK, l)
    assert l % c == 0 and c % SCAN_CHUNK == 0
    n = l // c
    total = bn * n

    def cur(s):
        t = jnp.minimum(s, total - 1)
        return t // n, n - 1 - t % n

    def prev(s):
        t = jnp.maximum(s - 1, 0)
        return t // n, n - 1 - t % n

    tok_cur = lambda s: (*cur(s), 0)
    tok_prev = lambda s: (*prev(s), 0)
    o_f, q, v, f_b, gate = stash
    in_specs = [
        pl.BlockSpec((None, c, d), tok_prev),
        pl.BlockSpec((None, N_MOD, d), lambda s: (mod_row0 + mod_stride * prev(s)[0], 0, 0)),
        pl.BlockSpec((None, c, d), tok_prev),
    ] + [pl.BlockSpec((None, c, d), tok_cur)] * 4 + [
        _const_spec((1, d)), _const_spec((1, d)), _const_spec((1, d)),
        _const_spec(w_out.shape), _const_spec(w1.shape), _const_spec(w2.shape),
    ]
    args = [x, mod, gate, o_f, q, v, f_b, onorm, g2, fg, w_out, w1, w2]
    if state is not None:
        in_specs.append(pl.BlockSpec((None, None, None, HEADS, HEAD_W, HEAD_W),
                                     lambda s: (cur(s)[0], layer_a, 1, 0, 0, 0)))
        args.append(state)
    st_spec = pl.BlockSpec((k, HEADS, HEAD_W, HEAD_W), lambda s: (cur(s)[0], 0, 0, 0))
    y, s_b = pl.pallas_call(
        functools.partial(_hg_bwd_kernel, has_s0=state is not None, final=final, n_blocks=n,
                          seq_rows=seq_rows),
        grid=(total + 1,),
        in_specs=in_specs,
        out_specs=[pl.BlockSpec((None, c, d), tok_prev), st_spec],
        out_shape=[
            jax.ShapeDtypeStruct((bn, l, d), F32),
            jax.ShapeDtypeStruct((n_seq, HEADS, HEAD_W, HEAD_W), F32),
        ],
        scratch_shapes=_scan_scratch(d) + [
            pltpu.VMEM((c, d), F32),
            pltpu.SMEM((c // SCAN_CHUNK,), jnp.int32),
        ],
        compiler_params=pltpu.CompilerParams(
            dimension_semantics=("arbitrary",), vmem_limit_bytes=VMEM_LIMIT),
        name="hg_bwd",
    )(*args)
    return y.reshape(out_shape), s_b


def _cm_kernel(x_ref, mod_ref, g1_ref, g2_ref, fg_ref, win_ref, lng_ref, lnb_ref, ws_ref, bst_ref,
               wo_ref, w1_ref, w2_ref, y_ref, *, final):
    tb, d = x_ref.shape
    n_chunks = tb // MIX_CHUNK
    gw = d // CM_GROUPS
    x = x_ref[...]
    sh, sc, gt1 = mod_ref[0:1, :], mod_ref[1:2, :], mod_ref[2:3, :]
    h = (_rms(x, g1_ref[...]) * (1.0 + sc) + sh).astype(BF16)
    u = _gelu_tanh(_dot(h, win_ref[:, :d]))
    vv = _gelu_tanh(_dot(h, win_ref[:, d:]))
    mu = jnp.mean(vv, axis=-1, keepdims=True)
    vc = vv - mu
    var = jnp.mean(vc * vc, axis=-1, keepdims=True)
    vn = (vc * lax.rsqrt(var + EPS) * lng_ref[...] + lnb_ref[...]).astype(BF16)
    per_group = []
    for g in range(CM_GROUPS):
        rhs = jnp.concatenate(
            [vn[ch * MIX_CHUNK:(ch + 1) * MIX_CHUNK, g * gw:(g + 1) * gw] for ch in range(n_chunks)], axis=1)
        per_group.append(_dot(ws_ref[g], rhs) + bst_ref[:, g:g + 1])
    s = jnp.concatenate(
        [jnp.concatenate([pg[:, ch * gw:(ch + 1) * gw] for pg in per_group], axis=1) for ch in range(n_chunks)],
        axis=0)
    x1 = x + gt1 * _dot((u * s).astype(BF16), wo_ref[...])
    x2 = _mlp_tail(x1, mod_ref, g2_ref[...], w1_ref, w2_ref)
    y_ref[...] = _rms(x2, fg_ref[...]) if final else x2


def _cm_call(x, mod, mod_row0, mod_stride, g1, g2, fg, w_in, ln_g, ln_b, w_s, b_st, w_out, w1, w2, final):
    out_shape = x.shape
    if mod_stride == 0 and x.shape[1] % MIX_CHUNK == 0:
        x = x.reshape(1, -1, x.shape[-1])
    bn, l, d = x.shape
    tb = min(CM_TOKEN_BLOCK, l)
    assert l % tb == 0 and tb % MIX_CHUNK == 0
    tok = lambda b, j: (b, j, 0)
    consts = [g1, g2, fg, w_in, ln_g, ln_b, w_s, b_st, w_out, w1, w2]
    return pl.pallas_call(
        functools.partial(_cm_kernel, final=final),
        grid=(bn, l // tb),
        in_specs=[
            pl.BlockSpec((None, tb, d), tok),
            pl.BlockSpec((None, N_MOD, d), lambda b, j: (mod_row0 + mod_stride * b, 0, 0)),
        ] + [_const_spec(a.shape) for a in consts],
        out_specs=pl.BlockSpec((None, tb, d), tok),
        out_shape=jax.ShapeDtypeStruct((bn, l, d), F32),
        compiler_params=pltpu.CompilerParams(
            dimension_semantics=("arbitrary", "arbitrary"), vmem_limit_bytes=VMEM_LIMIT),
        name="cm",
    )(x, mod, *consts).reshape(out_shape)


def kernel(x_prompt, x_sample, state_hgrn, c, c_ctx, ada_w, ada_b, norm_mix_g, norm_mlp_g, mlp_w1, mlp_w2,
           hgrn_w_in, hgrn_lb_logits, hgrn_onorm_g, hgrn_w_out, cm_w_in, cm_ln_g, cm_ln_b, cm_w_s, cm_b_s,
           cm_w_out, final_norm_g):
    depth, d, _ = ada_w.shape
    n_lat = c.shape[0]
    assert d % PAIR_W == 0 and d // HEAD_W == HEADS and 1 + n_lat <= COND_ROWS
    assert x_prompt.shape[1] % SCAN_CHUNK == 0 and x_sample.shape[1] % SCAN_CHUNK == 0

    cond = jnp.concatenate([c_ctx[None, :], c, jnp.zeros((COND_ROWS - 1 - n_lat, d), F32)], axis=0)
    mods = _ada_call(cond, ada_w, ada_b).reshape(depth, COND_ROWS, N_MOD, d)

    row = lambda a: a.reshape(1, d)
    lbl = hgrn_lb_logits.reshape(-1, d)
    fg = row(final_norm_g)
    ctx, lat = x_prompt, x_sample
    new_states = []
    for i in range(depth):
        j = i // N_MIXERS
        final = i == depth - 1
        w1 = mlp_w1[i].astype(BF16)
        w2 = mlp_w2[i].astype(BF16)
        g1, g2 = row(norm_mix_g[i]), row(norm_mlp_g[i])
        if i % N_MIXERS == 0:
            w_in = hgrn_w_in[j].astype(BF16)
            w_out = hgrn_w_out[j].astype(BF16)
            onorm = row(hgrn_onorm_g[j])
            *stash, s_f = _hg_fwd_call(ctx, mods[i], 0, 0, g1, w_in, lbl, None, j)
            ctx, s_b = _hg_bwd_call(ctx, mods[i], 0, 0, stash, onorm, g2, fg, w_out, w1, w2, None, j, final)
            new_states.append(jnp.stack([s_f, s_b], axis=1))
            *stash, _ = _hg_fwd_call(lat, mods[i], 1, 1, g1, w_in, lbl, state_hgrn, j)
            lat, _ = _hg_bwd_call(lat, mods[i], 1, 1, stash, onorm, g2, fg, w_out, w1, w2, state_hgrn, j, final)
        else:
            cm_args = (g1, g2, fg, cm_w_in[j].astype(BF16), row(cm_ln_g[j]), row(cm_ln_b[j]),
                       cm_w_s[j].astype(BF16), cm_b_s[j].T, cm_w_out[j].astype(BF16), w1, w2, final)
            ctx = _cm_call(ctx, mods[i], 0, 0, *cm_args)
            lat = _cm_call(lat, mods[i], 1, 1, *cm_args)
    new_state = jnp.stack(new_states, axis=1).astype(x_prompt.dtype)
    return ctx, lat, new_state
```

```python
import functools

import jax
import jax.numpy as jnp
from jax import lax
from jax.experimental import pallas as pl
from jax.experimental.pallas import tpu as pltpu

F32 = jnp.float32
BF16 = jnp.bfloat16

EPS = 1e-6
N_MOD = 6
N_MIXERS = 2
HEADS = 8
HEAD_W = 128
PAIR_W = 2 * HEAD_W
MIX_CHUNK = 128
CM_GROUPS = 8
SCAN_BLOCK = 64
SCAN_CHUNK = 2 * SCAN_BLOCK
SUBLANES = 8
COND_ROWS = SUBLANES
SAFE_LOG2_DECAY = -115.0
HG_TOKEN_BLOCK = 512
CM_TOKEN_BLOCK = 512
VMEM_LIMIT = 60 * 1024 * 1024


def _silu(x):
    h = 0.5 * x
    return h * jnp.tanh(h) + h


def _forget_gate(z, lb):
    return 0.5 * (1.0 + lb) + (0.5 * (1.0 - lb)) * jnp.tanh(0.5 * z)


def _gelu_tanh(x):
    h = 0.5 * x
    t = jnp.tanh(x * (0.7978845608028654 + (0.7978845608028654 * 0.044715) * (x * x)))
    return h * t + h


def _rms(x, g):
    return x * lax.rsqrt(jnp.mean(x * x, axis=-1, keepdims=True) + EPS) * g


def _dot(a, b):
    return jnp.dot(a, b, preferred_element_type=F32)


def _dot_nt(a, b):
    return lax.dot_general(a, b, (((1,), (1,)), ((), ())), preferred_element_type=F32)


def _dot_tn(a, b):
    return lax.dot_general(a, b, (((0,), (0,)), ((), ())), preferred_element_type=F32)


def _ada_kernel(cond_ref, w_ref, b_ref, out_ref):
    a = _silu(cond_ref[...]).astype(BF16)
    out_ref[...] = _dot(a, w_ref[...].astype(BF16)) + b_ref[...]


def _ada_call(cond, ada_w, ada_b):
    depth, d, _ = ada_w.shape
    return pl.pallas_call(
        _ada_kernel,
        grid=(depth, N_MOD),
        in_specs=[
            pl.BlockSpec((COND_ROWS, d), lambda i, n: (0, 0)),
            pl.BlockSpec((None, d, d), lambda i, n: (i, 0, n)),
            pl.BlockSpec((None, 1, d), lambda i, n: (i, 0, n)),
        ],
        out_specs=pl.BlockSpec((None, COND_ROWS, d), lambda i, n: (i, 0, n)),
        out_shape=jax.ShapeDtypeStruct((depth, COND_ROWS, N_MOD * d), F32),
        compiler_params=pltpu.CompilerParams(dimension_semantics=("arbitrary", "arbitrary")),
        name="ada",
    )(cond, ada_w, ada_b.reshape(depth, 1, N_MOD * d))


def _scan_masks(rev):
    c = SCAN_CHUNK
    t = lax.broadcasted_iota(jnp.int32, (c, 2 * c), 0)
    s = lax.broadcasted_iota(jnp.int32, (c, 2 * c), 1) & (c - 1)
    causal = (s >= t) if rev else (s <= t)
    diag = ((t >= SCAN_BLOCK) == (s >= SCAN_BLOCK)) & causal
    s_row = lax.broadcasted_iota(jnp.int32, (1, 2 * c), 1) & (c - 1)
    near_keys = (s_row >= SCAN_BLOCK) if rev else (s_row < SCAN_BLOCK)
    return diag, near_keys


def _block_cumsum(g, rev):
    c = g.shape[0]
    t = lax.broadcasted_iota(jnp.int32, (c, 2 * c), 0)
    s = lax.broadcasted_iota(jnp.int32, (c, 2 * c), 1) & (c - 1)
    same = (t >= SCAN_BLOCK) == (s >= SCAN_BLOCK)
    tri = jnp.where(same & ((s >= t) if rev else (s <= t)), 1.0, 0.0).astype(BF16)
    g_hi = g.astype(BF16)
    g_lo = (g - g_hi.astype(F32)).astype(BF16)
    return _dot(tri, jnp.concatenate([g_hi, g_lo], axis=0))


def _scan_prep(q, k, g, v, rev):
    c, d = q.shape
    b_in = _block_cumsum(g, rev)
    b_lo, b_hi = b_in[:SCAN_BLOCK], b_in[SCAN_BLOCK:]

    def rows(lo_part, hi_part):
        return jnp.concatenate([lo_part, hi_part], axis=0)

    half = SCAN_BLOCK // 2
    if rev:
        tot_lo, tot_hi = b_lo[0:1], b_hi[0:1]
        mid_lo, mid_hi = b_lo[half:half + 1], b_hi[half:half + 1]
    else:
        tot_lo, tot_hi = b_lo[SCAN_BLOCK - 1:], b_hi[SCAN_BLOCK - 1:]
        mid_lo, mid_hi = b_lo[half - 1:half], b_hi[half - 1:half]

    qh = q * jnp.exp2(b_in)
    ke = k * jnp.exp2(rows(tot_lo - b_lo, tot_hi - b_hi))
    if rev:
        qi = rows(qh[:SCAN_BLOCK] * jnp.exp2(tot_hi), qh[SCAN_BLOCK:])
        ks = rows(ke[:SCAN_BLOCK], ke[SCAN_BLOCK:] * jnp.exp2(tot_lo))
    else:
        qi = rows(qh[:SCAN_BLOCK], qh[SCAN_BLOCK:] * jnp.exp2(tot_lo))
        ks = rows(ke[:SCAN_BLOCK] * jnp.exp2(tot_hi), ke[SCAN_BLOCK:])
    qi = qi.astype(BF16)
    ks = ks.astype(BF16)
    dec = jnp.exp2(tot_lo + tot_hi)
    qh16 = qh.astype(BF16)
    ke16 = ke.astype(BF16)
    v16 = v.astype(BF16)
    qm16 = (q * jnp.exp2(rows(b_lo - mid_lo, b_hi - mid_hi))).astype(BF16)
    km16 = (k * jnp.exp2(rows(mid_lo - b_lo, mid_hi - b_hi))).astype(BF16)

    half_sums = jnp.minimum(jnp.minimum(mid_lo, tot_lo - mid_lo), jnp.minimum(mid_hi, tot_hi - mid_hi))
    fast_vec = jnp.min(half_sums, axis=-1, keepdims=True) >= SAFE_LOG2_DECAY
    fast = jnp.min(half_sums) >= SAFE_LOG2_DECAY

    return (qm16, km16, qh16, ke16, v16, qi, ks, dec, fast_vec), fast


def _scan_finish(operands, st_ref, rev):
    qm16, km16, qh16, ke16, v16, qi, ks, dec, fast_vec = operands
    c = qi.shape[0]
    n_pairs = qi.shape[1] // PAIR_W
    diag_mask, near_keys = _scan_masks(rev)
    diag_mask = diag_mask & fast_vec
    lo, hi = slice(0, SCAN_BLOCK), slice(SCAN_BLOCK, c)
    near, far = (hi, lo) if rev else (lo, hi)
    left = lax.broadcasted_iota(jnp.int32, (1, PAIR_W), 1) < HEAD_W

    def by_head(a):
        zero = jnp.zeros_like(a)
        return jnp.concatenate([jnp.where(left, a, zero), jnp.where(left, zero, a)], axis=0)

    outs = []
    for pr in range(n_pairs):
        sl = slice(pr * PAIR_W, (pr + 1) * PAIR_W)
        attd = _dot_nt(qm16[:, sl], by_head(km16[:, sl]))
        attx = _dot_nt(qh16[far, sl], by_head(ke16[:, sl]))
        att_near = jnp.where(diag_mask[near], attd[near], 0.0)
        att_far = jnp.where(diag_mask[far], attd[far], jnp.where(near_keys, attx, 0.0))
        att = jnp.concatenate([att_far, att_near] if rev else [att_near, att_far], axis=0)
        st = st_ref[pr]
        v_p = v16[:, sl]
        o = _dot(att.astype(BF16), by_head(v_p)) + _dot_nt(qi[:, sl], by_head(st.astype(BF16)))
        outs.append(o)
        v_stack = jnp.concatenate([v_p[:, :HEAD_W], v_p[:, HEAD_W:]], axis=0)
        st_ref[pr] = st * dec[:, sl] + _dot_tn(v_stack, by_head(ks[:, sl]))
    return jnp.concatenate(outs, axis=1)


def _scan_slow_fix(q_ref, v_ref, f_ref, rows, o_ref, od_ref, slow_refs, rev):
    q_s, k_s, v_s, b_s = slow_refs
    c, d = od_ref.shape
    f = f_ref[rows, :]
    q_s[...] = q_ref[rows, :].astype(F32)
    k_s[...] = 1.0 - f
    v_s[...] = v_ref[rows, :].astype(F32)
    b_s[...] = _block_cumsum(jnp.log2(f), rev)
    row = lax.broadcasted_iota(jnp.int32, (c, 1), 0)

    def body(grp, carry):
        t0 = pl.multiple_of(grp * SUBLANES, SUBLANES)
        q8 = q_s[pl.ds(t0, SUBLANES), :]
        b8 = b_s[pl.ds(t0, SUBLANES), :]
        out_rows = []
        for r in range(SUBLANES):
            ti = t0 + r
            prod = q8[r:r + 1] * k_s[...] * jnp.exp2(jnp.minimum(b8[r:r + 1] - b_s[...], 0.0))
            valid = ((row >= SCAN_BLOCK) == (ti >= SCAN_BLOCK)) & ((row >= ti) if rev else (row <= ti))
            heads = []
            for h in range(d // HEAD_W):
                hs = slice(h * HEAD_W, (h + 1) * HEAD_W)
                w = jnp.where(valid, jnp.sum(prod[:, hs], axis=-1, keepdims=True), 0.0)
                heads.append(jnp.sum(w * v_s[:, hs], axis=0, keepdims=True))
            out_rows.append(jnp.concatenate(heads, axis=1))
        od_ref[pl.ds(t0, SUBLANES), :] = jnp.concatenate(out_rows, axis=0)
        return carry

    lax.fori_loop(0, c // SUBLANES, body, 0)
    o_ref[rows, :] += od_ref[...]


def _scan_slow_fixes(fast_ref, q_ref, v_ref, f_ref, o_ref, od_ref, slow_refs, rev):
    def body(ci, carry):
        @pl.when(fast_ref[ci] == 0)
        def _():
            rows = pl.ds(pl.multiple_of(ci * SCAN_CHUNK, SCAN_CHUNK), SCAN_CHUNK)
            _scan_slow_fix(q_ref, v_ref, f_ref, rows, o_ref, od_ref, slow_refs, rev)
        return carry

    lax.fori_loop(0, fast_ref.shape[0], body, 0)


def _state_init(st_ref, s0_ref):
    if s0_ref is None:
        st_ref[...] = jnp.zeros_like(st_ref)
    else:
        for h in range(HEADS):
            o = (h % 2) * HEAD_W
            st_ref[h // 2, :, o:o + HEAD_W] = s0_ref[h].T


def _state_write(st_ref, out_ref):
    for h in range(HEADS):
        o = (h % 2) * HEAD_W
        out_ref[h] = st_ref[h // 2, :, o:o + HEAD_W].T


def _lower_bound(lbl_ref, layer, direction):
    n_layers = lbl_ref.shape[0] // 2
    rows = [lbl_ref[2 * i + direction:2 * i + direction + 1, :] for i in range(n_layers)]
    m = functools.reduce(jnp.maximum, rows)
    ex = [jnp.exp(r - m) for r in rows]
    return sum(ex[:layer + 1]) / sum(ex)


def _chunk_rows(n_rows, rev):
    order = range(n_rows // SCAN_CHUNK)
    return [slice(ci * SCAN_CHUNK, (ci + 1) * SCAN_CHUNK) for ci in (reversed(order) if rev else order)]


def _scan_scratch(d):
    c = SCAN_CHUNK
    return [
        pltpu.VMEM((d // PAIR_W, HEAD_W, PAIR_W), F32),
        pltpu.VMEM((c, d), F32),
        pltpu.VMEM((c, d), F32), pltpu.VMEM((c, d), F32),
        pltpu.VMEM((c, d), F32), pltpu.VMEM((c, d), F32),
    ]


def _const_spec(op):
    if isinstance(op, tuple):
        arr, i = op
        return pl.BlockSpec((None,) + arr.shape[1:], lambda *_: (i,) + (0,) * (arr.ndim - 1),
                            pipeline_mode=pl.Buffered(1))
    return pl.BlockSpec(op.shape, lambda *_: (0,) * op.ndim, pipeline_mode=pl.Buffered(1))


def _const_arg(op):
    return op[0] if isinstance(op, tuple) else op


def _hg_fwd_kernel(*refs, has_s0, layer):
    if has_s0:
        x_ref, mod_ref, g_ref, w_ref, lbl_ref, s0_ref = refs[:6]
        rest = refs[6:]
    else:
        x_ref, mod_ref, g_ref, w_ref, lbl_ref = refs[:5]
        s0_ref = None
        rest = refs[5:]
    of_ref, q_ref, v_ref, fb_ref, gate_ref, sf_ref = rest[:6]
    st_ref, od_ref = rest[6:8]
    slow_refs = rest[8:12]
    ff_ref, fast_ref = rest[12:14]
    j = pl.program_id(1)
    d = x_ref.shape[-1]

    @pl.when(j == 0)
    def _():
        _state_init(st_ref, s0_ref)

    sh, sc = mod_ref[0:1, :], mod_ref[1:2, :]
    h = (_rms(x_ref[...], g_ref[...]) * (1.0 + sc) + sh).astype(BF16)
    proj = lambda i: _dot(h, w_ref[:, i * d:(i + 1) * d])
    lb_f = _lower_bound(lbl_ref, layer, 0)
    lb_b = _lower_bound(lbl_ref, layer, 1)
    q = _silu(proj(0))
    f_f = _forget_gate(proj(1), lb_f)
    zi = proj(3)
    q_ref[...] = q.astype(BF16)
    v_ref[...] = zi.astype(BF16)
    ff_ref[...] = f_f
    k_f = 1.0 - f_f
    g_f = jnp.log2(f_f)

    def backward_gate():
        fb_ref[...] = _forget_gate(proj(2), lb_b)

    def output_gate():
        gate_ref[...] = _silu(proj(4)).astype(BF16)

    fillers = [backward_gate, output_gate]
    for rows in _chunk_rows(x_ref.shape[0], rev=False):
        operands, fast = _scan_prep(q[rows], k_f[rows], g_f[rows], zi[rows], rev=False)
        fast_ref[rows.start // SCAN_CHUNK] = fast.astype(jnp.int32)
        if fillers:
            fillers.pop(0)()
        of_ref[rows, :] = _scan_finish(operands, st_ref, rev=False)
    for filler in fillers:
        filler()
    _scan_slow_fixes(fast_ref, q_ref, v_ref, ff_ref, of_ref, od_ref, slow_refs, rev=False)

    @pl.when(j == pl.num_programs(1) - 1)
    def _():
        _state_write(st_ref, sf_ref)


def _hg_fwd_call(x, mod, mod_row0, mod_stride, g, w_in, lbl, state, layer_a):
    bn, l, d = x.shape
    c = min(HG_TOKEN_BLOCK, l)
    assert l % c == 0 and c % SCAN_CHUNK == 0
    n = l // c
    tok = lambda b, j: (b, j, 0)
    in_specs = [
        pl.BlockSpec((None, c, d), tok),
        pl.BlockSpec((None, N_MOD, d), lambda b, j: (mod_row0 + mod_stride * b, 0, 0)),
        _const_spec(g), _const_spec(w_in), _const_spec(lbl),
    ]
    args = [x, mod, g, _const_arg(w_in), lbl]
    if state is not None:
        in_specs.append(pl.BlockSpec((None, None, None, HEADS, HEAD_W, HEAD_W),
                                     lambda b, j: (b, layer_a, 0, 0, 0, 0)))
        args.append(state)
    st_spec = pl.BlockSpec((None, HEADS, HEAD_W, HEAD_W), lambda b, j: (b, 0, 0, 0))
    return pl.pallas_call(
        functools.partial(_hg_fwd_kernel, has_s0=state is not None, layer=layer_a),
        grid=(bn, n),
        in_specs=in_specs,
        out_specs=[pl.BlockSpec((None, c, d), tok)] * 5 + [st_spec],
        out_shape=[
            jax.ShapeDtypeStruct((bn, l, d), F32),
            jax.ShapeDtypeStruct((bn, l, d), BF16),
            jax.ShapeDtypeStruct((bn, l, d), BF16),
            jax.ShapeDtypeStruct((bn, l, d), F32),
            jax.ShapeDtypeStruct((bn, l, d), BF16),
            jax.ShapeDtypeStruct((bn, HEADS, HEAD_W, HEAD_W), F32),
        ],
        scratch_shapes=_scan_scratch(d) + [
            pltpu.VMEM((c, d), F32),
            pltpu.SMEM((c // SCAN_CHUNK,), jnp.int32),
        ],
        compiler_params=pltpu.CompilerParams(
            dimension_semantics=("arbitrary", "arbitrary"), vmem_limit_bytes=VMEM_LIMIT),
        name="hg_fwd",
    )(*args)


def _mlp_tail(x1, mod_ref, g2, w1_ref, w2_ref):
    sh2, sc2, gt2 = mod_ref[3:4, :], mod_ref[4:5, :], mod_ref[5:6, :]
    h2 = (_rms(x1, g2) * (1.0 + sc2) + sh2).astype(BF16)
    a = jnp.maximum(_dot(h2, w1_ref[...]), 0.0)
    return x1 + gt2 * _dot((a * a).astype(BF16), w2_ref[...])


def _hg_bwd_kernel(*refs, has_s0, final, n_blocks):
    x_ref, mod_ref, gate_ref, of_ref, q_ref, v_ref, fb_ref, on_ref, g2_ref, fg_ref = refs[:10]
    wo_ref, w1_ref, w2_ref, sf_ref = refs[10:14]
    if has_s0:
        s0_ref = refs[14]
        rest = refs[15:]
    else:
        s0_ref = None
        rest = refs[14:]
    y_ref, sfb_ref = rest[:2]
    st_ref, od_ref = rest[2:4]
    slow_refs = rest[4:8]
    o_ref, fast_ref = rest[8:10]
    s = pl.program_id(0)
    total = pl.num_programs(0) - 1
    live = s < total
    j = jnp.minimum(s, total - 1) % n_blocks
    d = x_ref.shape[-1]

    @pl.when(s == 0)
    def _():
        o_ref[...] = jnp.zeros_like(o_ref)

    @pl.when(live & (j == 0))
    def _():
        _state_init(st_ref, s0_ref)

    chunk_rows = _chunk_rows(q_ref.shape[0], rev=True)
    n_slabs = 2 * len(chunk_rows)
    slab_w = w1_ref.shape[1] // n_slabs


    o = o_ref[...]
    parts = []
    for h in range(d // HEAD_W):
        oh = o[:, h * HEAD_W:(h + 1) * HEAD_W]
        parts.append(oh * lax.rsqrt(jnp.mean(oh * oh, axis=-1, keepdims=True) + EPS))
    o = jnp.concatenate(parts, axis=1) * on_ref[...] * gate_ref[...].astype(F32)
    gt1 = mod_ref[2:3, :]
    x1 = x_ref[...] + gt1 * _dot(o.astype(BF16), wo_ref[...])
    sh2, sc2, gt2 = mod_ref[3:4, :], mod_ref[4:5, :], mod_ref[5:6, :]
    h2 = (_rms(x1, g2_ref[...]) * (1.0 + sc2) + sh2).astype(BF16)

    def mlp_slab(i):
        a = jnp.maximum(_dot(h2, w1_ref[:, i * slab_w:(i + 1) * slab_w]), 0.0)
        return _dot((a * a).astype(BF16), w2_ref[i * slab_w:(i + 1) * slab_w, :])

    mlp = 0.0
    for ci, rows in enumerate(chunk_rows):
        f_b = fb_ref[rows, :]
        operands, fast = _scan_prep(q_ref[rows, :].astype(F32), 1.0 - f_b, jnp.log2(f_b),
                                    v_ref[rows, :].astype(F32), rev=True)
        fast_ref[rows.start // SCAN_CHUNK] = fast.astype(jnp.int32)
        mlp = mlp + mlp_slab(2 * ci)
        o_ref[rows, :] = of_ref[rows, :] + _scan_finish(operands, st_ref, rev=True)
        mlp = mlp + mlp_slab(2 * ci + 1)
    x2 = x1 + gt2 * mlp
    y_ref[...] = _rms(x2, fg_ref[...]) if final else x2
    _scan_slow_fixes(fast_ref, q_ref, v_ref, fb_ref, o_ref, od_ref, slow_refs, rev=True)

    @pl.when(live & (j == n_blocks - 1))
    def _():
        sfb_ref[0] = sf_ref[...]
        _state_write(st_ref, sfb_ref.at[1])


def _hg_bwd_call(x, mod, mod_row0, mod_stride, stash, s_f, onorm, g2, fg, w_out, w1, w2, state, layer_a, final):
    bn, l, d = x.shape
    c = min(HG_TOKEN_BLOCK, l)
    assert l % c == 0 and c % SCAN_CHUNK == 0
    n = l // c
    total = bn * n

    def cur(s):
        t = jnp.minimum(s, total - 1)
        return t // n, n - 1 - t % n

    def prev(s):
        t = jnp.maximum(s - 1, 0)
        return t // n, n - 1 - t % n

    tok_cur = lambda s: (*cur(s), 0)
    tok_prev = lambda s: (*prev(s), 0)
    o_f, q, v, f_b, gate = stash
    consts = [onorm, g2, fg, w_out, w1, w2]
    in_specs = [
        pl.BlockSpec((None, c, d), tok_prev),
        pl.BlockSpec((None, N_MOD, d), lambda s: (mod_row0 + mod_stride * prev(s)[0], 0, 0)),
        pl.BlockSpec((None, c, d), tok_prev),
    ] + [pl.BlockSpec((None, c, d), tok_cur)] * 4 + [_const_spec(op) for op in consts]
    in_specs.append(pl.BlockSpec((None, HEADS, HEAD_W, HEAD_W), lambda s: (cur(s)[0], 0, 0, 0)))
    args = [x, mod, gate, o_f, q, v, f_b] + [_const_arg(op) for op in consts] + [s_f]
    if state is not None:
        in_specs.append(pl.BlockSpec((None, None, None, HEADS, HEAD_W, HEAD_W),
                                     lambda s: (cur(s)[0], layer_a, 1, 0, 0, 0)))
        args.append(state)
    st_spec = pl.BlockSpec((None, 2, HEADS, HEAD_W, HEAD_W), lambda s: (cur(s)[0], 0, 0, 0, 0))
    return pl.pallas_call(
        functools.partial(_hg_bwd_kernel, has_s0=state is not None, final=final, n_blocks=n),
        grid=(total + 1,),
        in_specs=in_specs,
        out_specs=[pl.BlockSpec((None, c, d), tok_prev), st_spec],
        out_shape=[
            jax.ShapeDtypeStruct((bn, l, d), F32),
            jax.ShapeDtypeStruct((bn, 2, HEADS, HEAD_W, HEAD_W), F32),
        ],
        scratch_shapes=_scan_scratch(d) + [
            pltpu.VMEM((c, d), F32),
            pltpu.SMEM((c // SCAN_CHUNK,), jnp.int32),
        ],
        compiler_params=pltpu.CompilerParams(
            dimension_semantics=("arbitrary",), vmem_limit_bytes=VMEM_LIMIT),
        name="hg_bwd",
    )(*args)


def _cm_kernel(x_ref, mod_ref, g1_ref, g2_ref, fg_ref, win_ref, lng_ref, lnb_ref, ws_ref, bst_ref,
               wo_ref, w1_ref, w2_ref, y_ref, *, final):
    tb, d = x_ref.shape
    n_chunks = tb // MIX_CHUNK
    gw = d // CM_GROUPS
    x = x_ref[...]
    sh, sc, gt1 = mod_ref[0:1, :], mod_ref[1:2, :], mod_ref[2:3, :]
    h = (_rms(x, g1_ref[...]) * (1.0 + sc) + sh).astype(BF16)
    u = _gelu_tanh(_dot(h, win_ref[:, :d]))
    vv = _gelu_tanh(_dot(h, win_ref[:, d:]))
    mu = jnp.mean(vv, axis=-1, keepdims=True)
    vc = vv - mu
    var = jnp.mean(vc * vc, axis=-1, keepdims=True)
    vn = (vc * lax.rsqrt(var + EPS) * lng_ref[...] + lnb_ref[...]).astype(BF16)
    per_group = []
    for g in range(CM_GROUPS):
        rhs = jnp.concatenate(
            [vn[ch * MIX_CHUNK:(ch + 1) * MIX_CHUNK, g * gw:(g + 1) * gw] for ch in range(n_chunks)], axis=1)
        per_group.append(_dot(ws_ref[g], rhs) + bst_ref[:, g:g + 1])
    s = jnp.concatenate(
        [jnp.concatenate([pg[:, ch * gw:(ch + 1) * gw] for pg in per_group], axis=1) for ch in range(n_chunks)],
        axis=0)
    x1 = x + gt1 * _dot((u * s).astype(BF16), wo_ref[...])
    x2 = _mlp_tail(x1, mod_ref, g2_ref[...], w1_ref, w2_ref)
    y_ref[...] = _rms(x2, fg_ref[...]) if final else x2


def _cm_call(x, mod, mod_row0, mod_stride, g1, g2, fg, w_in, ln_g, ln_b, w_s, b_st, w_out, w1, w2, final):
    out_shape = x.shape
    if mod_stride == 0 and x.shape[1] % MIX_CHUNK == 0:
        x = x.reshape(1, -1, x.shape[-1])
    bn, l, d = x.shape
    tb = min(CM_TOKEN_BLOCK, l)
    assert l % tb == 0 and tb % MIX_CHUNK == 0
    tok = lambda b, j: (b, j, 0)
    consts = [g1, g2, fg, w_in, ln_g, ln_b, w_s, b_st, w_out, w1, w2]
    return pl.pallas_call(
        functools.partial(_cm_kernel, final=final),
        grid=(bn, l // tb),
        in_specs=[
            pl.BlockSpec((None, tb, d), tok),
            pl.BlockSpec((None, N_MOD, d), lambda b, j: (mod_row0 + mod_stride * b, 0, 0)),
        ] + [_const_spec(op) for op in consts],
        out_specs=pl.BlockSpec((None, tb, d), tok),
        out_shape=jax.ShapeDtypeStruct((bn, l, d), F32),
        compiler_params=pltpu.CompilerParams(
            dimension_semantics=("arbitrary", "arbitrary"), vmem_limit_bytes=VMEM_LIMIT),
        name="cm",
    )(x, mod, *[_const_arg(op) for op in consts]).reshape(out_shape)


def kernel(x_prompt, x_sample, state_hgrn, c, c_ctx, ada_w, ada_b, norm_mix_g, norm_mlp_g, mlp_w1, mlp_w2,
           hgrn_w_in, hgrn_lb_logits, hgrn_onorm_g, hgrn_w_out, cm_w_in, cm_ln_g, cm_ln_b, cm_w_s, cm_b_s,
           cm_w_out, final_norm_g):
    depth, d, _ = ada_w.shape
    n_lat = c.shape[0]
    assert d % PAIR_W == 0 and d // HEAD_W == HEADS and 1 + n_lat <= COND_ROWS
    assert x_prompt.shape[1] % SCAN_CHUNK == 0 and x_sample.shape[1] % SCAN_CHUNK == 0

    cond = jnp.concatenate([c_ctx[None, :], c, jnp.zeros((COND_ROWS - 1 - n_lat, d), F32)], axis=0)
    mods = _ada_call(cond, ada_w, ada_b).reshape(depth, COND_ROWS, N_MOD, d)

    row = lambda a: a.reshape(1, d)
    lbl = hgrn_lb_logits.reshape(-1, d)
    fg = row(final_norm_g)
    mlp_w1, mlp_w2, hgrn_w_in, hgrn_w_out, cm_w_in, cm_w_s, cm_w_out = (
        a.astype(BF16) for a in (mlp_w1, mlp_w2, hgrn_w_in, hgrn_w_out, cm_w_in, cm_w_s, cm_w_out))
    ctx, lat = x_prompt, x_sample
    new_states = []
    for i in range(depth):
        j = i // N_MIXERS
        final = i == depth - 1
        w1 = (mlp_w1, i)
        w2 = (mlp_w2, i)
        g1, g2 = row(norm_mix_g[i]), row(norm_mlp_g[i])
        if i % N_MIXERS == 0:
            w_in = (hgrn_w_in, j)
            w_out = (hgrn_w_out, j)
            onorm = row(hgrn_onorm_g[j])
            *stash, s_f = _hg_fwd_call(ctx, mods[i], 0, 0, g1, w_in, lbl, None, j)
            ctx, s_fb = _hg_bwd_call(ctx, mods[i], 0, 0, stash, s_f, onorm, g2, fg, w_out, w1, w2, None, j, final)
            new_states.append(s_fb)
            *stash, s_f = _hg_fwd_call(lat, mods[i], 1, 1, g1, w_in, lbl, state_hgrn, j)
            lat, _ = _hg_bwd_call(lat, mods[i], 1, 1, stash, s_f, onorm, g2, fg, w_out, w1, w2, state_hgrn, j, final)
        else:
            cm_args = (g1, g2, fg, (cm_w_in, j), row(cm_ln_g[j]), row(cm_ln_b[j]),
                       (cm_w_s, j), cm_b_s[j].T, (cm_w_out, j), w1, w2, final)
            ctx = _cm_call(ctx, mods[i], 0, 0, *cm_args)
            lat = _cm_call(lat, mods[i], 1, 1, *cm_args)
    new_state = jnp.stack(new_states, axis=1).astype(x_prompt.dtype)
    return ctx, lat, new_state
```

```python
import functools

import jax
import jax.numpy as jnp
from jax import lax
from jax.experimental import pallas as pl
from jax.experimental.pallas import tpu as pltpu

F32 = jnp.float32
BF16 = jnp.bfloat16

EPS = 1e-6
N_MOD = 6
N_MIXERS = 2
HEADS = 8
HEAD_W = 128
PAIR_W = 2 * HEAD_W
MIX_CHUNK = 128
CM_GROUPS = 8
SCAN_BLOCK = 64
SCAN_CHUNK = 2 * SCAN_BLOCK
SUBLANES = 8
COND_ROWS = SUBLANES
SAFE_LOG2_DECAY = -115.0
HG_TOKEN_BLOCK = 512
CM_TOKEN_BLOCK = 512
VMEM_LIMIT = 60 * 1024 * 1024


def _silu(x):
    h = 0.5 * x
    return h * jnp.tanh(h) + h


def _forget_gate(z, lb):
    return 0.5 * (1.0 + lb) + (0.5 * (1.0 - lb)) * jnp.tanh(0.5 * z)


def _gelu_tanh(x):
    h = 0.5 * x
    t = jnp.tanh(x * (0.7978845608028654 + (0.7978845608028654 * 0.044715) * (x * x)))
    return h * t + h


def _rms(x, g):
    return x * lax.rsqrt(jnp.mean(x * x, axis=-1, keepdims=True) + EPS) * g


def _dot(a, b):
    return jnp.dot(a, b, preferred_element_type=F32)


def _dot_nt(a, b):
    return lax.dot_general(a, b, (((1,), (1,)), ((), ())), preferred_element_type=F32)


def _dot_tn(a, b):
    return lax.dot_general(a, b, (((0,), (0,)), ((), ())), preferred_element_type=F32)


def _ada_kernel(cond_ref, w_ref, b_ref, out_ref):
    a = _silu(cond_ref[...]).astype(BF16)
    out_ref[...] = _dot(a, w_ref[...].astype(BF16)) + b_ref[...]


def _ada_call(cond, ada_w, ada_b):
    depth, d, _ = ada_w.shape
    return pl.pallas_call(
        _ada_kernel,
        grid=(depth, N_MOD),
        in_specs=[
            pl.BlockSpec((COND_ROWS, d), lambda i, n: (0, 0)),
            pl.BlockSpec((None, d, d), lambda i, n: (i, 0, n)),
            pl.BlockSpec((None, 1, d), lambda i, n: (i, 0, n)),
        ],
        out_specs=pl.BlockSpec((None, COND_ROWS, d), lambda i, n: (i, 0, n)),
        out_shape=jax.ShapeDtypeStruct((depth, COND_ROWS, N_MOD * d), F32),
        compiler_params=pltpu.CompilerParams(dimension_semantics=("arbitrary", "arbitrary")),
        name="ada",
    )(cond, ada_w, ada_b.reshape(depth, 1, N_MOD * d))


def _scan_masks(rev):
    c = SCAN_CHUNK
    t = lax.broadcasted_iota(jnp.int32, (c, 2 * c), 0)
    s = lax.broadcasted_iota(jnp.int32, (c, 2 * c), 1) & (c - 1)
    causal = (s >= t) if rev else (s <= t)
    diag = ((t >= SCAN_BLOCK) == (s >= SCAN_BLOCK)) & causal
    s_row = lax.broadcasted_iota(jnp.int32, (1, 2 * c), 1) & (c - 1)
    near_keys = (s_row >= SCAN_BLOCK) if rev else (s_row < SCAN_BLOCK)
    return diag, near_keys


def _block_cumsum(g, rev):
    c = g.shape[0]
    t = lax.broadcasted_iota(jnp.int32, (c, 2 * c), 0)
    s = lax.broadcasted_iota(jnp.int32, (c, 2 * c), 1) & (c - 1)
    same = (t >= SCAN_BLOCK) == (s >= SCAN_BLOCK)
    tri = jnp.where(same & ((s >= t) if rev else (s <= t)), 1.0, 0.0).astype(BF16)
    g_hi = g.astype(BF16)
    g_lo = (g - g_hi.astype(F32)).astype(BF16)
    return _dot(tri, jnp.concatenate([g_hi, g_lo], axis=0))


def _scan_prep(q, k, g, rev):
    c, d = q.shape
    b_in = _block_cumsum(g, rev)
    b_lo, b_hi = b_in[:SCAN_BLOCK], b_in[SCAN_BLOCK:]

    def rows(lo_part, hi_part):
        return jnp.concatenate([lo_part, hi_part], axis=0)

    half = SCAN_BLOCK // 2
    if rev:
        tot_lo, tot_hi = b_lo[0:1], b_hi[0:1]
        mid_lo, mid_hi = b_lo[half:half + 1], b_hi[half:half + 1]
    else:
        tot_lo, tot_hi = b_lo[SCAN_BLOCK - 1:], b_hi[SCAN_BLOCK - 1:]
        mid_lo, mid_hi = b_lo[half - 1:half], b_hi[half - 1:half]

    qh = q * jnp.exp2(b_in)
    ke = k * jnp.exp2(rows(tot_lo - b_lo, tot_hi - b_hi))
    if rev:
        qi = rows(qh[:SCAN_BLOCK] * jnp.exp2(tot_hi), qh[SCAN_BLOCK:])
        ks = rows(ke[:SCAN_BLOCK], ke[SCAN_BLOCK:] * jnp.exp2(tot_lo))
    else:
        qi = rows(qh[:SCAN_BLOCK], qh[SCAN_BLOCK:] * jnp.exp2(tot_lo))
        ks = rows(ke[:SCAN_BLOCK] * jnp.exp2(tot_hi), ke[SCAN_BLOCK:])
    qi = qi.astype(BF16)
    ks = ks.astype(BF16)
    dec = jnp.exp2(tot_lo + tot_hi)
    qh16 = qh.astype(BF16)
    ke16 = ke.astype(BF16)
    qm16 = (q * jnp.exp2(rows(b_lo - mid_lo, b_hi - mid_hi))).astype(BF16)
    km16 = (k * jnp.exp2(rows(mid_lo - b_lo, mid_hi - b_hi))).astype(BF16)

    half_sums = jnp.minimum(jnp.minimum(mid_lo, tot_lo - mid_lo), jnp.minimum(mid_hi, tot_hi - mid_hi))
    fast_vec = jnp.min(half_sums, axis=-1, keepdims=True) >= SAFE_LOG2_DECAY
    fast = jnp.min(half_sums) >= SAFE_LOG2_DECAY

    return (qm16, km16, qh16, ke16, qi, ks, dec, fast_vec), fast


def _scan_finish(operands, v16, st_ref, rev):
    qm16, km16, qh16, ke16, qi, ks, dec, fast_vec = operands
    c = qi.shape[0]
    n_pairs = qi.shape[1] // PAIR_W
    diag_mask, near_keys = _scan_masks(rev)
    diag_mask = diag_mask & fast_vec
    lo, hi = slice(0, SCAN_BLOCK), slice(SCAN_BLOCK, c)
    near, far = (hi, lo) if rev else (lo, hi)
    left = lax.broadcasted_iota(jnp.int32, (1, PAIR_W), 1) < HEAD_W

    def by_head(a):
        zero = jnp.zeros_like(a)
        return jnp.concatenate([jnp.where(left, a, zero), jnp.where(left, zero, a)], axis=0)

    outs = []
    for pr in range(n_pairs):
        sl = slice(pr * PAIR_W, (pr + 1) * PAIR_W)
        attd = _dot_nt(qm16[:, sl], by_head(km16[:, sl]))
        attx = _dot_nt(qh16[far, sl], by_head(ke16[:, sl]))
        att_near = jnp.where(diag_mask[near], attd[near], 0.0)
        att_far = jnp.where(diag_mask[far], attd[far], jnp.where(near_keys, attx, 0.0))
        att = jnp.concatenate([att_far, att_near] if rev else [att_near, att_far], axis=0)
        st = st_ref[pr]
        v_p = v16[:, sl]
        o = _dot(att.astype(BF16), by_head(v_p)) + _dot_nt(qi[:, sl], by_head(st.astype(BF16)))
        outs.append(o)
        v_stack = jnp.concatenate([v_p[:, :HEAD_W], v_p[:, HEAD_W:]], axis=0)
        st_ref[pr] = st * dec[:, sl] + _dot_tn(v_stack, by_head(ks[:, sl]))
    return jnp.concatenate(outs, axis=1)


def _scan_slow_fix(q_ref, v_ref, f_ref, rows, o_ref, od_ref, slow_refs, rev):
    q_s, k_s, v_s, b_s = slow_refs
    c, d = od_ref.shape
    f = f_ref[rows, :]
    q_s[...] = q_ref[rows, :].astype(F32)
    k_s[...] = 1.0 - f
    v_s[...] = v_ref[rows, :].astype(F32)
    b_s[...] = _block_cumsum(jnp.log2(f), rev)
    row = lax.broadcasted_iota(jnp.int32, (c, 1), 0)

    def body(grp, carry):
        t0 = pl.multiple_of(grp * SUBLANES, SUBLANES)
        q8 = q_s[pl.ds(t0, SUBLANES), :]
        b8 = b_s[pl.ds(t0, SUBLANES), :]
        out_rows = []
        for r in range(SUBLANES):
            ti = t0 + r
            prod = q8[r:r + 1] * k_s[...] * jnp.exp2(jnp.minimum(b8[r:r + 1] - b_s[...], 0.0))
            valid = ((row >= SCAN_BLOCK) == (ti >= SCAN_BLOCK)) & ((row >= ti) if rev else (row <= ti))
            heads = []
            for h in range(d // HEAD_W):
                hs = slice(h * HEAD_W, (h + 1) * HEAD_W)
                w = jnp.where(valid, jnp.sum(prod[:, hs], axis=-1, keepdims=True), 0.0)
                heads.append(jnp.sum(w * v_s[:, hs], axis=0, keepdims=True))
            out_rows.append(jnp.concatenate(heads, axis=1))
        od_ref[pl.ds(t0, SUBLANES), :] = jnp.concatenate(out_rows, axis=0)
        return carry

    lax.fori_loop(0, c // SUBLANES, body, 0)
    o_ref[rows, :] += od_ref[...]


def _scan_slow_fixes(fast_ref, q_ref, v_ref, f_ref, o_ref, od_ref, slow_refs, rev):
    def body(ci, carry):
        @pl.when(fast_ref[ci] == 0)
        def _():
            rows = pl.ds(pl.multiple_of(ci * SCAN_CHUNK, SCAN_CHUNK), SCAN_CHUNK)
            _scan_slow_fix(q_ref, v_ref, f_ref, rows, o_ref, od_ref, slow_refs, rev)
        return carry

    lax.fori_loop(0, fast_ref.shape[0], body, 0)


def _state_init(st_ref, s0_ref):
    if s0_ref is None:
        st_ref[...] = jnp.zeros_like(st_ref)
    else:
        for h in range(HEADS):
            o = (h % 2) * HEAD_W
            st_ref[h // 2, :, o:o + HEAD_W] = s0_ref[h].T


def _state_write(st_ref, out_ref):
    for h in range(HEADS):
        o = (h % 2) * HEAD_W
        out_ref[h] = st_ref[h // 2, :, o:o + HEAD_W].T


def _lower_bound(lbl_ref, layer, direction):
    n_layers = lbl_ref.shape[0] // 2
    rows = [lbl_ref[2 * i + direction:2 * i + direction + 1, :] for i in range(n_layers)]
    m = functools.reduce(jnp.maximum, rows)
    ex = [jnp.exp(r - m) for r in rows]
    return sum(ex[:layer + 1]) / sum(ex)


def _chunk_rows(n_rows, rev):
    order = range(n_rows // SCAN_CHUNK)
    return [slice(ci * SCAN_CHUNK, (ci + 1) * SCAN_CHUNK) for ci in (reversed(order) if rev else order)]


def _scan_scratch(d):
    c = SCAN_CHUNK
    return [
        pltpu.VMEM((d // PAIR_W, HEAD_W, PAIR_W), F32),
        pltpu.VMEM((c, d), F32),
        pltpu.VMEM((c, d), F32), pltpu.VMEM((c, d), F32),
        pltpu.VMEM((c, d), F32), pltpu.VMEM((c, d), F32),
    ]


def _const_spec(op):
    if isinstance(op, tuple):
        arr, i = op
        return pl.BlockSpec((None,) + arr.shape[1:], lambda *_: (i,) + (0,) * (arr.ndim - 1),
                            pipeline_mode=pl.Buffered(1))
    return pl.BlockSpec(op.shape, lambda *_: (0,) * op.ndim, pipeline_mode=pl.Buffered(1))


def _const_arg(op):
    return op[0] if isinstance(op, tuple) else op


def _hg_fwd_kernel(*refs, has_s0, layer):
    if has_s0:
        x_ref, mod_ref, g_ref, w_ref, lbl_ref, s0_ref = refs[:6]
        rest = refs[6:]
    else:
        x_ref, mod_ref, g_ref, w_ref, lbl_ref = refs[:5]
        s0_ref = None
        rest = refs[5:]
    of_ref, q_ref, v_ref, fb_ref, gate_ref, sf_ref = rest[:6]
    st_ref, od_ref = rest[6:8]
    slow_refs = rest[8:12]
    ff_ref, fast_ref = rest[12:14]
    j = pl.program_id(1)
    d = x_ref.shape[-1]

    @pl.when(j == 0)
    def _():
        _state_init(st_ref, s0_ref)

    sh, sc = mod_ref[0:1, :], mod_ref[1:2, :]
    h = (_rms(x_ref[...], g_ref[...]) * (1.0 + sc) + sh).astype(BF16)
    proj = lambda i: _dot(h, w_ref[:, i * d:(i + 1) * d])
    lb_f = _lower_bound(lbl_ref, layer, 0)
    lb_b = _lower_bound(lbl_ref, layer, 1)
    q = _silu(proj(0))
    f_f = _forget_gate(proj(1), lb_f)
    q_ref[...] = q.astype(BF16)
    v_ref[...] = proj(3).astype(BF16)
    ff_ref[...] = f_f
    k_f = 1.0 - f_f
    g_f = jnp.log2(f_f)

    chunks = _chunk_rows(x_ref.shape[0], rev=False)
    n_pieces = max(1, len(chunks) // 2)
    piece_w = d // n_pieces

    def backward_gate(p):
        cols = slice(p * piece_w, (p + 1) * piece_w)
        z = _dot(h, w_ref[:, 2 * d + p * piece_w:2 * d + (p + 1) * piece_w])
        fb_ref[:, cols] = _forget_gate(z, lb_b[:, cols])

    def output_gate(p):
        cols = slice(p * piece_w, (p + 1) * piece_w)
        z = _dot(h, w_ref[:, 4 * d + p * piece_w:4 * d + (p + 1) * piece_w])
        gate_ref[:, cols] = _silu(z).astype(BF16)

    fillers = [functools.partial(f, p) for f in (backward_gate, output_gate) for p in range(n_pieces)]
    for rows in chunks:
        operands, fast = _scan_prep(q[rows], k_f[rows], g_f[rows], rev=False)
        fast_ref[rows.start // SCAN_CHUNK] = fast.astype(jnp.int32)
        if fillers:
            fillers.pop(0)()
        of_ref[rows, :] = _scan_finish(operands, v_ref[rows, :], st_ref, rev=False)
    for filler in fillers:
        filler()
    _scan_slow_fixes(fast_ref, q_ref, v_ref, ff_ref, of_ref, od_ref, slow_refs, rev=False)

    @pl.when(j == pl.num_programs(1) - 1)
    def _():
        _state_write(st_ref, sf_ref)


def _hg_fwd_call(x, mod, mod_row0, mod_stride, g, w_in, lbl, state, layer_a):
    bn, l, d = x.shape
    c = min(HG_TOKEN_BLOCK, l)
    assert l % c == 0 and c % SCAN_CHUNK == 0
    n = l // c
    tok = lambda b, j: (b, j, 0)
    in_specs = [
        pl.BlockSpec((None, c, d), tok),
        pl.BlockSpec((None, N_MOD, d), lambda b, j: (mod_row0 + mod_stride * b, 0, 0)),
        _const_spec(g), _const_spec(w_in), _const_spec(lbl),
    ]
    args = [x, mod, g, _const_arg(w_in), lbl]
    if state is not None:
        in_specs.append(pl.BlockSpec((None, None, None, HEADS, HEAD_W, HEAD_W),
                                     lambda b, j: (b, layer_a, 0, 0, 0, 0)))
        args.append(state)
    st_spec = pl.BlockSpec((None, HEADS, HEAD_W, HEAD_W), lambda b, j: (b, 0, 0, 0))
    return pl.pallas_call(
        functools.partial(_hg_fwd_kernel, has_s0=state is not None, layer=layer_a),
        grid=(bn, n),
        in_specs=in_specs,
        out_specs=[pl.BlockSpec((None, c, d), tok)] * 5 + [st_spec],
        out_shape=[
            jax.ShapeDtypeStruct((bn, l, d), F32),
            jax.ShapeDtypeStruct((bn, l, d), BF16),
            jax.ShapeDtypeStruct((bn, l, d), BF16),
            jax.ShapeDtypeStruct((bn, l, d), F32),
            jax.ShapeDtypeStruct((bn, l, d), BF16),
            jax.ShapeDtypeStruct((bn, HEADS, HEAD_W, HEAD_W), F32),
        ],
        scratch_shapes=_scan_scratch(d) + [
            pltpu.VMEM((c, d), F32),
            pltpu.SMEM((c // SCAN_CHUNK,), jnp.int32),
        ],
        compiler_params=pltpu.CompilerParams(
            dimension_semantics=("arbitrary", "arbitrary"), vmem_limit_bytes=VMEM_LIMIT),
        name="hg_fwd",
    )(*args)


def _mlp_tail(x1, mod_ref, g2, w1_ref, w2_ref):
    sh2, sc2, gt2 = mod_ref[3:4, :], mod_ref[4:5, :], mod_ref[5:6, :]
    h2 = (_rms(x1, g2) * (1.0 + sc2) + sh2).astype(BF16)
    a = jnp.maximum(_dot(h2, w1_ref[...]), 0.0)
    return x1 + gt2 * _dot((a * a).astype(BF16), w2_ref[...])


def _hg_bwd_kernel(*refs, has_s0, final, n_blocks):
    x_ref, mod_ref, gate_ref, of_ref, q_ref, v_ref, fb_ref, on_ref, g2_ref, fg_ref = refs[:10]
    wo_ref, w1_ref, w2_ref, sf_ref = refs[10:14]
    if has_s0:
        s0_ref = refs[14]
        rest = refs[15:]
    else:
        s0_ref = None
        rest = refs[14:]
    y_ref, sfb_ref = rest[:2]
    st_ref, od_ref = rest[2:4]
    slow_refs = rest[4:8]
    o_ref, fast_ref = rest[8:10]
    s = pl.program_id(0)
    total = pl.num_programs(0) - 1
    live = s < total
    j = jnp.minimum(s, total - 1) % n_blocks
    d = x_ref.shape[-1]

    @pl.when(s == 0)
    def _():
        o_ref[...] = jnp.zeros_like(o_ref)

    @pl.when(live & (j == 0))
    def _():
        _state_init(st_ref, s0_ref)

    chunk_rows = _chunk_rows(q_ref.shape[0], rev=True)
    n_slabs = len(chunk_rows)
    slab_w = w1_ref.shape[1] // n_slabs


    o = o_ref[...]
    parts = []
    for h in range(d // HEAD_W):
        oh = o[:, h * HEAD_W:(h + 1) * HEAD_W]
        parts.append(oh * lax.rsqrt(jnp.mean(oh * oh, axis=-1, keepdims=True) + EPS))
    o = jnp.concatenate(parts, axis=1) * on_ref[...] * gate_ref[...].astype(F32)
    gt1 = mod_ref[2:3, :]
    x1 = x_ref[...] + gt1 * _dot(o.astype(BF16), wo_ref[...])
    sh2, sc2, gt2 = mod_ref[3:4, :], mod_ref[4:5, :], mod_ref[5:6, :]
    h2 = (_rms(x1, g2_ref[...]) * (1.0 + sc2) + sh2).astype(BF16)

    def mlp_slab(i):
        a = jnp.maximum(_dot(h2, w1_ref[:, i * slab_w:(i + 1) * slab_w]), 0.0)
        return _dot((a * a).astype(BF16), w2_ref[i * slab_w:(i + 1) * slab_w, :])

    mlp = 0.0
    for ci, rows in enumerate(chunk_rows):
        f_b = fb_ref[rows, :]
        operands, fast = _scan_prep(q_ref[rows, :].astype(F32), 1.0 - f_b, jnp.log2(f_b), rev=True)
        fast_ref[rows.start // SCAN_CHUNK] = fast.astype(jnp.int32)
        mlp = mlp + mlp_slab(ci)
        o_ref[rows, :] = of_ref[rows, :] + _scan_finish(operands, v_ref[rows, :], st_ref, rev=True)
    x2 = x1 + gt2 * mlp
    y_ref[...] = _rms(x2, fg_ref[...]) if final else x2
    _scan_slow_fixes(fast_ref, q_ref, v_ref, fb_ref, o_ref, od_ref, slow_refs, rev=True)

    @pl.when(live & (j == n_blocks - 1))
    def _():
        sfb_ref[0] = sf_ref[...]
        _state_write(st_ref, sfb_ref.at[1])


def _hg_bwd_call(x, mod, mod_row0, mod_stride, stash, s_f, onorm, g2, fg, w_out, w1, w2, state, layer_a, final):
    bn, l, d = x.shape
    c = min(HG_TOKEN_BLOCK, l)
    assert l % c == 0 and c % SCAN_CHUNK == 0
    n = l // c
    total = bn * n

    def cur(s):
        t = jnp.minimum(s, total - 1)
        return t // n, n - 1 - t % n

    def prev(s):
        t = jnp.maximum(s - 1, 0)
        return t // n, n - 1 - t % n

    tok_cur = lambda s: (*cur(s), 0)
    tok_prev = lambda s: (*prev(s), 0)
    o_f, q, v, f_b, gate = stash
    consts = [onorm, g2, fg, w_out, w1, w2]
    in_specs = [
        pl.BlockSpec((None, c, d), tok_prev),
        pl.BlockSpec((None, N_MOD, d), lambda s: (mod_row0 + mod_stride * prev(s)[0], 0, 0)),
        pl.BlockSpec((None, c, d), tok_prev),
    ] + [pl.BlockSpec((None, c, d), tok_cur)] * 4 + [_const_spec(op) for op in consts]
    in_specs.append(pl.BlockSpec((None, HEADS, HEAD_W, HEAD_W), lambda s: (cur(s)[0], 0, 0, 0)))
    args = [x, mod, gate, o_f, q, v, f_b] + [_const_arg(op) for op in consts] + [s_f]
    if state is not None:
        in_specs.append(pl.BlockSpec((None, None, None, HEADS, HEAD_W, HEAD_W),
                                     lambda s: (cur(s)[0], layer_a, 1, 0, 0, 0)))
        args.append(state)
    st_spec = pl.BlockSpec((None, 2, HEADS, HEAD_W, HEAD_W), lambda s: (cur(s)[0], 0, 0, 0, 0))
    return pl.pallas_call(
        functools.partial(_hg_bwd_kernel, has_s0=state is not None, final=final, n_blocks=n),
        grid=(total + 1,),
        in_specs=in_specs,
        out_specs=[pl.BlockSpec((None, c, d), tok_prev), st_spec],
        out_shape=[
            jax.ShapeDtypeStruct((bn, l, d), F32),
            jax.ShapeDtypeStruct((bn, 2, HEADS, HEAD_W, HEAD_W), F32),
        ],
        scratch_shapes=_scan_scratch(d) + [
            pltpu.VMEM((c, d), F32),
            pltpu.SMEM((c // SCAN_CHUNK,), jnp.int32),
        ],
        compiler_params=pltpu.CompilerParams(
            dimension_semantics=("arbitrary",), vmem_limit_bytes=VMEM_LIMIT),
        name="hg_bwd",
    )(*args)


def _cm_kernel(x_ref, mod_ref, g1_ref, g2_ref, fg_ref, win_ref, lng_ref, lnb_ref, ws_ref, bst_ref,
               wo_ref, w1_ref, w2_ref, y_ref, *, final):
    tb, d = x_ref.shape
    n_chunks = tb // MIX_CHUNK
    gw = d // CM_GROUPS
    x = x_ref[...]
    sh, sc, gt1 = mod_ref[0:1, :], mod_ref[1:2, :], mod_ref[2:3, :]
    h = (_rms(x, g1_ref[...]) * (1.0 + sc) + sh).astype(BF16)
    u = _gelu_tanh(_dot(h, win_ref[:, :d]))
    vv = _gelu_tanh(_dot(h, win_ref[:, d:]))
    mu = jnp.mean(vv, axis=-1, keepdims=True)
    vc = vv - mu
    var = jnp.mean(vc * vc, axis=-1, keepdims=True)
    vn = (vc * lax.rsqrt(var + EPS) * lng_ref[...] + lnb_ref[...]).astype(BF16)
    per_group = []
    for g in range(CM_GROUPS):
        rhs = jnp.concatenate(
            [vn[ch * MIX_CHUNK:(ch + 1) * MIX_CHUNK, g * gw:(g + 1) * gw] for ch in range(n_chunks)], axis=1)
        per_group.append(_dot(ws_ref[g], rhs) + bst_ref[:, g:g + 1])
    s = jnp.concatenate(
        [jnp.concatenate([pg[:, ch * gw:(ch + 1) * gw] for pg in per_group], axis=1) for ch in range(n_chunks)],
        axis=0)
    x1 = x + gt1 * _dot((u * s).astype(BF16), wo_ref[...])
    x2 = _mlp_tail(x1, mod_ref, g2_ref[...], w1_ref, w2_ref)
    y_ref[...] = _rms(x2, fg_ref[...]) if final else x2


def _cm_call(x, mod, mod_row0, mod_stride, g1, g2, fg, w_in, ln_g, ln_b, w_s, b_st, w_out, w1, w2, final):
    out_shape = x.shape
    if mod_stride == 0 and x.shape[1] % MIX_CHUNK == 0:
        x = x.reshape(1, -1, x.shape[-1])
    bn, l, d = x.shape
    tb = min(CM_TOKEN_BLOCK, l)
    assert l % tb == 0 and tb % MIX_CHUNK == 0
    tok = lambda b, j: (b, j, 0)
    consts = [g1, g2, fg, w_in, ln_g, ln_b, w_s, b_st, w_out, w1, w2]
    return pl.pallas_call(
        functools.partial(_cm_kernel, final=final),
        grid=(bn, l // tb),
        in_specs=[
            pl.BlockSpec((None, tb, d), tok),
            pl.BlockSpec((None, N_MOD, d), lambda b, j: (mod_row0 + mod_stride * b, 0, 0)),
        ] + [_const_spec(op) for op in consts],
        out_specs=pl.BlockSpec((None, tb, d), tok),
        out_shape=jax.ShapeDtypeStruct((bn, l, d), F32),
        compiler_params=pltpu.CompilerParams(
            dimension_semantics=("arbitrary", "arbitrary"), vmem_limit_bytes=VMEM_LIMIT),
        name="cm",
    )(x, mod, *[_const_arg(op) for op in consts]).reshape(out_shape)


def kernel(x_prompt, x_sample, state_hgrn, c, c_ctx, ada_w, ada_b, norm_mix_g, norm_mlp_g, mlp_w1, mlp_w2,
           hgrn_w_in, hgrn_lb_logits, hgrn_onorm_g, hgrn_w_out, cm_w_in, cm_ln_g, cm_ln_b, cm_w_s, cm_b_s,
           cm_w_out, final_norm_g):
    depth, d, _ = ada_w.shape
    n_lat = c.shape[0]
    assert d % PAIR_W == 0 and d // HEAD_W == HEADS and 1 + n_lat <= COND_ROWS
    assert x_prompt.shape[1] % SCAN_CHUNK == 0 and x_sample.shape[1] % SCAN_CHUNK == 0

    cond = jnp.concatenate([c_ctx[None, :], c, jnp.zeros((COND_ROWS - 1 - n_lat, d), F32)], axis=0)
    mods = _ada_call(cond, ada_w, ada_b).reshape(depth, COND_ROWS, N_MOD, d)

    row = lambda a: a.reshape(1, d)
    lbl = hgrn_lb_logits.reshape(-1, d)
    fg = row(final_norm_g)
    mlp_w1, mlp_w2, hgrn_w_in, hgrn_w_out, cm_w_in, cm_w_s, cm_w_out = (
        a.astype(BF16) for a in (mlp_w1, mlp_w2, hgrn_w_in, hgrn_w_out, cm_w_in, cm_w_s, cm_w_out))
    ctx, lat = x_prompt, x_sample
    new_states = []
    for i in range(depth):
        j = i // N_MIXERS
        final = i == depth - 1
        w1 = (mlp_w1, i)
        w2 = (mlp_w2, i)
        g1, g2 = row(norm_mix_g[i]), row(norm_mlp_g[i])
        if i % N_MIXERS == 0:
            w_in = (hgrn_w_in, j)
            w_out = (hgrn_w_out, j)
            onorm = row(hgrn_onorm_g[j])
            *stash, s_f = _hg_fwd_call(ctx, mods[i], 0, 0, g1, w_in, lbl, None, j)
            ctx, s_fb = _hg_bwd_call(ctx, mods[i], 0, 0, stash, s_f, onorm, g2, fg, w_out, w1, w2, None, j, final)
            new_states.append(s_fb)
            *stash, s_f = _hg_fwd_call(lat, mods[i], 1, 1, g1, w_in, lbl, state_hgrn, j)
            lat, _ = _hg_bwd_call(lat, mods[i], 1, 1, stash, s_f, onorm, g2, fg, w_out, w1, w2, state_hgrn, j, final)
        else:
            cm_args = (g1, g2, fg, (cm_w_in, j), row(cm_ln_g[j]), row(cm_ln_b[j]),
                       (cm_w_s, j), cm_b_s[j].T, (cm_w_out, j), w1, w2, final)
            ctx = _cm_call(ctx, mods[i], 0, 0, *cm_args)
            lat = _cm_call(lat, mods[i], 1, 1, *cm_args)
    new_state = jnp.stack(new_states, axis=1).astype(x_prompt.dtype)
    return ctx, lat, new_state
```

```python
import functools

import jax
import jax.numpy as jnp
from jax import lax
from jax.experimental import pallas as pl
from jax.experimental.pallas import tpu as pltpu

F32 = jnp.float32
BF16 = jnp.bfloat16

EPS = 1e-6
N_MOD = 6
N_MIXERS = 2
HEADS = 8
HEAD_W = 128
PAIR_W = 2 * HEAD_W
MIX_CHUNK = 128
CM_GROUPS = 8
SCAN_BLOCK = 64
SCAN_CHUNK = 2 * SCAN_BLOCK
SUBLANES = 8
COND_ROWS = SUBLANES
SAFE_LOG2_DECAY = -115.0
HG_TOKEN_BLOCK = 512
CM_TOKEN_BLOCK = 512
VMEM_LIMIT = 60 * 1024 * 1024


def _silu(x):
    h = 0.5 * x
    return h * jnp.tanh(h) + h


def _forget_gate(z, lb):
    return 0.5 * (1.0 + lb) + (0.5 * (1.0 - lb)) * jnp.tanh(0.5 * z)


def _gelu_tanh(x):
    h = 0.5 * x
    t = jnp.tanh(x * (0.7978845608028654 + (0.7978845608028654 * 0.044715) * (x * x)))
    return h * t + h


def _rms(x, g):
    return x * lax.rsqrt(jnp.mean(x * x, axis=-1, keepdims=True) + EPS) * g


def _dot(a, b):
    return jnp.dot(a, b, preferred_element_type=F32)


def _dot_nt(a, b):
    return lax.dot_general(a, b, (((1,), (1,)), ((), ())), preferred_element_type=F32)


def _dot_tn(a, b):
    return lax.dot_general(a, b, (((0,), (0,)), ((), ())), preferred_element_type=F32)


def _ada_kernel(cond_ref, w_ref, b_ref, out_ref):
    a = _silu(cond_ref[...]).astype(BF16)
    out_ref[...] = _dot(a, w_ref[...].astype(BF16)) + b_ref[...]


def _ada_call(cond, ada_w, ada_b):
    depth, d, _ = ada_w.shape
    return pl.pallas_call(
        _ada_kernel,
        grid=(depth, N_MOD),
        in_specs=[
            pl.BlockSpec((COND_ROWS, d), lambda i, n: (0, 0)),
            pl.BlockSpec((None, d, d), lambda i, n: (i, 0, n)),
            pl.BlockSpec((None, 1, d), lambda i, n: (i, 0, n)),
        ],
        out_specs=pl.BlockSpec((None, COND_ROWS, d), lambda i, n: (i, 0, n)),
        out_shape=jax.ShapeDtypeStruct((depth, COND_ROWS, N_MOD * d), F32),
        compiler_params=pltpu.CompilerParams(dimension_semantics=("arbitrary", "arbitrary")),
        name="ada",
    )(cond, ada_w, ada_b.reshape(depth, 1, N_MOD * d))


def _scan_masks(rev):
    c = SCAN_CHUNK
    t = lax.broadcasted_iota(jnp.int32, (c, 2 * c), 0)
    s = lax.broadcasted_iota(jnp.int32, (c, 2 * c), 1) & (c - 1)
    causal = (s >= t) if rev else (s <= t)
    diag = ((t >= SCAN_BLOCK) == (s >= SCAN_BLOCK)) & causal
    s_row = lax.broadcasted_iota(jnp.int32, (1, 2 * c), 1) & (c - 1)
    near_keys = (s_row >= SCAN_BLOCK) if rev else (s_row < SCAN_BLOCK)
    return diag, near_keys


def _block_cumsum(g, rev):
    c = g.shape[0]
    t = lax.broadcasted_iota(jnp.int32, (c, 2 * c), 0)
    s = lax.broadcasted_iota(jnp.int32, (c, 2 * c), 1) & (c - 1)
    same = (t >= SCAN_BLOCK) == (s >= SCAN_BLOCK)
    tri = jnp.where(same & ((s >= t) if rev else (s <= t)), 1.0, 0.0).astype(BF16)
    g_hi = g.astype(BF16)
    g_lo = (g - g_hi.astype(F32)).astype(BF16)
    return _dot(tri, jnp.concatenate([g_hi, g_lo], axis=0))


def _scan_prep(q, k, g, rev):
    c, d = q.shape
    b_in = _block_cumsum(g, rev)
    b_lo, b_hi = b_in[:SCAN_BLOCK], b_in[SCAN_BLOCK:]

    def rows(lo_part, hi_part):
        return jnp.concatenate([lo_part, hi_part], axis=0)

    half = SCAN_BLOCK // 2
    if rev:
        tot_lo, tot_hi = b_lo[0:1], b_hi[0:1]
        mid_lo, mid_hi = b_lo[half:half + 1], b_hi[half:half + 1]
    else:
        tot_lo, tot_hi = b_lo[SCAN_BLOCK - 1:], b_hi[SCAN_BLOCK - 1:]
        mid_lo, mid_hi = b_lo[half - 1:half], b_hi[half - 1:half]

    qh = q * jnp.exp2(b_in)
    ke = k * jnp.exp2(rows(tot_lo - b_lo, tot_hi - b_hi))
    if rev:
        qi = rows(qh[:SCAN_BLOCK] * jnp.exp2(tot_hi), qh[SCAN_BLOCK:])
        ks = rows(ke[:SCAN_BLOCK], ke[SCAN_BLOCK:] * jnp.exp2(tot_lo))
    else:
        qi = rows(qh[:SCAN_BLOCK], qh[SCAN_BLOCK:] * jnp.exp2(tot_lo))
        ks = rows(ke[:SCAN_BLOCK] * jnp.exp2(tot_hi), ke[SCAN_BLOCK:])
    qi = qi.astype(BF16)
    ks = ks.astype(BF16)
    dec = jnp.exp2(tot_lo + tot_hi)
    qh16 = qh.astype(BF16)
    ke16 = ke.astype(BF16)
    qm16 = (q * jnp.exp2(rows(b_lo - mid_lo, b_hi - mid_hi))).astype(BF16)
    km16 = (k * jnp.exp2(rows(mid_lo - b_lo, mid_hi - b_hi))).astype(BF16)

    half_sums = jnp.minimum(jnp.minimum(mid_lo, tot_lo - mid_lo), jnp.minimum(mid_hi, tot_hi - mid_hi))
    fast_vec = jnp.min(half_sums, axis=-1, keepdims=True) >= SAFE_LOG2_DECAY
    fast = jnp.min(half_sums) >= SAFE_LOG2_DECAY

    return (qm16, km16, qh16, ke16, qi, ks, dec, fast_vec), fast


def _scan_finish(operands, v16, st_ref, rev):
    qm16, km16, qh16, ke16, qi, ks, dec, fast_vec = operands
    c = qi.shape[0]
    n_pairs = qi.shape[1] // PAIR_W
    diag_mask, near_keys = _scan_masks(rev)
    diag_mask = diag_mask & fast_vec
    lo, hi = slice(0, SCAN_BLOCK), slice(SCAN_BLOCK, c)
    near, far = (hi, lo) if rev else (lo, hi)
    left = lax.broadcasted_iota(jnp.int32, (1, PAIR_W), 1) < HEAD_W

    def by_head(a):
        zero = jnp.zeros_like(a)
        return jnp.concatenate([jnp.where(left, a, zero), jnp.where(left, zero, a)], axis=0)

    outs = []
    for pr in range(n_pairs):
        sl = slice(pr * PAIR_W, (pr + 1) * PAIR_W)
        attd = _dot_nt(qm16[:, sl], by_head(km16[:, sl]))
        attx = _dot_nt(qh16[far, sl], by_head(ke16[:, sl]))
        att_near = jnp.where(diag_mask[near], attd[near], 0.0)
        att_far = jnp.where(diag_mask[far], attd[far], jnp.where(near_keys, attx, 0.0))
        att = jnp.concatenate([att_far, att_near] if rev else [att_near, att_far], axis=0)
        st = st_ref[pr]
        v_p = v16[:, sl]
        o = _dot(att.astype(BF16), by_head(v_p)) + _dot_nt(qi[:, sl], by_head(st.astype(BF16)))
        outs.append(o)
        v_stack = jnp.concatenate([v_p[:, :HEAD_W], v_p[:, HEAD_W:]], axis=0)
        st_ref[pr] = st * dec[:, sl] + _dot_tn(v_stack, by_head(ks[:, sl]))
    return jnp.concatenate(outs, axis=1)


def _scan_slow_fix(q_ref, v_ref, f_ref, rows, o_ref, od_ref, slow_refs, rev):
    q_s, k_s, v_s, b_s = slow_refs
    c, d = od_ref.shape
    f = f_ref[rows, :]
    q_s[...] = q_ref[rows, :].astype(F32)
    k_s[...] = 1.0 - f
    v_s[...] = v_ref[rows, :].astype(F32)
    b_s[...] = _block_cumsum(jnp.log2(f), rev)
    row = lax.broadcasted_iota(jnp.int32, (c, 1), 0)

    def body(grp, carry):
        t0 = pl.multiple_of(grp * SUBLANES, SUBLANES)
        q8 = q_s[pl.ds(t0, SUBLANES), :]
        b8 = b_s[pl.ds(t0, SUBLANES), :]
        out_rows = []
        for r in range(SUBLANES):
            ti = t0 + r
            prod = q8[r:r + 1] * k_s[...] * jnp.exp2(jnp.minimum(b8[r:r + 1] - b_s[...], 0.0))
            valid = ((row >= SCAN_BLOCK) == (ti >= SCAN_BLOCK)) & ((row >= ti) if rev else (row <= ti))
            heads = []
            for h in range(d // HEAD_W):
                hs = slice(h * HEAD_W, (h + 1) * HEAD_W)
                w = jnp.where(valid, jnp.sum(prod[:, hs], axis=-1, keepdims=True), 0.0)
                heads.append(jnp.sum(w * v_s[:, hs], axis=0, keepdims=True))
            out_rows.append(jnp.concatenate(heads, axis=1))
        od_ref[pl.ds(t0, SUBLANES), :] = jnp.concatenate(out_rows, axis=0)
        return carry

    lax.fori_loop(0, c // SUBLANES, body, 0)
    o_ref[rows, :] += od_ref[...]


def _scan_slow_fixes(fast_ref, q_ref, v_ref, f_ref, o_ref, od_ref, slow_refs, rev):
    def body(ci, carry):
        @pl.when(fast_ref[ci] == 0)
        def _():
            rows = pl.ds(pl.multiple_of(ci * SCAN_CHUNK, SCAN_CHUNK), SCAN_CHUNK)
            _scan_slow_fix(q_ref, v_ref, f_ref, rows, o_ref, od_ref, slow_refs, rev)
        return carry

    lax.fori_loop(0, fast_ref.shape[0], body, 0)


def _state_init(st_ref, s0_ref):
    if s0_ref is None:
        st_ref[...] = jnp.zeros_like(st_ref)
    else:
        for h in range(HEADS):
            o = (h % 2) * HEAD_W
            st_ref[h // 2, :, o:o + HEAD_W] = s0_ref[h].T


def _state_write(st_ref, out_ref):
    for h in range(HEADS):
        o = (h % 2) * HEAD_W
        out_ref[h] = st_ref[h // 2, :, o:o + HEAD_W].T


def _lower_bound(lbl_ref, layer, direction):
    n_layers = lbl_ref.shape[0] // 2
    rows = [lbl_ref[2 * i + direction:2 * i + direction + 1, :] for i in range(n_layers)]
    m = functools.reduce(jnp.maximum, rows)
    ex = [jnp.exp(r - m) for r in rows]
    return sum(ex[:layer + 1]) / sum(ex)


def _chunk_rows(n_rows, rev):
    order = range(n_rows // SCAN_CHUNK)
    return [slice(ci * SCAN_CHUNK, (ci + 1) * SCAN_CHUNK) for ci in (reversed(order) if rev else order)]


def _scan_scratch(d):
    c = SCAN_CHUNK
    return [
        pltpu.VMEM((d // PAIR_W, HEAD_W, PAIR_W), F32),
        pltpu.VMEM((c, d), F32),
        pltpu.VMEM((c, d), F32), pltpu.VMEM((c, d), F32),
        pltpu.VMEM((c, d), F32), pltpu.VMEM((c, d), F32),
    ]


def _const_spec(op):
    if isinstance(op, tuple):
        arr, i = op
        return pl.BlockSpec((None,) + arr.shape[1:], lambda *_: (i,) + (0,) * (arr.ndim - 1),
                            pipeline_mode=pl.Buffered(1))
    return pl.BlockSpec(op.shape, lambda *_: (0,) * op.ndim, pipeline_mode=pl.Buffered(1))


def _const_arg(op):
    return op[0] if isinstance(op, tuple) else op


def _hg_fwd_kernel(*refs, has_s0, layer):
    if has_s0:
        x_ref, mod_ref, g_ref, w_ref, lbl_ref, s0_ref = refs[:6]
        rest = refs[6:]
    else:
        x_ref, mod_ref, g_ref, w_ref, lbl_ref = refs[:5]
        s0_ref = None
        rest = refs[5:]
    of_ref, q_ref, v_ref, fb_ref, gate_ref, sf_ref = rest[:6]
    st_ref, od_ref = rest[6:8]
    slow_refs = rest[8:12]
    ff_ref, fast_ref = rest[12:14]
    j = pl.program_id(1)
    d = x_ref.shape[-1]

    @pl.when(j == 0)
    def _():
        _state_init(st_ref, s0_ref)

    sh, sc = mod_ref[0:1, :], mod_ref[1:2, :]
    h = (_rms(x_ref[...], g_ref[...]) * (1.0 + sc) + sh).astype(BF16)
    proj = lambda i: _dot(h, w_ref[:, i * d:(i + 1) * d])
    lb_f = _lower_bound(lbl_ref, layer, 0)
    lb_b = _lower_bound(lbl_ref, layer, 1)
    q = _silu(proj(0))
    f_f = _forget_gate(proj(1), lb_f)
    q_ref[...] = q.astype(BF16)
    v_ref[...] = proj(3).astype(BF16)
    ff_ref[...] = f_f
    k_f = 1.0 - f_f
    g_f = jnp.log2(f_f)

    chunks = _chunk_rows(x_ref.shape[0], rev=False)
    n_pieces = max(1, len(chunks) // 2)
    piece_w = d // n_pieces

    def backward_gate(p):
        cols = slice(p * piece_w, (p + 1) * piece_w)
        z = _dot(h, w_ref[:, 2 * d + p * piece_w:2 * d + (p + 1) * piece_w])
        fb_ref[:, cols] = _forget_gate(z, lb_b[:, cols])

    def output_gate(p):
        cols = slice(p * piece_w, (p + 1) * piece_w)
        z = _dot(h, w_ref[:, 4 * d + p * piece_w:4 * d + (p + 1) * piece_w])
        gate_ref[:, cols] = _silu(z).astype(BF16)

    fillers = [functools.partial(f, p) for f in (backward_gate, output_gate) for p in range(n_pieces)]
    for rows in chunks:
        operands, fast = _scan_prep(q[rows], k_f[rows], g_f[rows], rev=False)
        fast_ref[rows.start // SCAN_CHUNK] = fast.astype(jnp.int32)
        if fillers:
            fillers.pop(0)()
        of_ref[rows, :] = _scan_finish(operands, v_ref[rows, :], st_ref, rev=False)
    for filler in fillers:
        filler()
    _scan_slow_fixes(fast_ref, q_ref, v_ref, ff_ref, of_ref, od_ref, slow_refs, rev=False)

    @pl.when(j == pl.num_programs(1) - 1)
    def _():
        _state_write(st_ref, sf_ref)


def _hg_fwd_call(x, mod, mod_row0, mod_stride, g, w_in, lbl, state, layer_a):
    bn, l, d = x.shape
    c = min(HG_TOKEN_BLOCK, l)
    assert l % c == 0 and c % SCAN_CHUNK == 0
    n = l // c
    tok = lambda b, j: (b, j, 0)
    in_specs = [
        pl.BlockSpec((None, c, d), tok),
        pl.BlockSpec((None, N_MOD, d), lambda b, j: (mod_row0 + mod_stride * b, 0, 0)),
        _const_spec(g), _const_spec(w_in), _const_spec(lbl),
    ]
    args = [x, mod, g, _const_arg(w_in), lbl]
    if state is not None:
        in_specs.append(pl.BlockSpec((None, None, None, HEADS, HEAD_W, HEAD_W),
                                     lambda b, j: (b, layer_a, 0, 0, 0, 0)))
        args.append(state)
    st_spec = pl.BlockSpec((None, HEADS, HEAD_W, HEAD_W), lambda b, j: (b, 0, 0, 0))
    return pl.pallas_call(
        functools.partial(_hg_fwd_kernel, has_s0=state is not None, layer=layer_a),
        grid=(bn, n),
        in_specs=in_specs,
        out_specs=[pl.BlockSpec((None, c, d), tok)] * 5 + [st_spec],
        out_shape=[
            jax.ShapeDtypeStruct((bn, l, d), F32),
            jax.ShapeDtypeStruct((bn, l, d), BF16),
            jax.ShapeDtypeStruct((bn, l, d), BF16),
            jax.ShapeDtypeStruct((bn, l, d), F32),
            jax.ShapeDtypeStruct((bn, l, d), BF16),
            jax.ShapeDtypeStruct((bn, HEADS, HEAD_W, HEAD_W), F32),
        ],
        scratch_shapes=_scan_scratch(d) + [
            pltpu.VMEM((c, d), F32),
            pltpu.SMEM((c // SCAN_CHUNK,), jnp.int32),
        ],
        compiler_params=pltpu.CompilerParams(
            dimension_semantics=("arbitrary", "arbitrary"), vmem_limit_bytes=VMEM_LIMIT),
        name="hg_fwd",
    )(*args)


def _mlp_tail(x1, mod_ref, g2, w1_ref, w2_ref):
    sh2, sc2, gt2 = mod_ref[3:4, :], mod_ref[4:5, :], mod_ref[5:6, :]
    h2 = (_rms(x1, g2) * (1.0 + sc2) + sh2).astype(BF16)
    a = jnp.maximum(_dot(h2, w1_ref[...]), 0.0)
    return x1 + gt2 * _dot((a * a).astype(BF16), w2_ref[...])


def _hg_bwd_kernel(*refs, has_s0, final, n_blocks):
    x_ref, mod_ref, gate_ref, of_ref, q_ref, v_ref, fb_ref, on_ref, g2_ref, fg_ref = refs[:10]
    wo_ref, w1_ref, w2_ref, sf_ref = refs[10:14]
    if has_s0:
        s0_ref = refs[14]
        rest = refs[15:]
    else:
        s0_ref = None
        rest = refs[14:]
    y_ref, sfb_ref = rest[:2]
    st_ref, od_ref = rest[2:4]
    slow_refs = rest[4:8]
    o_ref, fast_ref = rest[8:10]
    s = pl.program_id(0)
    total = pl.num_programs(0) - 1
    live = s < total
    j = jnp.minimum(s, total - 1) % n_blocks
    d = x_ref.shape[-1]

    @pl.when(live & (j == 0))
    def _():
        _state_init(st_ref, s0_ref)

    chunk_rows = _chunk_rows(q_ref.shape[0], rev=True)
    n_slabs = len(chunk_rows)
    slab_w = w1_ref.shape[1] // n_slabs

    def step(finishing):
        if finishing:
            o = o_ref[...]
            parts = []
            for h in range(d // HEAD_W):
                oh = o[:, h * HEAD_W:(h + 1) * HEAD_W]
                parts.append(oh * lax.rsqrt(jnp.mean(oh * oh, axis=-1, keepdims=True) + EPS))
            o = jnp.concatenate(parts, axis=1) * on_ref[...] * gate_ref[...].astype(F32)
            gt1 = mod_ref[2:3, :]
            x1 = x_ref[...] + gt1 * _dot(o.astype(BF16), wo_ref[...])
            sh2, sc2, gt2 = mod_ref[3:4, :], mod_ref[4:5, :], mod_ref[5:6, :]
            h2 = (_rms(x1, g2_ref[...]) * (1.0 + sc2) + sh2).astype(BF16)

        def mlp_slab(i):
            a = jnp.maximum(_dot(h2, w1_ref[:, i * slab_w:(i + 1) * slab_w]), 0.0)
            return _dot((a * a).astype(BF16), w2_ref[i * slab_w:(i + 1) * slab_w, :])

        mlp = 0.0
        for ci, rows in enumerate(chunk_rows):
            f_b = fb_ref[rows, :]
            operands, fast = _scan_prep(q_ref[rows, :].astype(F32), 1.0 - f_b, jnp.log2(f_b), rev=True)
            fast_ref[rows.start // SCAN_CHUNK] = fast.astype(jnp.int32)
            if finishing:
                mlp = mlp + mlp_slab(ci)
            o_ref[rows, :] = of_ref[rows, :] + _scan_finish(operands, v_ref[rows, :], st_ref, rev=True)
        if finishing:
            x2 = x1 + gt2 * mlp
            y_ref[...] = _rms(x2, fg_ref[...]) if final else x2

    pl.when(s == 0)(lambda: step(finishing=False))
    pl.when(s > 0)(lambda: step(finishing=True))
    _scan_slow_fixes(fast_ref, q_ref, v_ref, fb_ref, o_ref, od_ref, slow_refs, rev=True)

    @pl.when(live & (j == n_blocks - 1))
    def _():
        sfb_ref[0] = sf_ref[...]
        _state_write(st_ref, sfb_ref.at[1])


def _hg_bwd_call(x, mod, mod_row0, mod_stride, stash, s_f, onorm, g2, fg, w_out, w1, w2, state, layer_a, final):
    bn, l, d = x.shape
    c = min(HG_TOKEN_BLOCK, l)
    assert l % c == 0 and c % SCAN_CHUNK == 0
    n = l // c
    total = bn * n

    def cur(s):
        t = jnp.minimum(s, total - 1)
        return t // n, n - 1 - t % n

    def prev(s):
        t = jnp.maximum(s - 1, 0)
        return t // n, n - 1 - t % n

    tok_cur = lambda s: (*cur(s), 0)
    tok_prev = lambda s: (*prev(s), 0)
    o_f, q, v, f_b, gate = stash
    consts = [onorm, g2, fg, w_out, w1, w2]
    in_specs = [
        pl.BlockSpec((None, c, d), tok_prev),
        pl.BlockSpec((None, N_MOD, d), lambda s: (mod_row0 + mod_stride * prev(s)[0], 0, 0)),
        pl.BlockSpec((None, c, d), tok_prev),
    ] + [pl.BlockSpec((None, c, d), tok_cur)] * 4 + [_const_spec(op) for op in consts]
    in_specs.append(pl.BlockSpec((None, HEADS, HEAD_W, HEAD_W), lambda s: (cur(s)[0], 0, 0, 0)))
    args = [x, mod, gate, o_f, q, v, f_b] + [_const_arg(op) for op in consts] + [s_f]
    if state is not None:
        in_specs.append(pl.BlockSpec((None, None, None, HEADS, HEAD_W, HEAD_W),
                                     lambda s: (cur(s)[0], layer_a, 1, 0, 0, 0)))
        args.append(state)
    st_spec = pl.BlockSpec((None, 2, HEADS, HEAD_W, HEAD_W), lambda s: (cur(s)[0], 0, 0, 0, 0))
    return pl.pallas_call(
        functools.partial(_hg_bwd_kernel, has_s0=state is not None, final=final, n_blocks=n),
        grid=(total + 1,),
        in_specs=in_specs,
        out_specs=[pl.BlockSpec((None, c, d), tok_prev), st_spec],
        out_shape=[
            jax.ShapeDtypeStruct((bn, l, d), F32),
            jax.ShapeDtypeStruct((bn, 2, HEADS, HEAD_W, HEAD_W), F32),
        ],
        scratch_shapes=_scan_scratch(d) + [
            pltpu.VMEM((c, d), F32),
            pltpu.SMEM((c // SCAN_CHUNK,), jnp.int32),
        ],
        compiler_params=pltpu.CompilerParams(
            dimension_semantics=("arbitrary",), vmem_limit_bytes=VMEM_LIMIT),
        name="hg_bwd",
    )(*args)


def _cm_kernel(x_ref, mod_ref, g1_ref, g2_ref, fg_ref, win_ref, lng_ref, lnb_ref, ws_ref, bst_ref,
               wo_ref, w1_ref, w2_ref, y_ref, *, final):
    tb, d = x_ref.shape
    n_chunks = tb // MIX_CHUNK
    gw = d // CM_GROUPS
    x = x_ref[...]
    sh, sc, gt1 = mod_ref[0:1, :], mod_ref[1:2, :], mod_ref[2:3, :]
    h = (_rms(x, g1_ref[...]) * (1.0 + sc) + sh).astype(BF16)
    u = _gelu_tanh(_dot(h, win_ref[:, :d]))
    vv = _gelu_tanh(_dot(h, win_ref[:, d:]))
    mu = jnp.mean(vv, axis=-1, keepdims=True)
    vc = vv - mu
    var = jnp.mean(vc * vc, axis=-1, keepdims=True)
    vn = (vc * lax.rsqrt(var + EPS) * lng_ref[...] + lnb_ref[...]).astype(BF16)
    per_group = []
    for g in range(CM_GROUPS):
        rhs = jnp.concatenate(
            [vn[ch * MIX_CHUNK:(ch + 1) * MIX_CHUNK, g * gw:(g + 1) * gw] for ch in range(n_chunks)], axis=1)
        per_group.append(_dot(ws_ref[g], rhs) + bst_ref[:, g:g + 1])
    s = jnp.concatenate(
        [jnp.concatenate([pg[:, ch * gw:(ch + 1) * gw] for pg in per_group], axis=1) for ch in range(n_chunks)],
        axis=0)
    x1 = x + gt1 * _dot((u * s).astype(BF16), wo_ref[...])
    x2 = _mlp_tail(x1, mod_ref, g2_ref[...], w1_ref, w2_ref)
    y_ref[...] = _rms(x2, fg_ref[...]) if final else x2


def _cm_call(x, mod, mod_row0, mod_stride, g1, g2, fg, w_in, ln_g, ln_b, w_s, b_st, w_out, w1, w2, final):
    out_shape = x.shape
    if mod_stride == 0 and x.shape[1] % MIX_CHUNK == 0:
        x = x.reshape(1, -1, x.shape[-1])
    bn, l, d = x.shape
    tb = min(CM_TOKEN_BLOCK, l)
    assert l % tb == 0 and tb % MIX_CHUNK == 0
    tok = lambda b, j: (b, j, 0)
    consts = [g1, g2, fg, w_in, ln_g, ln_b, w_s, b_st, w_out, w1, w2]
    return pl.pallas_call(
        functools.partial(_cm_kernel, final=final),
        grid=(bn, l // tb),
        in_specs=[
            pl.BlockSpec((None, tb, d), tok),
            pl.BlockSpec((None, N_MOD, d), lambda b, j: (mod_row0 + mod_stride * b, 0, 0)),
        ] + [_const_spec(op) for op in consts],
        out_specs=pl.BlockSpec((None, tb, d), tok),
        out_shape=jax.ShapeDtypeStruct((bn, l, d), F32),
        compiler_params=pltpu.CompilerParams(
            dimension_semantics=("arbitrary", "arbitrary"), vmem_limit_bytes=VMEM_LIMIT),
        name="cm",
    )(x, mod, *[_const_arg(op) for op in consts]).reshape(out_shape)


def kernel(x_prompt, x_sample, state_hgrn, c, c_ctx, ada_w, ada_b, norm_mix_g, norm_mlp_g, mlp_w1, mlp_w2,
           hgrn_w_in, hgrn_lb_logits, hgrn_onorm_g, hgrn_w_out, cm_w_in, cm_ln_g, cm_ln_b, cm_w_s, cm_b_s,
           cm_w_out, final_norm_g):
    depth, d, _ = ada_w.shape
    n_lat = c.shape[0]
    assert d % PAIR_W == 0 and d // HEAD_W == HEADS and 1 + n_lat <= COND_ROWS
    assert x_prompt.shape[1] % SCAN_CHUNK == 0 and x_sample.shape[1] % SCAN_CHUNK == 0

    cond = jnp.concatenate([c_ctx[None, :], c, jnp.zeros((COND_ROWS - 1 - n_lat, d), F32)], axis=0)
    mods = _ada_call(cond, ada_w, ada_b).reshape(depth, COND_ROWS, N_MOD, d)

    row = lambda a: a.reshape(1, d)
    lbl = hgrn_lb_logits.reshape(-1, d)
    fg = row(final_norm_g)
    mlp_w1, mlp_w2, hgrn_w_in, hgrn_w_out, cm_w_in, cm_w_s, cm_w_out = (
        a.astype(BF16) for a in (mlp_w1, mlp_w2, hgrn_w_in, hgrn_w_out, cm_w_in, cm_w_s, cm_w_out))
    ctx, lat = x_prompt, x_sample
    new_states = []
    for i in range(depth):
        j = i // N_MIXERS
        final = i == depth - 1
        w1 = (mlp_w1, i)
        w2 = (mlp_w2, i)
        g1, g2 = row(norm_mix_g[i]), row(norm_mlp_g[i])
        if i % N_MIXERS == 0:
            w_in = (hgrn_w_in, j)
            w_out = (hgrn_w_out, j)
            onorm = row(hgrn_onorm_g[j])
            *stash, s_f = _hg_fwd_call(ctx, mods[i], 0, 0, g1, w_in, lbl, None, j)
            ctx, s_fb = _hg_bwd_call(ctx, mods[i], 0, 0, stash, s_f, onorm, g2, fg, w_out, w1, w2, None, j, final)
            new_states.append(s_fb)
            *stash, s_f = _hg_fwd_call(lat, mods[i], 1, 1, g1, w_in, lbl, state_hgrn, j)
            lat, _ = _hg_bwd_call(lat, mods[i], 1, 1, stash, s_f, onorm, g2, fg, w_out, w1, w2, state_hgrn, j, final)
        else:
            cm_args = (g1, g2, fg, (cm_w_in, j), row(cm_ln_g[j]), row(cm_ln_b[j]),
                       (cm_w_s, j), cm_b_s[j].T, (cm_w_out, j), w1, w2, final)
            ctx = _cm_call(ctx, mods[i], 0, 0, *cm_args)
            lat = _cm_call(lat, mods[i], 1, 1, *cm_args)
    new_state = jnp.stack(new_states, axis=1).astype(x_prompt.dtype)
    return ctx, lat, new_state
```

```python
import functools

import jax
import jax.numpy as jnp
from jax import lax
from jax.experimental import pallas as pl
from jax.experimental.pallas import tpu as pltpu

F32 = jnp.float32
BF16 = jnp.bfloat16

EPS = 1e-6
N_MOD = 6
N_MIXERS = 2
HEADS = 8
HEAD_W = 128
PAIR_W = 2 * HEAD_W
MIX_CHUNK = 128
CM_GROUPS = 8
SCAN_BLOCK = 64
SCAN_CHUNK = 2 * SCAN_BLOCK
SUBLANES = 8
COND_ROWS = SUBLANES
SAFE_LOG2_DECAY = -115.0
HG_TOKEN_BLOCK = 512
CM_TOKEN_BLOCK = 512
VMEM_BYTES_V7X = 64 * 1024 * 1024
VMEM_LIMIT = VMEM_BYTES_V7X * 15 // 16


def _silu(x):
    h = 0.5 * x
    return h * jnp.tanh(h) + h


def _forget_gate(z, lb):
    return 0.5 * (1.0 + lb) + (0.5 * (1.0 - lb)) * jnp.tanh(0.5 * z)


def _gelu_tanh(x):
    h = 0.5 * x
    t = jnp.tanh(x * (0.7978845608028654 + (0.7978845608028654 * 0.044715) * (x * x)))
    return h * t + h


def _rms(x, g):
    return x * lax.rsqrt(jnp.mean(x * x, axis=-1, keepdims=True) + EPS) * g


def _dot(a, b):
    return jnp.dot(a, b, preferred_element_type=F32)


def _dot_nt(a, b):
    return lax.dot_general(a, b, (((1,), (1,)), ((), ())), preferred_element_type=F32)


def _dot_tn(a, b):
    return lax.dot_general(a, b, (((0,), (0,)), ((), ())), preferred_element_type=F32)


def _ada_kernel(cond_ref, w_ref, b_ref, out_ref):
    a = _silu(cond_ref[...]).astype(BF16)
    out_ref[...] = _dot(a, w_ref[...].astype(BF16)) + b_ref[...]


def _ada_call(cond, ada_w, ada_b):
    depth, d, _ = ada_w.shape
    return pl.pallas_call(
        _ada_kernel,
        grid=(depth, N_MOD),
        in_specs=[
            pl.BlockSpec((COND_ROWS, d), lambda i, n: (0, 0)),
            pl.BlockSpec((None, d, d), lambda i, n: (i, 0, n)),
            pl.BlockSpec((None, 1, d), lambda i, n: (i, 0, n)),
        ],
        out_specs=pl.BlockSpec((None, COND_ROWS, d), lambda i, n: (i, 0, n)),
        out_shape=jax.ShapeDtypeStruct((depth, COND_ROWS, N_MOD * d), F32),
        compiler_params=pltpu.CompilerParams(dimension_semantics=("arbitrary", "arbitrary")),
        name="ada",
    )(cond, ada_w, ada_b.reshape(depth, 1, N_MOD * d))


def _scan_masks(rev):
    c = SCAN_CHUNK
    t = lax.broadcasted_iota(jnp.int32, (c, 2 * c), 0)
    s = lax.broadcasted_iota(jnp.int32, (c, 2 * c), 1) & (c - 1)
    causal = (s >= t) if rev else (s <= t)
    diag = ((t >= SCAN_BLOCK) == (s >= SCAN_BLOCK)) & causal
    s_row = lax.broadcasted_iota(jnp.int32, (1, 2 * c), 1) & (c - 1)
    near_keys = (s_row >= SCAN_BLOCK) if rev else (s_row < SCAN_BLOCK)
    return diag, near_keys


def _block_cumsum(g, rev):
    c = g.shape[0]
    t = lax.broadcasted_iota(jnp.int32, (c, 2 * c), 0)
    s = lax.broadcasted_iota(jnp.int32, (c, 2 * c), 1) & (c - 1)
    same = (t >= SCAN_BLOCK) == (s >= SCAN_BLOCK)
    tri = jnp.where(same & ((s >= t) if rev else (s <= t)), 1.0, 0.0).astype(BF16)
    g_hi = g.astype(BF16)
    g_lo = (g - g_hi.astype(F32)).astype(BF16)
    return _dot(tri, jnp.concatenate([g_hi, g_lo], axis=0))


def _scan_prep(q, k, g, rev):
    b_in = _block_cumsum(g, rev)
    b_lo, b_hi = b_in[:SCAN_BLOCK], b_in[SCAN_BLOCK:]

    def rows(lo_part, hi_part):
        return jnp.concatenate([lo_part, hi_part], axis=0)

    half = SCAN_BLOCK // 2
    if rev:
        tot_lo, tot_hi = b_lo[0:1], b_hi[0:1]
        mid_lo, mid_hi = b_lo[half:half + 1], b_hi[half:half + 1]
    else:
        tot_lo, tot_hi = b_lo[SCAN_BLOCK - 1:], b_hi[SCAN_BLOCK - 1:]
        mid_lo, mid_hi = b_lo[half - 1:half], b_hi[half - 1:half]

    qh = q * jnp.exp2(b_in)
    ke = k * jnp.exp2(rows(tot_lo - b_lo, tot_hi - b_hi))
    if rev:
        qi = rows(qh[:SCAN_BLOCK] * jnp.exp2(tot_hi), qh[SCAN_BLOCK:])
        ks = rows(ke[:SCAN_BLOCK], ke[SCAN_BLOCK:] * jnp.exp2(tot_lo))
    else:
        qi = rows(qh[:SCAN_BLOCK], qh[SCAN_BLOCK:] * jnp.exp2(tot_lo))
        ks = rows(ke[:SCAN_BLOCK] * jnp.exp2(tot_hi), ke[SCAN_BLOCK:])
    qi = qi.astype(BF16)
    ks = ks.astype(BF16)
    dec = jnp.exp2(tot_lo + tot_hi)
    qh16 = qh.astype(BF16)
    ke16 = ke.astype(BF16)
    qm16 = (q * jnp.exp2(rows(b_lo - mid_lo, b_hi - mid_hi))).astype(BF16)
    km16 = (k * jnp.exp2(rows(mid_lo - b_lo, mid_hi - b_hi))).astype(BF16)

    half_sums = jnp.minimum(jnp.minimum(mid_lo, tot_lo - mid_lo), jnp.minimum(mid_hi, tot_hi - mid_hi))
    fast_vec = jnp.min(half_sums, axis=-1, keepdims=True) >= SAFE_LOG2_DECAY
    fast = jnp.min(half_sums) >= SAFE_LOG2_DECAY

    return (qm16, km16, qh16, ke16, qi, ks, dec, fast_vec), fast


def _scan_finish(operands, v16, st_ref, rev):
    qm16, km16, qh16, ke16, qi, ks, dec, fast_vec = operands
    c = qi.shape[0]
    n_pairs = qi.shape[1] // PAIR_W
    diag_mask, near_keys = _scan_masks(rev)
    diag_mask = diag_mask & fast_vec
    lo, hi = slice(0, SCAN_BLOCK), slice(SCAN_BLOCK, c)
    near, far = (hi, lo) if rev else (lo, hi)
    left = lax.broadcasted_iota(jnp.int32, (1, PAIR_W), 1) < HEAD_W

    def by_head(a):
        zero = jnp.zeros_like(a)
        return jnp.concatenate([jnp.where(left, a, zero), jnp.where(left, zero, a)], axis=0)

    outs = []
    for pr in range(n_pairs):
        sl = slice(pr * PAIR_W, (pr + 1) * PAIR_W)
        attd = _dot_nt(qm16[:, sl], by_head(km16[:, sl]))
        attx = _dot_nt(qh16[far, sl], by_head(ke16[:, sl]))
        att_near = jnp.where(diag_mask[near], attd[near], 0.0)
        att_far = jnp.where(diag_mask[far], attd[far], jnp.where(near_keys, attx, 0.0))
        att = jnp.concatenate([att_far, att_near] if rev else [att_near, att_far], axis=0)
        st = st_ref[pr]
        v_p = v16[:, sl]
        o = _dot(att.astype(BF16), by_head(v_p)) + _dot_nt(qi[:, sl], by_head(st.astype(BF16)))
        outs.append(o)
        v_stack = jnp.concatenate([v_p[:, :HEAD_W], v_p[:, HEAD_W:]], axis=0)
        st_ref[pr] = st * dec[:, sl] + _dot_tn(v_stack, by_head(ks[:, sl]))
    return jnp.concatenate(outs, axis=1)


def _scan_slow_fix(q_ref, v_ref, f_ref, rows, o_ref, od_ref, slow_refs, rev):
    q_s, k_s, v_s, b_s = slow_refs
    c, d = od_ref.shape
    f = f_ref[rows, :]
    q_s[...] = q_ref[rows, :].astype(F32)
    k_s[...] = 1.0 - f
    v_s[...] = v_ref[rows, :].astype(F32)
    b_s[...] = _block_cumsum(jnp.log2(f), rev)
    row = lax.broadcasted_iota(jnp.int32, (c, 1), 0)

    def body(grp, carry):
        t0 = pl.multiple_of(grp * SUBLANES, SUBLANES)
        q8 = q_s[pl.ds(t0, SUBLANES), :]
        b8 = b_s[pl.ds(t0, SUBLANES), :]
        out_rows = []
        for r in range(SUBLANES):
            ti = t0 + r
            prod = q8[r:r + 1] * k_s[...] * jnp.exp2(jnp.minimum(b8[r:r + 1] - b_s[...], 0.0))
            valid = ((row >= SCAN_BLOCK) == (ti >= SCAN_BLOCK)) & ((row >= ti) if rev else (row <= ti))
            heads = []
            for h in range(d // HEAD_W):
                hs = slice(h * HEAD_W, (h + 1) * HEAD_W)
                w = jnp.where(valid, jnp.sum(prod[:, hs], axis=-1, keepdims=True), 0.0)
                heads.append(jnp.sum(w * v_s[:, hs], axis=0, keepdims=True))
            out_rows.append(jnp.concatenate(heads, axis=1))
        od_ref[pl.ds(t0, SUBLANES), :] = jnp.concatenate(out_rows, axis=0)
        return carry

    lax.fori_loop(0, c // SUBLANES, body, 0)
    o_ref[rows, :] += od_ref[...]


def _scan_slow_fixes(fast_ref, q_ref, v_ref, f_ref, o_ref, od_ref, slow_refs, rev):
    def body(ci, carry):
        @pl.when(fast_ref[ci] == 0)
        def _():
            rows = pl.ds(pl.multiple_of(ci * SCAN_CHUNK, SCAN_CHUNK), SCAN_CHUNK)
            _scan_slow_fix(q_ref, v_ref, f_ref, rows, o_ref, od_ref, slow_refs, rev)
        return carry

    lax.fori_loop(0, fast_ref.shape[0], body, 0)


def _state_init(st_ref, s0_ref):
    if s0_ref is None:
        st_ref[...] = jnp.zeros_like(st_ref)
    else:
        for h in range(HEADS):
            o = (h % 2) * HEAD_W
            st_ref[h // 2, :, o:o + HEAD_W] = s0_ref[h].T


def _state_write(st_ref, out_ref):
    for h in range(HEADS):
        o = (h % 2) * HEAD_W
        out_ref[h] = st_ref[h // 2, :, o:o + HEAD_W].T


def _lower_bound(lbl_ref, layer, direction):
    n_layers = lbl_ref.shape[0] // 2
    rows = [lbl_ref[2 * i + direction:2 * i + direction + 1, :] for i in range(n_layers)]
    m = functools.reduce(jnp.maximum, rows)
    ex = [jnp.exp(r - m) for r in rows]
    return sum(ex[:layer + 1]) / sum(ex)


def _chunk_rows(n_rows, rev):
    order = range(n_rows // SCAN_CHUNK)
    return [slice(ci * SCAN_CHUNK, (ci + 1) * SCAN_CHUNK) for ci in (reversed(order) if rev else order)]


def _scan_scratch(d):
    c = SCAN_CHUNK
    return [
        pltpu.VMEM((d // PAIR_W, HEAD_W, PAIR_W), F32),
        pltpu.VMEM((c, d), F32),
        pltpu.VMEM((c, d), F32), pltpu.VMEM((c, d), F32),
        pltpu.VMEM((c, d), F32), pltpu.VMEM((c, d), F32),
    ]


def _const_spec(op):
    if isinstance(op, tuple):
        arr, i = op
        return pl.BlockSpec((None,) + arr.shape[1:], lambda *_: (i,) + (0,) * (arr.ndim - 1),
                            pipeline_mode=pl.Buffered(1))
    return pl.BlockSpec(op.shape, lambda *_: (0,) * op.ndim, pipeline_mode=pl.Buffered(1))


def _const_arg(op):
    return op[0] if isinstance(op, tuple) else op


def _hg_fwd_kernel(*refs, has_s0, layer):
    if has_s0:
        x_ref, mod_ref, g_ref, w_ref, lbl_ref, s0_ref = refs[:6]
        rest = refs[6:]
    else:
        x_ref, mod_ref, g_ref, w_ref, lbl_ref = refs[:5]
        s0_ref = None
        rest = refs[5:]
    of_ref, q_ref, v_ref, fb_ref, gate_ref, sf_ref = rest[:6]
    st_ref, od_ref = rest[6:8]
    slow_refs = rest[8:12]
    ff_ref, fast_ref = rest[12:14]
    j = pl.program_id(1)
    d = x_ref.shape[-1]

    @pl.when(j == 0)
    def _():
        _state_init(st_ref, s0_ref)

    sh, sc = mod_ref[0:1, :], mod_ref[1:2, :]
    h = (_rms(x_ref[...], g_ref[...]) * (1.0 + sc) + sh).astype(BF16)
    proj = lambda i: _dot(h, w_ref[:, i * d:(i + 1) * d])
    lb_f = _lower_bound(lbl_ref, layer, 0)
    lb_b = _lower_bound(lbl_ref, layer, 1)
    q = _silu(proj(0))
    f_f = _forget_gate(proj(1), lb_f)
    q_ref[...] = q.astype(BF16)
    v_ref[...] = proj(3).astype(BF16)
    ff_ref[...] = f_f
    k_f = 1.0 - f_f
    g_f = jnp.log2(f_f)

    chunks = _chunk_rows(x_ref.shape[0], rev=False)
    n_pieces = max(1, len(chunks) // 2)
    piece_w = d // n_pieces

    def backward_gate(p):
        cols = slice(p * piece_w, (p + 1) * piece_w)
        z = _dot(h, w_ref[:, 2 * d + p * piece_w:2 * d + (p + 1) * piece_w])
        fb_ref[:, cols] = _forget_gate(z, lb_b[:, cols])

    def output_gate(p):
        cols = slice(p * piece_w, (p + 1) * piece_w)
        z = _dot(h, w_ref[:, 4 * d + p * piece_w:4 * d + (p + 1) * piece_w])
        gate_ref[:, cols] = _silu(z).astype(BF16)

    fillers = [functools.partial(f, p) for f in (backward_gate, output_gate) for p in range(n_pieces)]
    for rows in chunks:
        operands, fast = _scan_prep(q[rows], k_f[rows], g_f[rows], rev=False)
        fast_ref[rows.start // SCAN_CHUNK] = fast.astype(jnp.int32)
        if fillers:
            fillers.pop(0)()
        of_ref[rows, :] = _scan_finish(operands, v_ref[rows, :], st_ref, rev=False)
    for filler in fillers:
        filler()
    _scan_slow_fixes(fast_ref, q_ref, v_ref, ff_ref, of_ref, od_ref, slow_refs, rev=False)

    @pl.when(j == pl.num_programs(1) - 1)
    def _():
        _state_write(st_ref, sf_ref)


def _hg_fwd_call(x, mod, mod_row0, mod_stride, g, w_in, lbl, state, layer_a):
    bn, l, d = x.shape
    c = min(HG_TOKEN_BLOCK, l)
    assert l % c == 0 and c % SCAN_CHUNK == 0
    n = l // c
    tok = lambda b, j: (b, j, 0)
    in_specs = [
        pl.BlockSpec((None, c, d), tok),
        pl.BlockSpec((None, N_MOD, d), lambda b, j: (mod_row0 + mod_stride * b, 0, 0)),
        _const_spec(g), _const_spec(w_in), _const_spec(lbl),
    ]
    args = [x, mod, g, _const_arg(w_in), lbl]
    if state is not None:
        in_specs.append(pl.BlockSpec((None, None, None, HEADS, HEAD_W, HEAD_W),
                                     lambda b, j: (b, layer_a, 0, 0, 0, 0)))
        args.append(state)
    st_spec = pl.BlockSpec((None, HEADS, HEAD_W, HEAD_W), lambda b, j: (b, 0, 0, 0))
    return pl.pallas_call(
        functools.partial(_hg_fwd_kernel, has_s0=state is not None, layer=layer_a),
        grid=(bn, n),
        in_specs=in_specs,
        out_specs=[pl.BlockSpec((None, c, d), tok)] * 5 + [st_spec],
        out_shape=[
            jax.ShapeDtypeStruct((bn, l, d), F32),
            jax.ShapeDtypeStruct((bn, l, d), BF16),
            jax.ShapeDtypeStruct((bn, l, d), BF16),
            jax.ShapeDtypeStruct((bn, l, d), F32),
            jax.ShapeDtypeStruct((bn, l, d), BF16),
            jax.ShapeDtypeStruct((bn, HEADS, HEAD_W, HEAD_W), F32),
        ],
        scratch_shapes=_scan_scratch(d) + [
            pltpu.VMEM((c, d), F32),
            pltpu.SMEM((c // SCAN_CHUNK,), jnp.int32),
        ],
        compiler_params=pltpu.CompilerParams(
            dimension_semantics=("arbitrary", "arbitrary"), vmem_limit_bytes=VMEM_LIMIT),
        name="hg_fwd",
    )(*args)


def _mlp_tail(x1, mod_ref, g2, w1_ref, w2_ref):
    sh2, sc2, gt2 = mod_ref[3:4, :], mod_ref[4:5, :], mod_ref[5:6, :]
    h2 = (_rms(x1, g2) * (1.0 + sc2) + sh2).astype(BF16)
    a = jnp.maximum(_dot(h2, w1_ref[...]), 0.0)
    return x1 + gt2 * _dot((a * a).astype(BF16), w2_ref[...])


def _hg_bwd_kernel(*refs, has_s0, final, n_blocks):
    x_ref, mod_ref, gate_ref, of_ref, q_ref, v_ref, fb_ref, on_ref, g2_ref, fg_ref = refs[:10]
    wo_ref, w1_ref, w2_ref, sf_ref = refs[10:14]
    if has_s0:
        s0_ref = refs[14]
        rest = refs[15:]
    else:
        s0_ref = None
        rest = refs[14:]
    y_ref, sfb_ref = rest[:2]
    st_ref, od_ref = rest[2:4]
    slow_refs = rest[4:8]
    o_ref, fast_ref = rest[8:10]
    s = pl.program_id(0)
    total = pl.num_programs(0) - 1
    live = s < total
    j = jnp.minimum(s, total - 1) % n_blocks
    d = x_ref.shape[-1]

    @pl.when(live & (j == 0))
    def _():
        _state_init(st_ref, s0_ref)

    chunk_rows = _chunk_rows(q_ref.shape[0], rev=True)
    n_slabs = len(chunk_rows)
    slab_w = w1_ref.shape[1] // n_slabs

    def step(finishing):
        if finishing:
            o = o_ref[...]
            parts = []
            for h in range(d // HEAD_W):
                oh = o[:, h * HEAD_W:(h + 1) * HEAD_W]
                parts.append(oh * lax.rsqrt(jnp.mean(oh * oh, axis=-1, keepdims=True) + EPS))
            o = jnp.concatenate(parts, axis=1) * on_ref[...] * gate_ref[...].astype(F32)
            gt1 = mod_ref[2:3, :]
            x1 = x_ref[...] + gt1 * _dot(o.astype(BF16), wo_ref[...])
            sh2, sc2, gt2 = mod_ref[3:4, :], mod_ref[4:5, :], mod_ref[5:6, :]
            h2 = (_rms(x1, g2_ref[...]) * (1.0 + sc2) + sh2).astype(BF16)

        def mlp_slab(i):
            a = jnp.maximum(_dot(h2, w1_ref[:, i * slab_w:(i + 1) * slab_w]), 0.0)
            return _dot((a * a).astype(BF16), w2_ref[i * slab_w:(i + 1) * slab_w, :])

        mlp = 0.0
        for ci, rows in enumerate(chunk_rows):
            f_b = fb_ref[rows, :]
            operands, fast = _scan_prep(q_ref[rows, :].astype(F32), 1.0 - f_b, jnp.log2(f_b), rev=True)
            fast_ref[rows.start // SCAN_CHUNK] = fast.astype(jnp.int32)
            if finishing:
                mlp = mlp + mlp_slab(ci)
            o_ref[rows, :] = of_ref[rows, :] + _scan_finish(operands, v_ref[rows, :], st_ref, rev=True)
        if finishing:
            x2 = x1 + gt2 * mlp
            y_ref[...] = _rms(x2, fg_ref[...]) if final else x2

    pl.when(s == 0)(lambda: step(finishing=False))
    pl.when(s > 0)(lambda: step(finishing=True))
    _scan_slow_fixes(fast_ref, q_ref, v_ref, fb_ref, o_ref, od_ref, slow_refs, rev=True)

    @pl.when(live & (j == n_blocks - 1))
    def _():
        sfb_ref[0] = sf_ref[...]
        _state_write(st_ref, sfb_ref.at[1])


def _hg_bwd_call(x, mod, mod_row0, mod_stride, stash, s_f, onorm, g2, fg, w_out, w1, w2, state, layer_a, final):
    bn, l, d = x.shape
    c = min(HG_TOKEN_BLOCK, l)
    assert l % c == 0 and c % SCAN_CHUNK == 0
    n = l // c
    total = bn * n

    def cur(s):
        t = jnp.minimum(s, total - 1)
        return t // n, n - 1 - t % n

    def prev(s):
        t = jnp.maximum(s - 1, 0)
        return t // n, n - 1 - t % n

    tok_cur = lambda s: (*cur(s), 0)
    tok_prev = lambda s: (*prev(s), 0)
    o_f, q, v, f_b, gate = stash
    consts = [onorm, g2, fg, w_out, w1, w2]
    in_specs = [
        pl.BlockSpec((None, c, d), tok_prev),
        pl.BlockSpec((None, N_MOD, d), lambda s: (mod_row0 + mod_stride * prev(s)[0], 0, 0)),
        pl.BlockSpec((None, c, d), tok_prev),
    ] + [pl.BlockSpec((None, c, d), tok_cur)] * 4 + [_const_spec(op) for op in consts]
    in_specs.append(pl.BlockSpec((None, HEADS, HEAD_W, HEAD_W), lambda s: (cur(s)[0], 0, 0, 0)))
    args = [x, mod, gate, o_f, q, v, f_b] + [_const_arg(op) for op in consts] + [s_f]
    if state is not None:
        in_specs.append(pl.BlockSpec((None, None, None, HEADS, HEAD_W, HEAD_W),
                                     lambda s: (cur(s)[0], layer_a, 1, 0, 0, 0)))
        args.append(state)
    st_spec = pl.BlockSpec((None, 2, HEADS, HEAD_W, HEAD_W), lambda s: (cur(s)[0], 0, 0, 0, 0))
    return pl.pallas_call(
        functools.partial(_hg_bwd_kernel, has_s0=state is not None, final=final, n_blocks=n),
        grid=(total + 1,),
        in_specs=in_specs,
        out_specs=[pl.BlockSpec((None, c, d), tok_prev), st_spec],
        out_shape=[
            jax.ShapeDtypeStruct((bn, l, d), F32),
            jax.ShapeDtypeStruct((bn, 2, HEADS, HEAD_W, HEAD_W), F32),
        ],
        scratch_shapes=_scan_scratch(d) + [
            pltpu.VMEM((c, d), F32),
            pltpu.SMEM((c // SCAN_CHUNK,), jnp.int32),
        ],
        compiler_params=pltpu.CompilerParams(
            dimension_semantics=("arbitrary",), vmem_limit_bytes=VMEM_LIMIT),
        name="hg_bwd",
    )(*args)


def _cm_kernel(x_ref, mod_ref, g1_ref, g2_ref, fg_ref, win_ref, lng_ref, lnb_ref, ws_ref, bst_ref,
               wo_ref, w1_ref, w2_ref, y_ref, *, final):
    tb, d = x_ref.shape
    n_chunks = tb // MIX_CHUNK
    gw = d // CM_GROUPS
    x = x_ref[...]
    sh, sc, gt1 = mod_ref[0:1, :], mod_ref[1:2, :], mod_ref[2:3, :]
    h = (_rms(x, g1_ref[...]) * (1.0 + sc) + sh).astype(BF16)
    u = _gelu_tanh(_dot(h, win_ref[:, :d]))
    vv = _gelu_tanh(_dot(h, win_ref[:, d:]))
    mu = jnp.mean(vv, axis=-1, keepdims=True)
    vc = vv - mu
    var = jnp.mean(vc * vc, axis=-1, keepdims=True)
    vn = (vc * lax.rsqrt(var + EPS) * lng_ref[...] + lnb_ref[...]).astype(BF16)
    per_group = []
    for g in range(CM_GROUPS):
        rhs = jnp.concatenate(
            [vn[ch * MIX_CHUNK:(ch + 1) * MIX_CHUNK, g * gw:(g + 1) * gw] for ch in range(n_chunks)], axis=1)
        per_group.append(_dot(ws_ref[g], rhs) + bst_ref[:, g:g + 1])
    s = jnp.concatenate(
        [jnp.concatenate([pg[:, ch * gw:(ch + 1) * gw] for pg in per_group], axis=1) for ch in range(n_chunks)],
        axis=0)
    x1 = x + gt1 * _dot((u * s).astype(BF16), wo_ref[...])
    x2 = _mlp_tail(x1, mod_ref, g2_ref[...], w1_ref, w2_ref)
    y_ref[...] = _rms(x2, fg_ref[...]) if final else x2


def _cm_call(x, mod, mod_row0, mod_stride, g1, g2, fg, w_in, ln_g, ln_b, w_s, b_st, w_out, w1, w2, final):
    out_shape = x.shape
    if mod_stride == 0 and x.shape[1] % MIX_CHUNK == 0:
        x = x.reshape(1, -1, x.shape[-1])
    bn, l, d = x.shape
    tb = min(CM_TOKEN_BLOCK, l)
    assert l % tb == 0 and tb % MIX_CHUNK == 0
    tok = lambda b, j: (b, j, 0)
    consts = [g1, g2, fg, w_in, ln_g, ln_b, w_s, b_st, w_out, w1, w2]
    return pl.pallas_call(
        functools.partial(_cm_kernel, final=final),
        grid=(bn, l // tb),
        in_specs=[
            pl.BlockSpec((None, tb, d), tok),
            pl.BlockSpec((None, N_MOD, d), lambda b, j: (mod_row0 + mod_stride * b, 0, 0)),
        ] + [_const_spec(op) for op in consts],
        out_specs=pl.BlockSpec((None, tb, d), tok),
        out_shape=jax.ShapeDtypeStruct((bn, l, d), F32),
        compiler_params=pltpu.CompilerParams(
            dimension_semantics=("arbitrary", "arbitrary"), vmem_limit_bytes=VMEM_LIMIT),
        name="cm",
    )(x, mod, *[_const_arg(op) for op in consts]).reshape(out_shape)


def kernel(x_prompt, x_sample, state_hgrn, c, c_ctx, ada_w, ada_b, norm_mix_g, norm_mlp_g, mlp_w1, mlp_w2,
           hgrn_w_in, hgrn_lb_logits, hgrn_onorm_g, hgrn_w_out, cm_w_in, cm_ln_g, cm_ln_b, cm_w_s, cm_b_s,
           cm_w_out, final_norm_g):
    depth, d, _ = ada_w.shape
    n_lat = c.shape[0]
    assert d % PAIR_W == 0 and d // HEAD_W == HEADS and 1 + n_lat <= COND_ROWS
    assert x_prompt.shape[1] % SCAN_CHUNK == 0 and x_sample.shape[1] % SCAN_CHUNK == 0

    cond = jnp.concatenate([c_ctx[None, :], c, jnp.zeros((COND_ROWS - 1 - n_lat, d), F32)], axis=0)
    mods = _ada_call(cond, ada_w, ada_b).reshape(depth, COND_ROWS, N_MOD, d)

    row = lambda a: a.reshape(1, d)
    lbl = hgrn_lb_logits.reshape(-1, d)
    fg = row(final_norm_g)
    mlp_w1, mlp_w2, hgrn_w_in, hgrn_w_out, cm_w_in, cm_w_s, cm_w_out = (
        a.astype(BF16) for a in (mlp_w1, mlp_w2, hgrn_w_in, hgrn_w_out, cm_w_in, cm_w_s, cm_w_out))
    ctx, lat = x_prompt, x_sample
    new_states = []
    for i in range(depth):
        j = i // N_MIXERS
        final = i == depth - 1
        w1 = (mlp_w1, i)
        w2 = (mlp_w2, i)
        g1, g2 = row(norm_mix_g[i]), row(norm_mlp_g[i])
        if i % N_MIXERS == 0:
            w_in = (hgrn_w_in, j)
            w_out = (hgrn_w_out, j)
            onorm = row(hgrn_onorm_g[j])
            *stash, s_f = _hg_fwd_call(ctx, mods[i], 0, 0, g1, w_in, lbl, None, j)
            ctx, s_fb = _hg_bwd_call(ctx, mods[i], 0, 0, stash, s_f, onorm, g2, fg, w_out, w1, w2, None, j, final)
            new_states.append(s_fb)
            *stash, s_f = _hg_fwd_call(lat, mods[i], 1, 1, g1, w_in, lbl, state_hgrn, j)
            lat, _ = _hg_bwd_call(lat, mods[i], 1, 1, stash, s_f, onorm, g2, fg, w_out, w1, w2, state_hgrn, j, final)
        else:
            cm_args = (g1, g2, fg, (cm_w_in, j), row(cm_ln_g[j]), row(cm_ln_b[j]),
                       (cm_w_s, j), cm_b_s[j].T, (cm_w_out, j), w1, w2, final)
            ctx = _cm_call(ctx, mods[i], 0, 0, *cm_args)
            lat = _cm_call(lat, mods[i], 1, 1, *cm_args)
    new_state = jnp.stack(new_states, axis=1).astype(x_prompt.dtype)
    return ctx, lat, new_state
```

```python
import functools

import jax
import jax.numpy as jnp
from jax import lax
from jax.experimental import pallas as pl
from jax.experimental.pallas import tpu as pltpu

F32 = jnp.float32
BF16 = jnp.bfloat16

EPS = 1e-6
N_MOD = 6
N_MIXERS = 2
HEADS = 8
HEAD_W = 128
PAIR_W = 2 * HEAD_W
MIX_CHUNK = 128
CM_GROUPS = 8
SCAN_BLOCK = 64
SCAN_CHUNK = 2 * SCAN_BLOCK
SUBLANES = 8
BF16_SUBLANES = 16
COND_ROWS = SUBLANES
SAFE_LOG2_DECAY = -115.0
HG_TOKEN_BLOCK = 512
CM_TOKEN_BLOCK = 512
VMEM_BYTES_V7X = 64 * 1024 * 1024
VMEM_LIMIT = VMEM_BYTES_V7X * 15 // 16


def _silu(x):
    h = 0.5 * x
    return h * jnp.tanh(h) + h


def _forget_gate(z, lb):
    return 0.5 * (1.0 + lb) + (0.5 * (1.0 - lb)) * jnp.tanh(0.5 * z)


def _gelu_tanh(x):
    h = 0.5 * x
    t = jnp.tanh(x * (0.7978845608028654 + (0.7978845608028654 * 0.044715) * (x * x)))
    return h * t + h


def _rms(x, g):
    return x * lax.rsqrt(jnp.mean(x * x, axis=-1, keepdims=True) + EPS) * g


def _dot(a, b):
    return jnp.dot(a, b, preferred_element_type=F32)


def _dot_nt(a, b):
    return lax.dot_general(a, b, (((1,), (1,)), ((), ())), preferred_element_type=F32)


def _dot_tn(a, b):
    return lax.dot_general(a, b, (((0,), (0,)), ((), ())), preferred_element_type=F32)


def _ada_kernel(cond_ref, w_ref, b_ref, out_ref):
    a = _silu(cond_ref[...]).astype(BF16)
    out_ref[...] = _dot(a, w_ref[...].astype(BF16)) + b_ref[...]


def _ada_call(cond, ada_w, ada_b):
    depth, d, _ = ada_w.shape
    return pl.pallas_call(
        _ada_kernel,
        grid=(depth, N_MOD),
        in_specs=[
            pl.BlockSpec((COND_ROWS, d), lambda i, n: (0, 0)),
            pl.BlockSpec((None, d, d), lambda i, n: (i, 0, n)),
            pl.BlockSpec((None, 1, d), lambda i, n: (i, 0, n)),
        ],
        out_specs=pl.BlockSpec((None, COND_ROWS, d), lambda i, n: (i, 0, n)),
        out_shape=jax.ShapeDtypeStruct((depth, COND_ROWS, N_MOD * d), F32),
        compiler_params=pltpu.CompilerParams(dimension_semantics=("arbitrary", "arbitrary")),
        name="ada",
    )(cond, ada_w, ada_b.reshape(depth, 1, N_MOD * d))


def _scan_masks(rev):
    c = SCAN_CHUNK
    t = lax.broadcasted_iota(jnp.int32, (c, 2 * c), 0)
    s = lax.broadcasted_iota(jnp.int32, (c, 2 * c), 1) & (c - 1)
    causal = (s >= t) if rev else (s <= t)
    diag = ((t >= SCAN_BLOCK) == (s >= SCAN_BLOCK)) & causal
    s_row = lax.broadcasted_iota(jnp.int32, (1, 2 * c), 1) & (c - 1)
    near_keys = (s_row >= SCAN_BLOCK) if rev else (s_row < SCAN_BLOCK)
    return diag, near_keys


def _block_cumsum(g, rev):
    c = g.shape[0]
    t = lax.broadcasted_iota(jnp.int32, (c, 2 * c), 0)
    s = lax.broadcasted_iota(jnp.int32, (c, 2 * c), 1) & (c - 1)
    same = (t >= SCAN_BLOCK) == (s >= SCAN_BLOCK)
    tri = jnp.where(same & ((s >= t) if rev else (s <= t)), 1.0, 0.0).astype(BF16)
    g_hi = g.astype(BF16)
    g_lo = (g - g_hi.astype(F32)).astype(BF16)
    return _dot(tri, jnp.concatenate([g_hi, g_lo], axis=0))


def _scan_prep(q, k, g, rev):
    b_in = _block_cumsum(g, rev)
    b_lo, b_hi = b_in[:SCAN_BLOCK], b_in[SCAN_BLOCK:]

    def rows(lo_part, hi_part):
        return jnp.concatenate([lo_part, hi_part], axis=0)

    half = SCAN_BLOCK // 2
    if rev:
        tot_lo, tot_hi = b_lo[0:1], b_hi[0:1]
        mid_lo, mid_hi = b_lo[half:half + 1], b_hi[half:half + 1]
    else:
        tot_lo, tot_hi = b_lo[SCAN_BLOCK - 1:], b_hi[SCAN_BLOCK - 1:]
        mid_lo, mid_hi = b_lo[half - 1:half], b_hi[half - 1:half]

    qh = q * jnp.exp2(b_in)
    ke = k * jnp.exp2(rows(tot_lo - b_lo, tot_hi - b_hi))
    if rev:
        qi = rows(qh[:SCAN_BLOCK] * jnp.exp2(tot_hi), qh[SCAN_BLOCK:])
        ks = rows(ke[:SCAN_BLOCK], ke[SCAN_BLOCK:] * jnp.exp2(tot_lo))
    else:
        qi = rows(qh[:SCAN_BLOCK], qh[SCAN_BLOCK:] * jnp.exp2(tot_lo))
        ks = rows(ke[:SCAN_BLOCK] * jnp.exp2(tot_hi), ke[SCAN_BLOCK:])
    qi = qi.astype(BF16)
    ks = ks.astype(BF16)
    dec = jnp.exp2(tot_lo + tot_hi)
    qh16 = qh.astype(BF16)
    ke16 = ke.astype(BF16)
    qm16 = (q * jnp.exp2(rows(b_lo - mid_lo, b_hi - mid_hi))).astype(BF16)
    km16 = (k * jnp.exp2(rows(mid_lo - b_lo, mid_hi - b_hi))).astype(BF16)

    half_sums = jnp.minimum(jnp.minimum(mid_lo, tot_lo - mid_lo), jnp.minimum(mid_hi, tot_hi - mid_hi))
    fast_vec = jnp.min(half_sums, axis=-1, keepdims=True) >= SAFE_LOG2_DECAY
    fast = jnp.min(half_sums) >= SAFE_LOG2_DECAY

    return (qm16, km16, qh16, ke16, qi, ks, dec, fast_vec), fast


def _scan_finish(operands, v16, st_ref, rev):
    qm16, km16, qh16, ke16, qi, ks, dec, fast_vec = operands
    c = qi.shape[0]
    n_pairs = qi.shape[1] // PAIR_W
    diag_mask, near_keys = _scan_masks(rev)
    diag_mask = diag_mask & fast_vec
    lo, hi = slice(0, SCAN_BLOCK), slice(SCAN_BLOCK, c)
    near, far = (hi, lo) if rev else (lo, hi)
    left = lax.broadcasted_iota(jnp.int32, (1, PAIR_W), 1) < HEAD_W

    def by_head(a):
        zero = jnp.zeros_like(a)
        return jnp.concatenate([jnp.where(left, a, zero), jnp.where(left, zero, a)], axis=0)

    outs = []
    for pr in range(n_pairs):
        sl = slice(pr * PAIR_W, (pr + 1) * PAIR_W)
        attd = _dot_nt(qm16[:, sl], by_head(km16[:, sl]))
        attx = _dot_nt(qh16[far, sl], by_head(ke16[:, sl]))
        att_near = jnp.where(diag_mask[near], attd[near], 0.0)
        att_far = jnp.where(diag_mask[far], attd[far], jnp.where(near_keys, attx, 0.0))
        att = jnp.concatenate([att_far, att_near] if rev else [att_near, att_far], axis=0)
        st = st_ref[pr]
        v_p = v16[:, sl]
        o = _dot(att.astype(BF16), by_head(v_p)) + _dot_nt(qi[:, sl], by_head(st.astype(BF16)))
        outs.append(o)
        v_stack = jnp.concatenate([v_p[:, :HEAD_W], v_p[:, HEAD_W:]], axis=0)
        st_ref[pr] = st * dec[:, sl] + _dot_tn(v_stack, by_head(ks[:, sl]))
    return jnp.concatenate(outs, axis=1)


def _scan_slow_fix(q_ref, v_ref, f_ref, rows, o_ref, od_ref, slow_refs, rev):
    q_s, k_s, v_s, b_s = slow_refs
    c, d = od_ref.shape
    f = f_ref[rows, :]
    q_s[...] = q_ref[rows, :].astype(F32)
    k_s[...] = 1.0 - f
    v_s[...] = v_ref[rows, :].astype(F32)
    b_s[...] = _block_cumsum(jnp.log2(f), rev)
    row = lax.broadcasted_iota(jnp.int32, (c, 1), 0)

    def body(grp, carry):
        t0 = pl.multiple_of(grp * SUBLANES, SUBLANES)
        q8 = q_s[pl.ds(t0, SUBLANES), :]
        b8 = b_s[pl.ds(t0, SUBLANES), :]
        out_rows = []
        for r in range(SUBLANES):
            ti = t0 + r
            prod = q8[r:r + 1] * k_s[...] * jnp.exp2(jnp.minimum(b8[r:r + 1] - b_s[...], 0.0))
            valid = ((row >= SCAN_BLOCK) == (ti >= SCAN_BLOCK)) & ((row >= ti) if rev else (row <= ti))
            heads = []
            for h in range(d // HEAD_W):
                hs = slice(h * HEAD_W, (h + 1) * HEAD_W)
                w = jnp.where(valid, jnp.sum(prod[:, hs], axis=-1, keepdims=True), 0.0)
                heads.append(jnp.sum(w * v_s[:, hs], axis=0, keepdims=True))
            out_rows.append(jnp.concatenate(heads, axis=1))
        od_ref[pl.ds(t0, SUBLANES), :] = jnp.concatenate(out_rows, axis=0)
        return carry

    lax.fori_loop(0, c // SUBLANES, body, 0)
    o_ref[rows, :] += od_ref[...]


def _scan_slow_fixes(fast_ref, q_ref, v_ref, f_ref, o_ref, od_ref, slow_refs, rev):
    def body(ci, carry):
        @pl.when(fast_ref[ci] == 0)
        def _():
            rows = pl.ds(pl.multiple_of(ci * SCAN_CHUNK, SCAN_CHUNK), SCAN_CHUNK)
            _scan_slow_fix(q_ref, v_ref, f_ref, rows, o_ref, od_ref, slow_refs, rev)
        return carry

    lax.fori_loop(0, fast_ref.shape[0], body, 0)


def _state_init(st_ref, s0_ref):
    if s0_ref is None:
        st_ref[...] = jnp.zeros_like(st_ref)
    else:
        for h in range(HEADS):
            o = (h % 2) * HEAD_W
            st_ref[h // 2, :, o:o + HEAD_W] = s0_ref[h].T


def _state_write(st_ref, out_ref):
    for h in range(HEADS):
        o = (h % 2) * HEAD_W
        out_ref[h] = st_ref[h // 2, :, o:o + HEAD_W].T


def _lower_bound(lbl_ref, layer, direction):
    n_layers = lbl_ref.shape[0] // 2
    rows = [lbl_ref[2 * i + direction:2 * i + direction + 1, :] for i in range(n_layers)]
    m = functools.reduce(jnp.maximum, rows)
    ex = [jnp.exp(r - m) for r in rows]
    return sum(ex[:layer + 1]) / sum(ex)


def _chunk_rows(n_rows, rev):
    order = range(n_rows // SCAN_CHUNK)
    return [slice(ci * SCAN_CHUNK, (ci + 1) * SCAN_CHUNK) for ci in (reversed(order) if rev else order)]


def _scan_scratch(d):
    c = SCAN_CHUNK
    return [
        pltpu.VMEM((d // PAIR_W, HEAD_W, PAIR_W), F32),
        pltpu.VMEM((c, d), F32),
        pltpu.VMEM((c, d), F32), pltpu.VMEM((c, d), F32),
        pltpu.VMEM((c, d), F32), pltpu.VMEM((c, d), F32),
    ]


def _const_spec(op):
    if isinstance(op, tuple):
        arr, i = op
        return pl.BlockSpec((None,) + arr.shape[1:], lambda *_: (i,) + (0,) * (arr.ndim - 1),
                            pipeline_mode=pl.Buffered(1))
    return pl.BlockSpec(op.shape, lambda *_: (0,) * op.ndim, pipeline_mode=pl.Buffered(1))


def _const_arg(op):
    return op[0] if isinstance(op, tuple) else op


def _hg_fwd_kernel(*refs, has_s0, layer, n_cast):
    if has_s0:
        x_ref, mod_ref, g_ref, w_ref, lbl_ref, s0_ref = refs[:6]
        rest = refs[6:]
    else:
        x_ref, mod_ref, g_ref, w_ref, lbl_ref = refs[:5]
        s0_ref = None
        rest = refs[5:]
    cast_in, rest = rest[:n_cast], rest[n_cast:]
    of_ref, q_ref, v_ref, fb_ref, gate_ref, sf_ref = rest[:6]
    cast_out, rest = rest[6:6 + n_cast], rest[6 + n_cast:]
    st_ref, od_ref = rest[0:2]
    slow_refs = rest[2:6]
    ff_ref, fast_ref = rest[6:8]
    j = pl.program_id(1)
    d = x_ref.shape[-1]
    for src_ref, dst_ref in zip(cast_in, cast_out):
        dst_ref[...] = src_ref[...].astype(BF16)

    @pl.when(j == 0)
    def _():
        _state_init(st_ref, s0_ref)

    sh, sc = mod_ref[0:1, :], mod_ref[1:2, :]
    h = (_rms(x_ref[...], g_ref[...]) * (1.0 + sc) + sh).astype(BF16)
    proj = lambda i: _dot(h, w_ref[:, i * d:(i + 1) * d])
    lb_f = _lower_bound(lbl_ref, layer, 0)
    lb_b = _lower_bound(lbl_ref, layer, 1)
    q = _silu(proj(0))
    f_f = _forget_gate(proj(1), lb_f)
    q_ref[...] = q.astype(BF16)
    v_ref[...] = proj(3).astype(BF16)
    ff_ref[...] = f_f
    k_f = 1.0 - f_f
    g_f = jnp.log2(f_f)

    chunks = _chunk_rows(x_ref.shape[0], rev=False)
    n_pieces = max(1, len(chunks) // 2)
    piece_w = d // n_pieces

    def backward_gate(p):
        cols = slice(p * piece_w, (p + 1) * piece_w)
        z = _dot(h, w_ref[:, 2 * d + p * piece_w:2 * d + (p + 1) * piece_w])
        fb_ref[:, cols] = _forget_gate(z, lb_b[:, cols])

    def output_gate(p):
        cols = slice(p * piece_w, (p + 1) * piece_w)
        z = _dot(h, w_ref[:, 4 * d + p * piece_w:4 * d + (p + 1) * piece_w])
        gate_ref[:, cols] = _silu(z).astype(BF16)

    fillers = [functools.partial(f, p) for f in (backward_gate, output_gate) for p in range(n_pieces)]
    for rows in chunks:
        operands, fast = _scan_prep(q[rows], k_f[rows], g_f[rows], rev=False)
        fast_ref[rows.start // SCAN_CHUNK] = fast.astype(jnp.int32)
        if fillers:
            fillers.pop(0)()
        of_ref[rows, :] = _scan_finish(operands, v_ref[rows, :], st_ref, rev=False)
    for filler in fillers:
        filler()
    _scan_slow_fixes(fast_ref, q_ref, v_ref, ff_ref, of_ref, od_ref, slow_refs, rev=False)

    @pl.when(j == pl.num_programs(1) - 1)
    def _():
        _state_write(st_ref, sf_ref)


def _hg_fwd_call(x, mod, mod_row0, mod_stride, g, w_in, lbl, state, layer_a, casts):
    bn, l, d = x.shape
    c = min(HG_TOKEN_BLOCK, l)
    assert l % c == 0 and c % SCAN_CHUNK == 0
    n = l // c
    tok = lambda b, j: (b, j, 0)
    steps = bn * n
    cast_args, cast_in_specs, cast_out_specs, cast_out_shapes = [], [], [], []
    for w, w_layer in casts:
        _, rows, cols = w.shape
        blk = rows // steps
        assert rows % steps == 0 and blk % BF16_SUBLANES == 0
        cast_args.append(w.reshape(w.shape[0], steps, blk, cols))
        cast_in_specs.append(pl.BlockSpec((None, None, blk, cols),
                                          lambda b, j, w_layer=w_layer: (w_layer, b * n + j, 0, 0)))
        cast_out_specs.append(pl.BlockSpec((None, blk, cols), lambda b, j: (b * n + j, 0, 0)))
        cast_out_shapes.append(jax.ShapeDtypeStruct((steps, blk, cols), BF16))
    in_specs = [
        pl.BlockSpec((None, c, d), tok),
        pl.BlockSpec((None, N_MOD, d), lambda b, j: (mod_row0 + mod_stride * b, 0, 0)),
        _const_spec(g), _const_spec(w_in), _const_spec(lbl),
    ]
    args = [x, mod, g, _const_arg(w_in), lbl]
    if state is not None:
        in_specs.append(pl.BlockSpec((None, None, None, HEADS, HEAD_W, HEAD_W),
                                     lambda b, j: (b, layer_a, 0, 0, 0, 0)))
        args.append(state)
    st_spec = pl.BlockSpec((None, HEADS, HEAD_W, HEAD_W), lambda b, j: (b, 0, 0, 0))
    outs = pl.pallas_call(
        functools.partial(_hg_fwd_kernel, has_s0=state is not None, layer=layer_a, n_cast=len(casts)),
        grid=(bn, n),
        in_specs=in_specs + cast_in_specs,
        out_specs=[pl.BlockSpec((None, c, d), tok)] * 5 + [st_spec] + cast_out_specs,
        out_shape=[
            jax.ShapeDtypeStruct((bn, l, d), F32),
            jax.ShapeDtypeStruct((bn, l, d), BF16),
            jax.ShapeDtypeStruct((bn, l, d), BF16),
            jax.ShapeDtypeStruct((bn, l, d), F32),
            jax.ShapeDtypeStruct((bn, l, d), BF16),
            jax.ShapeDtypeStruct((bn, HEADS, HEAD_W, HEAD_W), F32),
        ] + cast_out_shapes,
        scratch_shapes=_scan_scratch(d) + [
            pltpu.VMEM((c, d), F32),
            pltpu.SMEM((c // SCAN_CHUNK,), jnp.int32),
        ],
        compiler_params=pltpu.CompilerParams(
            dimension_semantics=("arbitrary", "arbitrary"), vmem_limit_bytes=VMEM_LIMIT),
        name="hg_fwd",
    )(*args, *cast_args)
    cast_w = [o.reshape(w.shape[1:]) for o, (w, _) in zip(outs[6:], casts)]
    return list(outs[:6]) + cast_w


def _mlp_tail(x1, mod_ref, g2, w1_ref, w2_ref):
    sh2, sc2, gt2 = mod_ref[3:4, :], mod_ref[4:5, :], mod_ref[5:6, :]
    h2 = (_rms(x1, g2) * (1.0 + sc2) + sh2).astype(BF16)
    a = jnp.maximum(_dot(h2, w1_ref[...]), 0.0)
    return x1 + gt2 * _dot((a * a).astype(BF16), w2_ref[...])


def _hg_bwd_kernel(*refs, has_s0, final, n_blocks):
    x_ref, mod_ref, gate_ref, of_ref, q_ref, v_ref, fb_ref, on_ref, g2_ref, fg_ref = refs[:10]
    wo_ref, w1_ref, w2_ref, sf_ref = refs[10:14]
    if has_s0:
        s0_ref = refs[14]
        rest = refs[15:]
    else:
        s0_ref = None
        rest = refs[14:]
    y_ref, sfb_ref = rest[:2]
    st_ref, od_ref = rest[2:4]
    slow_refs = rest[4:8]
    o_ref, fast_ref = rest[8:10]
    s = pl.program_id(0)
    total = pl.num_programs(0) - 1
    live = s < total
    j = jnp.minimum(s, total - 1) % n_blocks
    d = x_ref.shape[-1]

    @pl.when(live & (j == 0))
    def _():
        _state_init(st_ref, s0_ref)

    chunk_rows = _chunk_rows(q_ref.shape[0], rev=True)
    n_slabs = len(chunk_rows)
    slab_w = w1_ref.shape[1] // n_slabs

    def step(finishing):
        if finishing:
            o = o_ref[...]
            parts = []
            for h in range(d // HEAD_W):
                oh = o[:, h * HEAD_W:(h + 1) * HEAD_W]
                parts.append(oh * lax.rsqrt(jnp.mean(oh * oh, axis=-1, keepdims=True) + EPS))
            o = jnp.concatenate(parts, axis=1) * on_ref[...] * gate_ref[...].astype(F32)
            gt1 = mod_ref[2:3, :]
            x1 = x_ref[...] + gt1 * _dot(o.astype(BF16), wo_ref[...])
            sh2, sc2, gt2 = mod_ref[3:4, :], mod_ref[4:5, :], mod_ref[5:6, :]
            h2 = (_rms(x1, g2_ref[...]) * (1.0 + sc2) + sh2).astype(BF16)

        def mlp_slab(i):
            a = jnp.maximum(_dot(h2, w1_ref[:, i * slab_w:(i + 1) * slab_w]), 0.0)
            return _dot((a * a).astype(BF16), w2_ref[i * slab_w:(i + 1) * slab_w, :])

        mlp = 0.0
        for ci, rows in enumerate(chunk_rows):
            f_b = fb_ref[rows, :]
            operands, fast = _scan_prep(q_ref[rows, :].astype(F32), 1.0 - f_b, jnp.log2(f_b), rev=True)
            fast_ref[rows.start // SCAN_CHUNK] = fast.astype(jnp.int32)
            if finishing:
                mlp = mlp + mlp_slab(ci)
            o_ref[rows, :] = of_ref[rows, :] + _scan_finish(operands, v_ref[rows, :], st_ref, rev=True)
        if finishing:
            x2 = x1 + gt2 * mlp
            y_ref[...] = _rms(x2, fg_ref[...]) if final else x2

    pl.when(s == 0)(lambda: step(finishing=False))
    pl.when(s > 0)(lambda: step(finishing=True))
    _scan_slow_fixes(fast_ref, q_ref, v_ref, fb_ref, o_ref, od_ref, slow_refs, rev=True)

    @pl.when(live & (j == n_blocks - 1))
    def _():
        sfb_ref[0] = sf_ref[...]
        _state_write(st_ref, sfb_ref.at[1])


def _hg_bwd_call(x, mod, mod_row0, mod_stride, stash, s_f, onorm, g2, fg, w_out, w1, w2, state, layer_a, final):
    bn, l, d = x.shape
    c = min(HG_TOKEN_BLOCK, l)
    assert l % c == 0 and c % SCAN_CHUNK == 0
    n = l // c
    total = bn * n

    def cur(s):
        t = jnp.minimum(s, total - 1)
        return t // n, n - 1 - t % n

    def prev(s):
        t = jnp.maximum(s - 1, 0)
        return t // n, n - 1 - t % n

    tok_cur = lambda s: (*cur(s), 0)
    tok_prev = lambda s: (*prev(s), 0)
    o_f, q, v, f_b, gate = stash
    consts = [onorm, g2, fg, w_out, w1, w2]
    in_specs = [
        pl.BlockSpec((None, c, d), tok_prev),
        pl.BlockSpec((None, N_MOD, d), lambda s: (mod_row0 + mod_stride * prev(s)[0], 0, 0)),
        pl.BlockSpec((None, c, d), tok_prev),
    ] + [pl.BlockSpec((None, c, d), tok_cur)] * 4 + [_const_spec(op) for op in consts]
    in_specs.append(pl.BlockSpec((None, HEADS, HEAD_W, HEAD_W), lambda s: (cur(s)[0], 0, 0, 0)))
    args = [x, mod, gate, o_f, q, v, f_b] + [_const_arg(op) for op in consts] + [s_f]
    if state is not None:
        in_specs.append(pl.BlockSpec((None, None, None, HEADS, HEAD_W, HEAD_W),
                                     lambda s: (cur(s)[0], layer_a, 1, 0, 0, 0)))
        args.append(state)
    st_spec = pl.BlockSpec((None, 2, HEADS, HEAD_W, HEAD_W), lambda s: (cur(s)[0], 0, 0, 0, 0))
    return pl.pallas_call(
        functools.partial(_hg_bwd_kernel, has_s0=state is not None, final=final, n_blocks=n),
        grid=(total + 1,),
        in_specs=in_specs,
        out_specs=[pl.BlockSpec((None, c, d), tok_prev), st_spec],
        out_shape=[
            jax.ShapeDtypeStruct((bn, l, d), F32),
            jax.ShapeDtypeStruct((bn, 2, HEADS, HEAD_W, HEAD_W), F32),
        ],
        scratch_shapes=_scan_scratch(d) + [
            pltpu.VMEM((c, d), F32),
            pltpu.SMEM((c // SCAN_CHUNK,), jnp.int32),
        ],
        compiler_params=pltpu.CompilerParams(
            dimension_semantics=("arbitrary",), vmem_limit_bytes=VMEM_LIMIT),
        name="hg_bwd",
    )(*args)


def _cm_kernel(x_ref, mod_ref, g1_ref, g2_ref, fg_ref, win_ref, lng_ref, lnb_ref, ws_ref, bst_ref,
               wo_ref, w1_ref, w2_ref, y_ref, *, final):
    tb, d = x_ref.shape
    n_chunks = tb // MIX_CHUNK
    gw = d // CM_GROUPS
    x = x_ref[...]
    sh, sc, gt1 = mod_ref[0:1, :], mod_ref[1:2, :], mod_ref[2:3, :]
    h = (_rms(x, g1_ref[...]) * (1.0 + sc) + sh).astype(BF16)
    u = _gelu_tanh(_dot(h, win_ref[:, :d]))
    vv = _gelu_tanh(_dot(h, win_ref[:, d:]))
    mu = jnp.mean(vv, axis=-1, keepdims=True)
    vc = vv - mu
    var = jnp.mean(vc * vc, axis=-1, keepdims=True)
    vn = (vc * lax.rsqrt(var + EPS) * lng_ref[...] + lnb_ref[...]).astype(BF16)
    per_group = []
    for g in range(CM_GROUPS):
        rhs = jnp.concatenate(
            [vn[ch * MIX_CHUNK:(ch + 1) * MIX_CHUNK, g * gw:(g + 1) * gw] for ch in range(n_chunks)], axis=1)
        per_group.append(_dot(ws_ref[g], rhs) + bst_ref[:, g:g + 1])
    s = jnp.concatenate(
        [jnp.concatenate([pg[:, ch * gw:(ch + 1) * gw] for pg in per_group], axis=1) for ch in range(n_chunks)],
        axis=0)
    x1 = x + gt1 * _dot((u * s).astype(BF16), wo_ref[...])
    x2 = _mlp_tail(x1, mod_ref, g2_ref[...], w1_ref, w2_ref)
    y_ref[...] = _rms(x2, fg_ref[...]) if final else x2


def _cm_call(x, mod, mod_row0, mod_stride, g1, g2, fg, w_in, ln_g, ln_b, w_s, b_st, w_out, w1, w2, final):
    out_shape = x.shape
    if mod_stride == 0 and x.shape[1] % MIX_CHUNK == 0:
        x = x.reshape(1, -1, x.shape[-1])
    bn, l, d = x.shape
    tb = min(CM_TOKEN_BLOCK, l)
    assert l % tb == 0 and tb % MIX_CHUNK == 0
    tok = lambda b, j: (b, j, 0)
    consts = [g1, g2, fg, w_in, ln_g, ln_b, w_s, b_st, w_out, w1, w2]
    return pl.pallas_call(
        functools.partial(_cm_kernel, final=final),
        grid=(bn, l // tb),
        in_specs=[
            pl.BlockSpec((None, tb, d), tok),
            pl.BlockSpec((None, N_MOD, d), lambda b, j: (mod_row0 + mod_stride * b, 0, 0)),
        ] + [_const_spec(op) for op in consts],
        out_specs=pl.BlockSpec((None, tb, d), tok),
        out_shape=jax.ShapeDtypeStruct((bn, l, d), F32),
        compiler_params=pltpu.CompilerParams(
            dimension_semantics=("arbitrary", "arbitrary"), vmem_limit_bytes=VMEM_LIMIT),
        name="cm",
    )(x, mod, *[_const_arg(op) for op in consts]).reshape(out_shape)


def kernel(x_prompt, x_sample, state_hgrn, c, c_ctx, ada_w, ada_b, norm_mix_g, norm_mlp_g, mlp_w1, mlp_w2,
           hgrn_w_in, hgrn_lb_logits, hgrn_onorm_g, hgrn_w_out, cm_w_in, cm_ln_g, cm_ln_b, cm_w_s, cm_b_s,
           cm_w_out, final_norm_g):
    depth, d, _ = ada_w.shape
    n_lat = c.shape[0]
    assert d % PAIR_W == 0 and d // HEAD_W == HEADS and 1 + n_lat <= COND_ROWS
    assert x_prompt.shape[1] % SCAN_CHUNK == 0 and x_sample.shape[1] % SCAN_CHUNK == 0

    cond = jnp.concatenate([c_ctx[None, :], c, jnp.zeros((COND_ROWS - 1 - n_lat, d), F32)], axis=0)
    mods = _ada_call(cond, ada_w, ada_b).reshape(depth, COND_ROWS, N_MOD, d)

    row = lambda a: a.reshape(1, d)
    lbl = hgrn_lb_logits.reshape(-1, d)
    fg = row(final_norm_g)
    stacks = dict(mlp_w1=mlp_w1, mlp_w2=mlp_w2, hgrn_w_out=hgrn_w_out, cm_w_in=cm_w_in, cm_w_out=cm_w_out)
    ready = {}

    def weight(name, layer):
        if (name, layer) in ready:
            return ready[name, layer]
        if not isinstance(stacks[name], tuple):
            stacks[name] = (stacks[name].astype(BF16),)
        return (stacks[name][0], layer)

    def side_casts(call, names_layers):
        names_layers = [(nm, ly) for nm, ly in names_layers if not isinstance(stacks[nm], tuple)]
        outs = call([(stacks[nm], ly) for nm, ly in names_layers])
        for (nm, ly), w16 in zip(names_layers, outs[6:]):
            ready[nm, ly] = w16
        return outs[:6]

    hgrn_w_in, cm_w_s = hgrn_w_in.astype(BF16), cm_w_s.astype(BF16)
    ctx, lat = x_prompt, x_sample
    new_states = []
    for i in range(depth):
        j = i // N_MIXERS
        final = i == depth - 1
        g1, g2 = row(norm_mix_g[i]), row(norm_mlp_g[i])
        if i % N_MIXERS == 0:
            w_in = (hgrn_w_in, j)
            onorm = row(hgrn_onorm_g[j])
            own = [("hgrn_w_out", j), ("mlp_w1", i), ("mlp_w2", i)]
            nxt = []
            if i + 1 < depth and (i + 1) % N_MIXERS == 1:
                jn = (i + 1) // N_MIXERS
                nxt = [("cm_w_in", jn), ("cm_w_out", jn), ("mlp_w1", i + 1), ("mlp_w2", i + 1)]
            *stash, s_f = side_casts(
                functools.partial(_hg_fwd_call, ctx, mods[i], 0, 0, g1, w_in, lbl, None, j), own)
            w_out, w1, w2 = weight("hgrn_w_out", j), weight("mlp_w1", i), weight("mlp_w2", i)
            ctx, s_fb = _hg_bwd_call(ctx, mods[i], 0, 0, stash, s_f, onorm, g2, fg, w_out, w1, w2, None, j, final)
            new_states.append(s_fb)
            *stash, s_f = side_casts(
                functools.partial(_hg_fwd_call, lat, mods[i], 1, 1, g1, w_in, lbl, state_hgrn, j), nxt)
            lat, _ = _hg_bwd_call(lat, mods[i], 1, 1, stash, s_f, onorm, g2, fg, w_out, w1, w2, state_hgrn, j, final)
        else:
            cm_args = (g1, g2, fg, weight("cm_w_in", j), row(cm_ln_g[j]), row(cm_ln_b[j]),
                       (cm_w_s, j), cm_b_s[j].T, weight("cm_w_out", j), weight("mlp_w1", i), weight("mlp_w2", i),
                       final)
            ctx = _cm_call(ctx, mods[i], 0, 0, *cm_args)
            lat = _cm_call(lat, mods[i], 1, 1, *cm_args)
    new_state = jnp.stack(new_states, axis=1).astype(x_prompt.dtype)
    return ctx, lat, new_state
```

```python
import functools

import jax
import jax.numpy as jnp
from jax import lax
from jax.experimental import pallas as pl
from jax.experimental.pallas import tpu as pltpu

F32 = jnp.float32
BF16 = jnp.bfloat16

EPS = 1e-6
N_MOD = 6
N_MIXERS = 2
HEADS = 8
HEAD_W = 128
PAIR_W = 2 * HEAD_W
MIX_CHUNK = 128
CM_GROUPS = 8
SCAN_BLOCK = 64
SCAN_CHUNK = 2 * SCAN_BLOCK
SUBLANES = 8
BF16_SUBLANES = 16
COND_ROWS = SUBLANES
SAFE_LOG2_DECAY = -100.0
HG_TOKEN_BLOCK = 512
CM_TOKEN_BLOCK = 512
VMEM_BYTES_V7X = 64 * 1024 * 1024
VMEM_LIMIT = VMEM_BYTES_V7X * 15 // 16


def _silu(x):
    h = 0.5 * x
    return h * jnp.tanh(h) + h


def _forget_gate(z, lb):
    return 0.5 * (1.0 + lb) + (0.5 * (1.0 - lb)) * jnp.tanh(0.5 * z)


def _gelu_tanh(x):
    h = 0.5 * x
    t = jnp.tanh(x * (0.7978845608028654 + (0.7978845608028654 * 0.044715) * (x * x)))
    return h * t + h


def _rms(x, g):
    return x * lax.rsqrt(jnp.mean(x * x, axis=-1, keepdims=True) + EPS) * g


def _dot(a, b):
    return jnp.dot(a, b, preferred_element_type=F32)


def _dot_nt(a, b):
    return lax.dot_general(a, b, (((1,), (1,)), ((), ())), preferred_element_type=F32)


def _dot_tn(a, b):
    return lax.dot_general(a, b, (((0,), (0,)), ((), ())), preferred_element_type=F32)


def _ada_kernel(cond_ref, w_ref, b_ref, out_ref):
    a = _silu(cond_ref[...]).astype(BF16)
    out_ref[...] = _dot(a, w_ref[...].astype(BF16)) + b_ref[...]


def _ada_call(cond, ada_w, ada_b):
    depth, d, _ = ada_w.shape
    return pl.pallas_call(
        _ada_kernel,
        grid=(depth, N_MOD),
        in_specs=[
            pl.BlockSpec((COND_ROWS, d), lambda i, n: (0, 0)),
            pl.BlockSpec((None, d, d), lambda i, n: (i, 0, n)),
            pl.BlockSpec((None, 1, d), lambda i, n: (i, 0, n)),
        ],
        out_specs=pl.BlockSpec((None, COND_ROWS, d), lambda i, n: (i, 0, n)),
        out_shape=jax.ShapeDtypeStruct((depth, COND_ROWS, N_MOD * d), F32),
        compiler_params=pltpu.CompilerParams(dimension_semantics=("arbitrary", "arbitrary")),
        name="ada",
    )(cond, ada_w, ada_b.reshape(depth, 1, N_MOD * d))


def _scan_masks(rev):
    c = SCAN_CHUNK
    t = lax.broadcasted_iota(jnp.int32, (c, 2 * c), 0)
    s = lax.broadcasted_iota(jnp.int32, (c, 2 * c), 1) & (c - 1)
    causal = (s >= t) if rev else (s <= t)
    diag = ((t >= SCAN_BLOCK) == (s >= SCAN_BLOCK)) & causal
    s_row = lax.broadcasted_iota(jnp.int32, (1, 2 * c), 1) & (c - 1)
    near_keys = (s_row >= SCAN_BLOCK) if rev else (s_row < SCAN_BLOCK)
    return diag, near_keys


def _block_cumsum(g, rev):
    c = g.shape[0]
    t = lax.broadcasted_iota(jnp.int32, (c, 2 * c), 0)
    s = lax.broadcasted_iota(jnp.int32, (c, 2 * c), 1) & (c - 1)
    same = (t >= SCAN_BLOCK) == (s >= SCAN_BLOCK)
    tri = jnp.where(same & ((s >= t) if rev else (s <= t)), 1.0, 0.0).astype(BF16)
    g_hi = g.astype(BF16)
    g_lo = (g - g_hi.astype(F32)).astype(BF16)
    return _dot(tri, jnp.concatenate([g_hi, g_lo], axis=0))


def _scan_prep(q, k, g, rev):
    b_in = _block_cumsum(g, rev)
    b_lo, b_hi = b_in[:SCAN_BLOCK], b_in[SCAN_BLOCK:]

    def rows(lo_part, hi_part):
        return jnp.concatenate([lo_part, hi_part], axis=0)

    half = SCAN_BLOCK // 2
    if rev:
        tot_lo, tot_hi = b_lo[0:1], b_hi[0:1]
        mid_lo, mid_hi = b_lo[half:half + 1], b_hi[half:half + 1]
    else:
        tot_lo, tot_hi = b_lo[SCAN_BLOCK - 1:], b_hi[SCAN_BLOCK - 1:]
        mid_lo, mid_hi = b_lo[half - 1:half], b_hi[half - 1:half]

    qh = q * jnp.exp2(b_in)
    ke = k * jnp.exp2(rows(tot_lo - b_lo, tot_hi - b_hi))
    if rev:
        qi = rows(qh[:SCAN_BLOCK] * jnp.exp2(tot_hi), qh[SCAN_BLOCK:])
        ks = rows(ke[:SCAN_BLOCK], ke[SCAN_BLOCK:] * jnp.exp2(tot_lo))
    else:
        qi = rows(qh[:SCAN_BLOCK], qh[SCAN_BLOCK:] * jnp.exp2(tot_lo))
        ks = rows(ke[:SCAN_BLOCK] * jnp.exp2(tot_hi), ke[SCAN_BLOCK:])
    qi = qi.astype(BF16)
    ks = ks.astype(BF16)
    dec = jnp.exp2(tot_lo + tot_hi)
    qh16 = qh.astype(BF16)
    ke16 = ke.astype(BF16)
    qm16 = (q * jnp.exp2(rows(b_lo - mid_lo, b_hi - mid_hi))).astype(BF16)
    km16 = (k * jnp.exp2(rows(mid_lo - b_lo, mid_hi - b_hi))).astype(BF16)

    half_sums = jnp.minimum(jnp.minimum(mid_lo, tot_lo - mid_lo), jnp.minimum(mid_hi, tot_hi - mid_hi))
    fast_vec = jnp.min(half_sums, axis=-1, keepdims=True) >= SAFE_LOG2_DECAY
    fast = jnp.min(half_sums) >= SAFE_LOG2_DECAY

    return (qm16, km16, qh16, ke16, qi, ks, dec, fast_vec), fast


def _scan_finish(operands, v16, st_ref, rev):
    qm16, km16, qh16, ke16, qi, ks, dec, fast_vec = operands
    c = qi.shape[0]
    n_pairs = qi.shape[1] // PAIR_W
    diag_mask, near_keys = _scan_masks(rev)
    diag_mask = diag_mask & fast_vec
    lo, hi = slice(0, SCAN_BLOCK), slice(SCAN_BLOCK, c)
    near, far = (hi, lo) if rev else (lo, hi)
    left = lax.broadcasted_iota(jnp.int32, (1, PAIR_W), 1) < HEAD_W

    def by_head(a):
        zero = jnp.zeros_like(a)
        return jnp.concatenate([jnp.where(left, a, zero), jnp.where(left, zero, a)], axis=0)

    outs = []
    for pr in range(n_pairs):
        sl = slice(pr * PAIR_W, (pr + 1) * PAIR_W)
        attd = _dot_nt(qm16[:, sl], by_head(km16[:, sl]))
        attx = _dot_nt(qh16[far, sl], by_head(ke16[:, sl]))
        att_near = jnp.where(diag_mask[near], attd[near], 0.0)
        att_far = jnp.where(diag_mask[far], attd[far], jnp.where(near_keys, attx, 0.0))
        att = jnp.concatenate([att_far, att_near] if rev else [att_near, att_far], axis=0)
        st = st_ref[pr]
        v_p = v16[:, sl]
        o = _dot(att.astype(BF16), by_head(v_p)) + _dot_nt(qi[:, sl], by_head(st.astype(BF16)))
        outs.append(o)
        v_stack = jnp.concatenate([v_p[:, :HEAD_W], v_p[:, HEAD_W:]], axis=0)
        st_ref[pr] = st * dec[:, sl] + _dot_tn(v_stack, by_head(ks[:, sl]))
    return jnp.concatenate(outs, axis=1)


def _scan_slow_fix(q_ref, v_ref, f_ref, rows, o_ref, od_ref, slow_refs, rev):
    q_s, k_s, v_s, b_s = slow_refs
    c, d = od_ref.shape
    f = f_ref[rows, :]
    q_s[...] = q_ref[rows, :].astype(F32)
    k_s[...] = 1.0 - f
    v_s[...] = v_ref[rows, :].astype(F32)
    b_s[...] = _block_cumsum(jnp.log2(f), rev)
    row = lax.broadcasted_iota(jnp.int32, (c, 1), 0)

    def body(grp, carry):
        t0 = pl.multiple_of(grp * SUBLANES, SUBLANES)
        q8 = q_s[pl.ds(t0, SUBLANES), :]
        b8 = b_s[pl.ds(t0, SUBLANES), :]
        out_rows = []
        for r in range(SUBLANES):
            ti = t0 + r
            prod = q8[r:r + 1] * k_s[...] * jnp.exp2(jnp.minimum(b8[r:r + 1] - b_s[...], 0.0))
            valid = ((row >= SCAN_BLOCK) == (ti >= SCAN_BLOCK)) & ((row >= ti) if rev else (row <= ti))
            heads = []
            for h in range(d // HEAD_W):
                hs = slice(h * HEAD_W, (h + 1) * HEAD_W)
                w = jnp.where(valid, jnp.sum(prod[:, hs], axis=-1, keepdims=True), 0.0)
                heads.append(jnp.sum(w * v_s[:, hs], axis=0, keepdims=True))
            out_rows.append(jnp.concatenate(heads, axis=1))
        od_ref[pl.ds(t0, SUBLANES), :] = jnp.concatenate(out_rows, axis=0)
        return carry

    lax.fori_loop(0, c // SUBLANES, body, 0)
    o_ref[rows, :] += od_ref[...]


def _scan_slow_fixes(fast_ref, q_ref, v_ref, f_ref, o_ref, od_ref, slow_refs, rev):
    def body(ci, carry):
        @pl.when(fast_ref[ci] == 0)
        def _():
            rows = pl.ds(pl.multiple_of(ci * SCAN_CHUNK, SCAN_CHUNK), SCAN_CHUNK)
            _scan_slow_fix(q_ref, v_ref, f_ref, rows, o_ref, od_ref, slow_refs, rev)
        return carry

    lax.fori_loop(0, fast_ref.shape[0], body, 0)


def _state_init(st_ref, s0_ref):
    if s0_ref is None:
        st_ref[...] = jnp.zeros_like(st_ref)
    else:
        for h in range(HEADS):
            o = (h % 2) * HEAD_W
            st_ref[h // 2, :, o:o + HEAD_W] = s0_ref[h].T


def _state_write(st_ref, out_ref):
    for h in range(HEADS):
        o = (h % 2) * HEAD_W
        out_ref[h] = st_ref[h // 2, :, o:o + HEAD_W].T


def _lower_bound(lbl_ref, layer, direction):
    n_layers = lbl_ref.shape[0] // 2
    rows = [lbl_ref[2 * i + direction:2 * i + direction + 1, :] for i in range(n_layers)]
    m = functools.reduce(jnp.maximum, rows)
    ex = [jnp.exp(r - m) for r in rows]
    return sum(ex[:layer + 1]) / sum(ex)


def _chunk_rows(n_rows, rev):
    order = range(n_rows // SCAN_CHUNK)
    return [slice(ci * SCAN_CHUNK, (ci + 1) * SCAN_CHUNK) for ci in (reversed(order) if rev else order)]


def _scan_scratch(d):
    c = SCAN_CHUNK
    return [
        pltpu.VMEM((d // PAIR_W, HEAD_W, PAIR_W), F32),
        pltpu.VMEM((c, d), F32),
        pltpu.VMEM((c, d), F32), pltpu.VMEM((c, d), F32),
        pltpu.VMEM((c, d), F32), pltpu.VMEM((c, d), F32),
    ]


def _const_spec(op):
    if isinstance(op, tuple):
        arr, i = op
        return pl.BlockSpec((None,) + arr.shape[1:], lambda *_: (i,) + (0,) * (arr.ndim - 1),
                            pipeline_mode=pl.Buffered(1))
    return pl.BlockSpec(op.shape, lambda *_: (0,) * op.ndim, pipeline_mode=pl.Buffered(1))


def _const_arg(op):
    return op[0] if isinstance(op, tuple) else op


def _hg_fwd_kernel(*refs, has_s0, layer, n_cast):
    if has_s0:
        x_ref, mod_ref, g_ref, w_ref, lbl_ref, s0_ref = refs[:6]
        rest = refs[6:]
    else:
        x_ref, mod_ref, g_ref, w_ref, lbl_ref = refs[:5]
        s0_ref = None
        rest = refs[5:]
    cast_in, rest = rest[:n_cast], rest[n_cast:]
    of_ref, q_ref, v_ref, fb_ref, gate_ref, sf_ref = rest[:6]
    cast_out, rest = rest[6:6 + n_cast], rest[6 + n_cast:]
    st_ref, od_ref = rest[0:2]
    slow_refs = rest[2:6]
    ff_ref, fast_ref = rest[6:8]
    j = pl.program_id(1)
    d = x_ref.shape[-1]
    for src_ref, dst_ref in zip(cast_in, cast_out):
        dst_ref[...] = src_ref[...].astype(BF16)

    @pl.when(j == 0)
    def _():
        _state_init(st_ref, s0_ref)

    sh, sc = mod_ref[0:1, :], mod_ref[1:2, :]
    h = (_rms(x_ref[...], g_ref[...]) * (1.0 + sc) + sh).astype(BF16)
    proj = lambda i: _dot(h, w_ref[:, i * d:(i + 1) * d])
    lb_f = _lower_bound(lbl_ref, layer, 0)
    lb_b = _lower_bound(lbl_ref, layer, 1)
    q = _silu(proj(0))
    f_f = _forget_gate(proj(1), lb_f)
    q_ref[...] = q.astype(BF16)
    v_ref[...] = proj(3).astype(BF16)
    ff_ref[...] = f_f
    k_f = 1.0 - f_f
    g_f = jnp.log2(f_f)

    chunks = _chunk_rows(x_ref.shape[0], rev=False)
    n_pieces = max(1, len(chunks) // 2)
    piece_w = d // n_pieces

    def backward_gate(p):
        cols = slice(p * piece_w, (p + 1) * piece_w)
        z = _dot(h, w_ref[:, 2 * d + p * piece_w:2 * d + (p + 1) * piece_w])
        fb_ref[:, cols] = _forget_gate(z, lb_b[:, cols])

    def output_gate(p):
        cols = slice(p * piece_w, (p + 1) * piece_w)
        z = _dot(h, w_ref[:, 4 * d + p * piece_w:4 * d + (p + 1) * piece_w])
        gate_ref[:, cols] = _silu(z).astype(BF16)

    fillers = [functools.partial(f, p) for f in (backward_gate, output_gate) for p in range(n_pieces)]
    for rows in chunks:
        operands, fast = _scan_prep(q[rows], k_f[rows], g_f[rows], rev=False)
        fast_ref[rows.start // SCAN_CHUNK] = fast.astype(jnp.int32)
        if fillers:
            fillers.pop(0)()
        of_ref[rows, :] = _scan_finish(operands, v_ref[rows, :], st_ref, rev=False)
    for filler in fillers:
        filler()
    _scan_slow_fixes(fast_ref, q_ref, v_ref, ff_ref, of_ref, od_ref, slow_refs, rev=False)

    @pl.when(j == pl.num_programs(1) - 1)
    def _():
        _state_write(st_ref, sf_ref)


def _hg_fwd_call(x, mod, mod_row0, mod_stride, g, w_in, lbl, state, layer_a, casts):
    bn, l, d = x.shape
    c = min(HG_TOKEN_BLOCK, l)
    assert l % c == 0 and c % SCAN_CHUNK == 0
    n = l // c
    tok = lambda b, j: (b, j, 0)
    steps = bn * n
    cast_args, cast_in_specs, cast_out_specs, cast_out_shapes = [], [], [], []
    for w, w_layer in casts:
        _, rows, cols = w.shape
        blk = rows // steps
        assert rows % steps == 0 and blk % BF16_SUBLANES == 0
        cast_args.append(w.reshape(w.shape[0], steps, blk, cols))
        cast_in_specs.append(pl.BlockSpec((None, None, blk, cols),
                                          lambda b, j, w_layer=w_layer: (w_layer, b * n + j, 0, 0)))
        cast_out_specs.append(pl.BlockSpec((None, blk, cols), lambda b, j: (b * n + j, 0, 0)))
        cast_out_shapes.append(jax.ShapeDtypeStruct((steps, blk, cols), BF16))
    in_specs = [
        pl.BlockSpec((None, c, d), tok),
        pl.BlockSpec((None, N_MOD, d), lambda b, j: (mod_row0 + mod_stride * b, 0, 0)),
        _const_spec(g), _const_spec(w_in), _const_spec(lbl),
    ]
    args = [x, mod, g, _const_arg(w_in), lbl]
    if state is not None:
        in_specs.append(pl.BlockSpec((None, None, None, HEADS, HEAD_W, HEAD_W),
                                     lambda b, j: (b, layer_a, 0, 0, 0, 0)))
        args.append(state)
    st_spec = pl.BlockSpec((None, HEADS, HEAD_W, HEAD_W), lambda b, j: (b, 0, 0, 0))
    outs = pl.pallas_call(
        functools.partial(_hg_fwd_kernel, has_s0=state is not None, layer=layer_a, n_cast=len(casts)),
        grid=(bn, n),
        in_specs=in_specs + cast_in_specs,
        out_specs=[pl.BlockSpec((None, c, d), tok)] * 5 + [st_spec] + cast_out_specs,
        out_shape=[
            jax.ShapeDtypeStruct((bn, l, d), F32),
            jax.ShapeDtypeStruct((bn, l, d), BF16),
            jax.ShapeDtypeStruct((bn, l, d), BF16),
            jax.ShapeDtypeStruct((bn, l, d), F32),
            jax.ShapeDtypeStruct((bn, l, d), BF16),
            jax.ShapeDtypeStruct((bn, HEADS, HEAD_W, HEAD_W), F32),
        ] + cast_out_shapes,
        scratch_shapes=_scan_scratch(d) + [
            pltpu.VMEM((c, d), F32),
            pltpu.SMEM((c // SCAN_CHUNK,), jnp.int32),
        ],
        compiler_params=pltpu.CompilerParams(
            dimension_semantics=("arbitrary", "arbitrary"), vmem_limit_bytes=VMEM_LIMIT),
        name="hg_fwd",
    )(*args, *cast_args)
    cast_w = [o.reshape(w.shape[1:]) for o, (w, _) in zip(outs[6:], casts)]
    return list(outs[:6]) + cast_w


def _mlp_tail(x1, mod_ref, g2, w1_ref, w2_ref):
    sh2, sc2, gt2 = mod_ref[3:4, :], mod_ref[4:5, :], mod_ref[5:6, :]
    h2 = (_rms(x1, g2) * (1.0 + sc2) + sh2).astype(BF16)
    a = jnp.maximum(_dot(h2, w1_ref[...]), 0.0)
    return x1 + gt2 * _dot((a * a).astype(BF16), w2_ref[...])


def _hg_bwd_kernel(*refs, has_s0, final, n_blocks):
    x_ref, mod_ref, gate_ref, of_ref, q_ref, v_ref, fb_ref, on_ref, g2_ref, fg_ref = refs[:10]
    wo_ref, w1_ref, w2_ref, sf_ref = refs[10:14]
    if has_s0:
        s0_ref = refs[14]
        rest = refs[15:]
    else:
        s0_ref = None
        rest = refs[14:]
    y_ref, sfb_ref = rest[:2]
    st_ref, od_ref = rest[2:4]
    slow_refs = rest[4:8]
    o_ref, fast_ref = rest[8:10]
    s = pl.program_id(0)
    total = pl.num_programs(0) - 1
    live = s < total
    j = jnp.minimum(s, total - 1) % n_blocks
    d = x_ref.shape[-1]

    @pl.when(live & (j == 0))
    def _():
        _state_init(st_ref, s0_ref)

    chunk_rows = _chunk_rows(q_ref.shape[0], rev=True)
    n_slabs = len(chunk_rows)
    slab_w = w1_ref.shape[1] // n_slabs

    def step(finishing):
        if finishing:
            o = o_ref[...]
            parts = []
            for h in range(d // HEAD_W):
                oh = o[:, h * HEAD_W:(h + 1) * HEAD_W]
                parts.append(oh * lax.rsqrt(jnp.mean(oh * oh, axis=-1, keepdims=True) + EPS))
            o = jnp.concatenate(parts, axis=1) * on_ref[...] * gate_ref[...].astype(F32)
            gt1 = mod_ref[2:3, :]
            x1 = x_ref[...] + gt1 * _dot(o.astype(BF16), wo_ref[...])
            sh2, sc2, gt2 = mod_ref[3:4, :], mod_ref[4:5, :], mod_ref[5:6, :]
            h2 = (_rms(x1, g2_ref[...]) * (1.0 + sc2) + sh2).astype(BF16)

        def mlp_slab(i):
            a = jnp.maximum(_dot(h2, w1_ref[:, i * slab_w:(i + 1) * slab_w]), 0.0)
            return _dot((a * a).astype(BF16), w2_ref[i * slab_w:(i + 1) * slab_w, :])

        mlp = 0.0
        for ci, rows in enumerate(chunk_rows):
            f_b = fb_ref[rows, :]
            operands, fast = _scan_prep(q_ref[rows, :].astype(F32), 1.0 - f_b, jnp.log2(f_b), rev=True)
            fast_ref[rows.start // SCAN_CHUNK] = fast.astype(jnp.int32)
            if finishing:
                mlp = mlp + mlp_slab(ci)
            o_ref[rows, :] = of_ref[rows, :] + _scan_finish(operands, v_ref[rows, :], st_ref, rev=True)
        if finishing:
            x2 = x1 + gt2 * mlp
            y_ref[...] = _rms(x2, fg_ref[...]) if final else x2

    pl.when(s == 0)(lambda: step(finishing=False))
    pl.when(s > 0)(lambda: step(finishing=True))
    _scan_slow_fixes(fast_ref, q_ref, v_ref, fb_ref, o_ref, od_ref, slow_refs, rev=True)

    @pl.when(live & (j == n_blocks - 1))
    def _():
        sfb_ref[0] = sf_ref[...]
        _state_write(st_ref, sfb_ref.at[1])


def _hg_bwd_call(x, mod, mod_row0, mod_stride, stash, s_f, onorm, g2, fg, w_out, w1, w2, state, layer_a, final):
    bn, l, d = x.shape
    c = min(HG_TOKEN_BLOCK, l)
    assert l % c == 0 and c % SCAN_CHUNK == 0
    n = l // c
    total = bn * n

    def cur(s):
        t = jnp.minimum(s, total - 1)
        return t // n, n - 1 - t % n

    def prev(s):
        t = jnp.maximum(s - 1, 0)
        return t // n, n - 1 - t % n

    tok_cur = lambda s: (*cur(s), 0)
    tok_prev = lambda s: (*prev(s), 0)
    o_f, q, v, f_b, gate = stash
    consts = [onorm, g2, fg, w_out, w1, w2]
    in_specs = [
        pl.BlockSpec((None, c, d), tok_prev),
        pl.BlockSpec((None, N_MOD, d), lambda s: (mod_row0 + mod_stride * prev(s)[0], 0, 0)),
        pl.BlockSpec((None, c, d), tok_prev),
    ] + [pl.BlockSpec((None, c, d), tok_cur)] * 4 + [_const_spec(op) for op in consts]
    in_specs.append(pl.BlockSpec((None, HEADS, HEAD_W, HEAD_W), lambda s: (cur(s)[0], 0, 0, 0)))
    args = [x, mod, gate, o_f, q, v, f_b] + [_const_arg(op) for op in consts] + [s_f]
    if state is not None:
        in_specs.append(pl.BlockSpec((None, None, None, HEADS, HEAD_W, HEAD_W),
                                     lambda s: (cur(s)[0], layer_a, 1, 0, 0, 0)))
        args.append(state)
    st_spec = pl.BlockSpec((None, 2, HEADS, HEAD_W, HEAD_W), lambda s: (cur(s)[0], 0, 0, 0, 0))
    return pl.pallas_call(
        functools.partial(_hg_bwd_kernel, has_s0=state is not None, final=final, n_blocks=n),
        grid=(total + 1,),
        in_specs=in_specs,
        out_specs=[pl.BlockSpec((None, c, d), tok_prev), st_spec],
        out_shape=[
            jax.ShapeDtypeStruct((bn, l, d), F32),
            jax.ShapeDtypeStruct((bn, 2, HEADS, HEAD_W, HEAD_W), F32),
        ],
        scratch_shapes=_scan_scratch(d) + [
            pltpu.VMEM((c, d), F32),
            pltpu.SMEM((c // SCAN_CHUNK,), jnp.int32),
        ],
        compiler_params=pltpu.CompilerParams(
            dimension_semantics=("arbitrary",), vmem_limit_bytes=VMEM_LIMIT),
        name="hg_bwd",
    )(*args)


def _cm_kernel(x_ref, mod_ref, g1_ref, g2_ref, fg_ref, win_ref, lng_ref, lnb_ref, ws_ref, bst_ref,
               wo_ref, w1_ref, w2_ref, y_ref, *, final):
    tb, d = x_ref.shape
    n_chunks = tb // MIX_CHUNK
    gw = d // CM_GROUPS
    x = x_ref[...]
    sh, sc, gt1 = mod_ref[0:1, :], mod_ref[1:2, :], mod_ref[2:3, :]
    h = (_rms(x, g1_ref[...]) * (1.0 + sc) + sh).astype(BF16)
    u = _gelu_tanh(_dot(h, win_ref[:, :d]))
    vv = _gelu_tanh(_dot(h, win_ref[:, d:]))
    mu = jnp.mean(vv, axis=-1, keepdims=True)
    vc = vv - mu
    var = jnp.mean(vc * vc, axis=-1, keepdims=True)
    vn = (vc * lax.rsqrt(var + EPS) * lng_ref[...] + lnb_ref[...]).astype(BF16)
    per_group = []
    for g in range(CM_GROUPS):
        rhs = jnp.concatenate(
            [vn[ch * MIX_CHUNK:(ch + 1) * MIX_CHUNK, g * gw:(g + 1) * gw] for ch in range(n_chunks)], axis=1)
        per_group.append(_dot(ws_ref[g], rhs) + bst_ref[:, g:g + 1])
    s = jnp.concatenate(
        [jnp.concatenate([pg[:, ch * gw:(ch + 1) * gw] for pg in per_group], axis=1) for ch in range(n_chunks)],
        axis=0)
    x1 = x + gt1 * _dot((u * s).astype(BF16), wo_ref[...])
    x2 = _mlp_tail(x1, mod_ref, g2_ref[...], w1_ref, w2_ref)
    y_ref[...] = _rms(x2, fg_ref[...]) if final else x2


def _cm_call(x, mod, mod_row0, mod_stride, g1, g2, fg, w_in, ln_g, ln_b, w_s, b_st, w_out, w1, w2, final):
    out_shape = x.shape
    if mod_stride == 0 and x.shape[1] % MIX_CHUNK == 0:
        x = x.reshape(1, -1, x.shape[-1])
    bn, l, d = x.shape
    tb = min(CM_TOKEN_BLOCK, l)
    assert l % tb == 0 and tb % MIX_CHUNK == 0
    tok = lambda b, j: (b, j, 0)
    consts = [g1, g2, fg, w_in, ln_g, ln_b, w_s, b_st, w_out, w1, w2]
    return pl.pallas_call(
        functools.partial(_cm_kernel, final=final),
        grid=(bn, l // tb),
        in_specs=[
            pl.BlockSpec((None, tb, d), tok),
            pl.BlockSpec((None, N_MOD, d), lambda b, j: (mod_row0 + mod_stride * b, 0, 0)),
        ] + [_const_spec(op) for op in consts],
        out_specs=pl.BlockSpec((None, tb, d), tok),
        out_shape=jax.ShapeDtypeStruct((bn, l, d), F32),
        compiler_params=pltpu.CompilerParams(
            dimension_semantics=("arbitrary", "arbitrary"), vmem_limit_bytes=VMEM_LIMIT),
        name="cm",
    )(x, mod, *[_const_arg(op) for op in consts]).reshape(out_shape)


def kernel(x_prompt, x_sample, state_hgrn, c, c_ctx, ada_w, ada_b, norm_mix_g, norm_mlp_g, mlp_w1, mlp_w2,
           hgrn_w_in, hgrn_lb_logits, hgrn_onorm_g, hgrn_w_out, cm_w_in, cm_ln_g, cm_ln_b, cm_w_s, cm_b_s,
           cm_w_out, final_norm_g):
    depth, d, _ = ada_w.shape
    n_lat = c.shape[0]
    assert d % PAIR_W == 0 and d // HEAD_W == HEADS and 1 + n_lat <= COND_ROWS
    assert x_prompt.shape[1] % SCAN_CHUNK == 0 and x_sample.shape[1] % SCAN_CHUNK == 0

    cond = jnp.concatenate([c_ctx[None, :], c, jnp.zeros((COND_ROWS - 1 - n_lat, d), F32)], axis=0)
    mods = _ada_call(cond, ada_w, ada_b).reshape(depth, COND_ROWS, N_MOD, d)

    row = lambda a: a.reshape(1, d)
    lbl = hgrn_lb_logits.reshape(-1, d)
    fg = row(final_norm_g)
    stacks = dict(mlp_w1=mlp_w1, mlp_w2=mlp_w2, hgrn_w_out=hgrn_w_out, cm_w_in=cm_w_in, cm_w_out=cm_w_out)
    ready = {}

    def weight(name, layer):
        if (name, layer) in ready:
            return ready[name, layer]
        if not isinstance(stacks[name], tuple):
            stacks[name] = (stacks[name].astype(BF16),)
        return (stacks[name][0], layer)

    def side_casts(call, names_layers):
        names_layers = [(nm, ly) for nm, ly in names_layers if not isinstance(stacks[nm], tuple)]
        outs = call([(stacks[nm], ly) for nm, ly in names_layers])
        for (nm, ly), w16 in zip(names_layers, outs[6:]):
            ready[nm, ly] = w16
        return outs[:6]

    hgrn_w_in, cm_w_s = hgrn_w_in.astype(BF16), cm_w_s.astype(BF16)
    ctx, lat = x_prompt, x_sample
    new_states = []
    for i in range(depth):
        j = i // N_MIXERS
        final = i == depth - 1
        g1, g2 = row(norm_mix_g[i]), row(norm_mlp_g[i])
        if i % N_MIXERS == 0:
            w_in = (hgrn_w_in, j)
            onorm = row(hgrn_onorm_g[j])
            own = [("hgrn_w_out", j), ("mlp_w1", i), ("mlp_w2", i)]
            nxt = []
            if i + 1 < depth and (i + 1) % N_MIXERS == 1:
                jn = (i + 1) // N_MIXERS
                nxt = [("cm_w_in", jn), ("cm_w_out", jn), ("mlp_w1", i + 1), ("mlp_w2", i + 1)]
            *stash, s_f = side_casts(
                functools.partial(_hg_fwd_call, ctx, mods[i], 0, 0, g1, w_in, lbl, None, j), own)
            w_out, w1, w2 = weight("hgrn_w_out", j), weight("mlp_w1", i), weight("mlp_w2", i)
            ctx, s_fb = _hg_bwd_call(ctx, mods[i], 0, 0, stash, s_f, onorm, g2, fg, w_out, w1, w2, None, j, final)
            new_states.append(s_fb)
            *stash, s_f = side_casts(
                functools.partial(_hg_fwd_call, lat, mods[i], 1, 1, g1, w_in, lbl, state_hgrn, j), nxt)
            lat, _ = _hg_bwd_call(lat, mods[i], 1, 1, stash, s_f, onorm, g2, fg, w_out, w1, w2, state_hgrn, j, final)
        else:
            cm_args = (g1, g2, fg, weight("cm_w_in", j), row(cm_ln_g[j]), row(cm_ln_b[j]),
                       (cm_w_s, j), cm_b_s[j].T, weight("cm_w_out", j), weight("mlp_w1", i), weight("mlp_w2", i),
                       final)
            ctx = _cm_call(ctx, mods[i], 0, 0, *cm_args)
            lat = _cm_call(lat, mods[i], 1, 1, *cm_args)
    new_state = jnp.stack(new_states, axis=1).astype(x_prompt.dtype)
    return ctx, lat, new_state
```
